```python
import math
import jax, jax.numpy as jnp
from jax import lax
import numpy as np

D_MODEL = 1024
BATCH = 32
SEQ = 2048
DEPTH = 2
DEC_BATCH = 8
DEC_SEQ = 64
PAST_LEN = 1024

CHUNK = 64
N_META = 16
QBLOCK = 128
N_AB = (DEPTH + 1) // 2
N_C = DEPTH // 2
H_A = 4
DH_A = D_MODEL // (2 * H_A)
W_A = H_A * DH_A
CONV_W = 4
H_B = 4
DH_B = D_MODEL // (4 * H_B)
W_B = H_B * 2 * DH_B
H_C = 16
DH_C = D_MODEL // H_C
SB_SUB = 64
N_BUCKETS = 32
MAX_DISTANCE = 128
N_EXPERTS = 32
TOP_K = 4
D_EXPERT = D_MODEL // 2
MOE_BLOCK = 1024
SWIGLU_LIMIT = 7.0
SWIGLU_ALPHA = 1.702
DN_ALPHA = (2 * DEPTH) ** 0.25
DN_BETA = (8 * DEPTH) ** -0.25
LN_EPS = 1e-5
OFF_AX = 0
OFF_AV = OFF_AX + W_A
OFF_AO = OFF_AV + W_A
OFF_AIF = OFF_AO + W_A
OFF_BQ = OFF_AIF + 2 * H_A
OFF_BK = OFF_BQ + W_B
OFF_BV = OFF_BK + W_B
D_IN_AB = OFF_BV + W_B

kernel_name = 'hybrid_streaming_mlstm_diffattn_stickbreaking_moe'

F32 = jnp.float32


def layer_norm(x, g, b):
    xf = x.astype(F32)
    mu = jnp.mean(xf, -1, keepdims=True)
    var = jnp.mean(jnp.square(xf - mu), -1, keepdims=True)
    return ((xf - mu) * lax.rsqrt(var + LN_EPS) * g + b).astype(x.dtype)


def rms_norm(x, g):
    xf = x.astype(F32)
    return xf * lax.rsqrt(jnp.mean(xf * xf, -1, keepdims=True) + LN_EPS) * g


def split_heads(t, h):
    b, l, _ = t.shape
    return t.reshape(b, l, h, -1).transpose(0, 2, 1, 3)


def merge_heads(t):
    b, h, l, d = t.shape
    return t.transpose(0, 2, 1, 3).reshape(b, l, h * d)


def rel_bucket(rel):
    half = N_BUCKETS // 2
    exact = half // 2
    ret = jnp.where(rel > 0, half, 0)
    n = jnp.abs(rel)
    large = exact + (jnp.log(jnp.maximum(n, 1).astype(F32) / exact)
                     / math.log(MAX_DISTANCE / exact) * (half - exact)).astype(jnp.int32)
    large = jnp.minimum(large, half - 1)
    return ret + jnp.where(n < exact, n, large)


def causal_conv(x, buf, w, b):
    xp = jnp.concatenate([buf.astype(x.dtype), x], axis=1)
    l = x.shape[1]
    y = b
    for j in range(CONV_W):
        y = y + xp[:, j:j + l] * w[j]
    return y, xp[:, xp.shape[1] - (CONV_W - 1):]


def mlstm_chunk(state, inp):
    C, n, m = state
    q, k, v, ig, lf = inp
    l = q.shape[2]
    b = jnp.cumsum(lf, axis=-1)
    causal = jnp.tril(jnp.ones((l, l), bool))
    dmat = jnp.where(causal, b[..., :, None] - b[..., None, :] + ig[..., None, :], -jnp.inf)
    inter = b + m[..., None]
    m_t = jnp.maximum(inter, jnp.max(dmat, -1))
    w = jnp.exp(dmat - m_t[..., None])
    g = jnp.exp(inter - m_t)
    s = jnp.einsum('bhtd,bhsd->bhts', q, k) * w
    num = jnp.einsum('bhts,bhsd->bhtd', s, v) + g[..., None] * jnp.einsum('bhtk,bhkv->bhtv', q, C)
    den = jnp.sum(s, -1) + g * jnp.einsum('bhtk,bhk->bht', q, n)
    h = num / jnp.maximum(jnp.abs(den), jnp.exp(-m_t))[..., None]
    m_new = m_t[..., -1]
    w_end = jnp.exp(b[..., -1:] - b + ig - m_new[..., None])
    decay = jnp.exp(b[..., -1] + m - m_new)
    C_new = decay[..., None, None] * C + jnp.einsum('bhsk,bhsv->bhkv', k * w_end[..., None], v)
    n_new = decay[..., None] * n + jnp.einsum('bhs,bhsk->bhk', w_end, k)
    return (C_new, n_new, m_new), h


def mlstm_run(q, k, v, ig, lf, state, lead):
    hs = []
    if lead:
        state, h0 = mlstm_chunk(state, (q[:, :, :lead], k[:, :, :lead], v[:, :, :lead],
                                        ig[:, :, :lead], lf[:, :, :lead]))
        hs.append(h0)
    rest = q.shape[2] - lead
    if rest <= CHUNK:
        state, h1 = mlstm_chunk(state, (q[:, :, lead:], k[:, :, lead:], v[:, :, lead:],
                                        ig[:, :, lead:], lf[:, :, lead:]))
    else:
        nc = rest // CHUNK

        def to_chunks(a):
            a = a[:, :, lead:]
            a = a.reshape(a.shape[:2] + (nc, CHUNK) + a.shape[3:])
            return jnp.moveaxis(a, 2, 0)

        state, hc = lax.scan(mlstm_chunk, state, (to_chunks(q), to_chunks(k), to_chunks(v),
                                                  to_chunks(ig), to_chunks(lf)))
        hc = jnp.moveaxis(hc, 0, 2)
        h1 = hc.reshape(hc.shape[:2] + (nc * CHUNK, hc.shape[-1]))
    hs.append(h1)
    return state, jnp.concatenate(hs, axis=2)


def diff_attention(q, k, v, q_pos, q_chk, k_pos, k_chk, blocks, rel_table, lam, subln_g, lam_init):
    bsz = q.shape[0]
    outs = []
    for qs, qe, ke in blocks:
        qb = q[:, :, qs:qe].reshape(bsz, H_B, qe - qs, 2, DH_B)
        kb = k[:, :, :ke].reshape(bsz, H_B, ke, 2, DH_B)
        s = jnp.einsum('bhqmd,bhkmd->bhmqk', qb, kb).astype(F32) * DH_B ** -0.5
        rel = k_pos[None, :ke] - q_pos[qs:qe, None]
        bias = jnp.transpose(rel_table[rel_bucket(rel)], (2, 0, 1)).astype(F32)
        mask = k_chk[None, :ke] <= q_chk[qs:qe, None]
        s = jnp.where(mask, s + bias[None, :, None], -jnp.inf)
        p = jax.nn.softmax(s, axis=-1)
        a = p[:, :, 0] - lam * p[:, :, 1]
        outs.append(jnp.einsum('bhqk,bhkd->bhqd', a, v[:, :, :ke].astype(F32)))
    o = jnp.concatenate(outs, axis=2)
    return rms_norm(o, subln_g) * (1.0 - lam_init)


def rev_excl_sum(a):
    kk = a.shape[-1]
    kp = -(-kk // SB_SUB) * SB_SUB
    a = jnp.pad(a, [(0, 0)] * (a.ndim - 1) + [(0, kp - kk)])
    nsub = kp // SB_SUB
    ab = a.reshape(a.shape[:-1] + (nsub, SB_SUB))
    idx = jnp.arange(SB_SUB)
    upper = (idx[:, None] > idx[None, :]).astype(a.dtype)
    within = jnp.einsum('...nj,js->...ns', ab, upper, precision=lax.Precision.HIGHEST)
    bidx = jnp.arange(nsub)
    later_m = (bidx[:, None] > bidx[None, :]).astype(a.dtype)
    later = jnp.einsum('...m,mn->...n', jnp.sum(ab, -1), later_m, precision=lax.Precision.HIGHEST)
    out = (within + later[..., None]).reshape(a.shape[:-1] + (kp,))
    return out[..., :kk]


def stick_breaking(q, k, v, q_pos, k_pos, blocks):
    outs = []
    for qs, qe, ke in blocks:
        z = jnp.einsum('bhqd,bhkd->bhqk', q[:, :, qs:qe], k[:, :, :ke]).astype(F32) * DH_C ** -0.5
        mask = k_pos[None, :ke] < q_pos[qs:qe, None]
        log_stay = jnp.where(mask, jax.nn.log_sigmoid(-z), 0.0)
        log_after = rev_excl_sum(log_stay)
        a = jnp.where(mask, jnp.exp(jax.nn.log_sigmoid(z) + log_after), 0.0)
        outs.append(jnp.einsum('bhqk,bhkd->bhqd', a, v[:, :, :ke].astype(F32)))
    return jnp.concatenate(outs, axis=2)


def moe(x, w_r, b_r, w_gu, b_gu, w_dn, b_dn):
    bsz, l, d = x.shape
    xt = x.reshape(-1, d)
    n_tok = xt.shape[0]
    n_asg = n_tok * TOP_K
    blk = max(1, min(MOE_BLOCK, n_asg // N_EXPERTS))
    nb = -(-n_asg // blk) + N_EXPERTS
    logits = (xt @ w_r).astype(F32) + b_r
    top_v, top_i = lax.top_k(logits, TOP_K)
    gates = jax.nn.softmax(top_v, axis=-1).reshape(-1)
    flat_e = top_i.reshape(-1)
    order = jnp.argsort(flat_e)
    sizes = jnp.bincount(flat_e, length=N_EXPERTS).astype(jnp.int32)
    starts = jnp.cumsum(sizes) - sizes
    nblk = (sizes + blk - 1) // blk
    pad_starts = ((jnp.cumsum(nblk) - nblk) * blk).astype(jnp.int32)
    blk_e = jnp.clip(jnp.searchsorted(pad_starts, jnp.arange(nb, dtype=jnp.int32) * blk, side='right') - 1,
                     0, N_EXPERTS - 1)
    row_e = jnp.repeat(blk_e, blk)
    local = jnp.arange(nb * blk, dtype=jnp.int32) - pad_starts[row_e]
    valid = local < sizes[row_e]
    slot = order[jnp.where(valid, starts[row_e] + local, 0)]
    tok = jnp.where(valid, slot // TOP_K, n_tok)
    xb = jnp.where(valid[:, None], xt[jnp.minimum(tok, n_tok - 1)], 0.0).astype(xt.dtype).reshape(nb, blk, d)
    h = jnp.einsum('nbd,ndf->nbf', xb, w_gu[blk_e]) + b_gu[blk_e][:, None]
    glu = jnp.minimum(h[..., 0::2], SWIGLU_LIMIT)
    lin = jnp.clip(h[..., 1::2], -SWIGLU_LIMIT, SWIGLU_LIMIT)
    act = glu * jax.nn.sigmoid(SWIGLU_ALPHA * glu) * (lin + 1.0)
    out = jnp.einsum('nbf,nfd->nbd', act, w_dn[blk_e]) + b_dn[blk_e][:, None]
    out = out.reshape(-1, d) * gates[slot][:, None].astype(out.dtype)
    y = jnp.zeros_like(xt).at[tok].add(out.astype(xt.dtype), mode='drop')
    return y.reshape(bsz, l, d)


def ab_mixer(x, q_pos, q_chk, k_pos, k_chk, blocks, lead, past, p, j, lam_init):
    bsz = x.shape[0]
    h = x @ p['w_in_ab'][j]
    xa = h[..., OFF_AX:OFF_AV]
    va = h[..., OFF_AV:OFF_AO]
    oa = h[..., OFF_AO:OFF_AIF]
    gates = h[..., OFF_AIF:OFF_BQ].astype(F32) + p['b_if_a'][j]
    qb = h[..., OFF_BQ:OFF_BK]
    kb = h[..., OFF_BK:OFF_BV]
    vb = h[..., OFF_BV:D_IN_AB]
    if past is None:
        past_k = None
        past_v = None
        conv_buf = jnp.zeros((bsz, CONV_W - 1, W_A), x.dtype)
        state = (jnp.zeros((bsz, H_A, DH_A, DH_A), F32), jnp.zeros((bsz, H_A, DH_A), F32),
                 jnp.zeros((bsz, H_A), F32))
    else:
        past_k, past_v, c0, n0, m0, conv_buf = past
        state = (c0.astype(F32), n0.astype(F32), m0.astype(F32))
    xc, conv_state = causal_conv(xa, conv_buf, p['conv_w_a'][j], p['conv_b_a'][j])
    ca = split_heads(jax.nn.silu(xc), H_A).astype(F32)
    q_a = jnp.einsum('bhld,hde->bhle', ca, p['w_aq_a'][j].astype(F32))
    k_a = jnp.einsum('bhld,hde->bhle', ca, p['w_ak_a'][j].astype(F32)) * DH_A ** -0.5
    v_a = split_heads(va, H_A).astype(F32)
    ig = gates[..., :H_A].transpose(0, 2, 1)
    lf = jax.nn.log_sigmoid(gates[..., H_A:]).transpose(0, 2, 1)
    (c1, n1, m1), h_a = mlstm_run(q_a, k_a, v_a, ig, lf, state, lead)
    h_a = rms_norm(h_a, p['mh_gain_a'][j].reshape(H_A, 1, DH_A)) * jax.nn.sigmoid(split_heads(oa, H_A).astype(F32))
    q_b = split_heads(qb, H_B)
    k_new = split_heads(kb, H_B)
    v_new = split_heads(vb, H_B)
    if past_k is None:
        k_all = k_new
        v_all = v_new
    else:
        k_all = jnp.concatenate([past_k.astype(k_new.dtype), k_new], axis=2)
        v_all = jnp.concatenate([past_v.astype(v_new.dtype), v_new], axis=2)
    lam = (jnp.exp(jnp.sum(p['lam_q1'][j] * p['lam_k1'][j])) - jnp.exp(jnp.sum(p['lam_q2'][j] * p['lam_k2'][j]))).astype(F32) + lam_init
    o_b = diff_attention(q_b, k_all, v_all, q_pos, q_chk, k_pos, k_chk, blocks, p['rel_bias'], lam,
                         p['subln_gain_b'][j], lam_init)
    mixed = jnp.concatenate([merge_heads(h_a), merge_heads(o_b)], axis=-1).astype(x.dtype)
    y = mixed @ p['w_out_ab'][j]
    return y, (k_new, v_new, c1, n1, m1, conv_state)


def c_mixer(x, q_pos, k_pos, blocks, past, p, j):
    h = x @ p['w_in_c'][j]
    q = split_heads(h[..., :D_MODEL], H_C)
    k_new = split_heads(h[..., D_MODEL:2 * D_MODEL], H_C)
    v_new = split_heads(h[..., 2 * D_MODEL:], H_C)
    if past is None:
        k_all = k_new
        v_all = v_new
    else:
        k_all = jnp.concatenate([past[0].astype(k_new.dtype), k_new], axis=2)
        v_all = jnp.concatenate([past[1].astype(v_new.dtype), v_new], axis=2)
    o = stick_breaking(q, k_all, v_all, q_pos, k_pos, blocks)
    y = merge_heads(o).astype(x.dtype) @ p['w_out_c'][j]
    return y, (k_new, v_new)


def run_trunk(x, q_pos, q_chk, k_pos, k_chk, ab_blocks, c_blocks, lead, past_ab, past_c, p):
    ab_new = []
    c_new = []
    for l in range(DEPTH):
        j = l // 2
        if l % 2 == 0:
            past = None if past_ab is None else [a[j] for a in past_ab]
            y, st = ab_mixer(x, q_pos, q_chk, k_pos, k_chk, ab_blocks, lead, past, p, j,
                             0.8 - 0.6 * math.exp(-0.3 * l))
            ab_new.append(st)
        else:
            past = None if past_c is None else [a[j] for a in past_c]
            y, st = c_mixer(x, q_pos, k_pos, c_blocks, past, p, j)
            c_new.append(st)
        x = layer_norm(DN_ALPHA * x + y, p['ln_g'][l, 0], p['ln_b'][l, 0])
        f = moe(x, p['w_router'][l], p['b_router'][l], p['w_gu'][l], p['b_gu'][l], p['w_down'][l], p['b_down'][l])
        x = layer_norm(DN_ALPHA * x + f, p['ln_g'][l, 1], p['ln_b'][l, 1])
    ab_out = [jnp.stack([s[i] for s in ab_new]) for i in range(6)]
    c_out = [jnp.stack([s[i] for s in c_new]) for i in range(2)]
    return x, ab_out, c_out


def setup_inputs(seed: int = 0) -> dict:
    key = jax.random.key(seed)
    keys = jax.random.split(key, 64)
    counter = [0]

    def nrm(shape, scale):
        kk = keys[counter[0]]
        counter[0] += 1
        return jax.random.normal(kk, shape, F32) * scale

    vscale = jnp.ones((D_IN_AB,), F32).at[OFF_AV:OFF_AO].set(DN_BETA).at[OFF_BV:D_IN_AB].set(DN_BETA)
    cscale = jnp.ones((3 * D_MODEL,), F32).at[2 * D_MODEL:].set(DN_BETA)
    b_if = jnp.concatenate([nrm((N_AB, H_A), 0.1),
                            jnp.linspace(3.0, 6.0, H_A, dtype=F32)[None] + nrm((N_AB, H_A), 0.1)], axis=-1)
    return {
        'x_prompt': nrm((BATCH, SEQ, D_MODEL), 1.0),
        'x_sample': nrm((DEC_BATCH, DEC_SEQ, D_MODEL), 1.0),
        'cache_diff_k': nrm((N_AB, DEC_BATCH, H_B, PAST_LEN, 2 * DH_B), 1.0),
        'cache_diff_v': nrm((N_AB, DEC_BATCH, H_B, PAST_LEN, 2 * DH_B), DN_BETA),
        'state_mlstm_C': nrm((N_AB, DEC_BATCH, H_A, DH_A, DH_A), 0.5),
        'state_mlstm_n': nrm((N_AB, DEC_BATCH, H_A, DH_A), 0.5),
        'state_mlstm_m': nrm((N_AB, DEC_BATCH, H_A), 0.5),
        'state_mlstm_conv': nrm((N_AB, DEC_BATCH, CONV_W - 1, W_A), 1.0),
        'cache_sb_k': nrm((N_C, DEC_BATCH, H_C, PAST_LEN, DH_C), 1.0),
        'cache_sb_v': nrm((N_C, DEC_BATCH, H_C, PAST_LEN, DH_C), DN_BETA),
        'meta_tokens': nrm((N_META, D_MODEL), 1.0),
        'rel_bias': nrm((N_BUCKETS, H_B), 0.5),
        'w_in_ab': nrm((N_AB, D_MODEL, D_IN_AB), D_MODEL ** -0.5) * vscale,
        'w_out_ab': nrm((N_AB, D_MODEL, D_MODEL), D_MODEL ** -0.5 * DN_BETA),
        'conv_w_a': nrm((N_AB, CONV_W, W_A), 0.5),
        'conv_b_a': nrm((N_AB, W_A), 0.02),
        'w_aq_a': nrm((N_AB, H_A, DH_A, DH_A), DH_A ** -0.5),
        'w_ak_a': nrm((N_AB, H_A, DH_A, DH_A), DH_A ** -0.5),
        'b_if_a': b_if,
        'mh_gain_a': 1.0 + nrm((N_AB, W_A), 0.02),
        'lam_q1': nrm((N_AB, DH_B), 0.1),
        'lam_k1': nrm((N_AB, DH_B), 0.1),
        'lam_q2': nrm((N_AB, DH_B), 0.1),
        'lam_k2': nrm((N_AB, DH_B), 0.1),
        'subln_gain_b': 1.0 + nrm((N_AB, 2 * DH_B), 0.02),
        'w_in_c': nrm((N_C, D_MODEL, 3 * D_MODEL), D_MODEL ** -0.5) * cscale,
        'w_out_c': nrm((N_C, D_MODEL, D_MODEL), D_MODEL ** -0.5 * DN_BETA),
        'ln_g': 1.0 + nrm((DEPTH, 2, D_MODEL), 0.02),
        'ln_b': nrm((DEPTH, 2, D_MODEL), 0.02),
        'w_router': nrm((DEPTH, D_MODEL, N_EXPERTS), D_MODEL ** -0.5),
        'b_router': nrm((DEPTH, N_EXPERTS), 0.01),
        'w_gu': nrm((DEPTH, N_EXPERTS, D_MODEL, 2 * D_EXPERT), D_MODEL ** -0.5 * DN_BETA),
        'b_gu': nrm((DEPTH, N_EXPERTS, 2 * D_EXPERT), 0.01),
        'w_down': nrm((DEPTH, N_EXPERTS, D_EXPERT, D_MODEL), D_EXPERT ** -0.5 * DN_BETA),
        'b_down': nrm((DEPTH, N_EXPERTS, D_MODEL), 0.01),
    }


def reference(x_prompt, x_sample, cache_diff_k, cache_diff_v, state_mlstm_C, state_mlstm_n, state_mlstm_m,
              state_mlstm_conv, cache_sb_k, cache_sb_v, meta_tokens, rel_bias, w_in_ab, w_out_ab, conv_w_a,
              conv_b_a, w_aq_a, w_ak_a, b_if_a, mh_gain_a, lam_q1, lam_k1, lam_q2, lam_k2, subln_gain_b,
              w_in_c, w_out_c, ln_g, ln_b, w_router, b_router, w_gu, b_gu, w_down, b_down):
    p = {'rel_bias': rel_bias, 'w_in_ab': w_in_ab, 'w_out_ab': w_out_ab, 'conv_w_a': conv_w_a,
         'conv_b_a': conv_b_a, 'w_aq_a': w_aq_a, 'w_ak_a': w_ak_a, 'b_if_a': b_if_a, 'mh_gain_a': mh_gain_a,
         'lam_q1': lam_q1, 'lam_k1': lam_k1, 'lam_q2': lam_q2, 'lam_k2': lam_k2, 'subln_gain_b': subln_gain_b,
         'w_in_c': w_in_c, 'w_out_c': w_out_c, 'ln_g': ln_g, 'ln_b': ln_b, 'w_router': w_router,
         'b_router': b_router, 'w_gu': w_gu, 'b_gu': b_gu, 'w_down': w_down, 'b_down': b_down}

    bsz = x_prompt.shape[0]
    lp = N_META + x_prompt.shape[1]
    x0 = jnp.concatenate([jnp.broadcast_to(meta_tokens[None], (bsz, N_META, D_MODEL)).astype(x_prompt.dtype),
                          x_prompt], axis=1)
    pos_p = jnp.arange(lp, dtype=jnp.int32)
    chk_p = jnp.where(pos_p < N_META, 0, (pos_p - N_META) // CHUNK + 1)
    ab_blocks_p = []
    c_blocks_p = []
    for qs in range(0, lp, QBLOCK):
        qe = min(qs + QBLOCK, lp)
        last = qe - 1
        c_last = 0 if last < N_META else (last - N_META) // CHUNK + 1
        ab_blocks_p.append((qs, qe, min(lp, N_META + c_last * CHUNK)))
        c_blocks_p.append((qs, qe, qe))
    yp, ab_p, c_p = run_trunk(x0, pos_p, chk_p, pos_p, chk_p, ab_blocks_p, c_blocks_p, N_META, None, None, p)
    y_prompt = yp[:, N_META:]
    p_diff_k, p_diff_v, p_mlstm_C, p_mlstm_n, p_mlstm_m, p_mlstm_conv = ab_p
    p_sb_k, p_sb_v = c_p

    past = cache_diff_k.shape[3]
    ls = x_sample.shape[1]
    lk = past + ls
    pos_k = jnp.arange(lk, dtype=jnp.int32)
    chk_k = pos_k // CHUNK + 1
    pos_q = pos_k[past:]
    chk_q = chk_k[past:]
    s_blocks = [(qs, min(qs + QBLOCK, ls), lk) for qs in range(0, ls, QBLOCK)]
    past_ab = [cache_diff_k, cache_diff_v, state_mlstm_C, state_mlstm_n, state_mlstm_m, state_mlstm_conv]
    past_c = [cache_sb_k, cache_sb_v]
    y_sample, ab_s, c_s = run_trunk(x_sample, pos_q, chk_q, pos_k, chk_k, s_blocks, s_blocks, 0,
                                    past_ab, past_c, p)
    s_diff_k, s_diff_v, s_mlstm_C, s_mlstm_n, s_mlstm_m, s_mlstm_conv = ab_s
    s_sb_k, s_sb_v = c_s

    return (y_prompt, y_sample, p_diff_k, p_diff_v, p_mlstm_C, p_mlstm_n, p_mlstm_m, p_mlstm_conv, p_sb_k, p_sb_v,
            s_diff_k, s_diff_v, s_mlstm_C, s_mlstm_n, s_mlstm_m, s_mlstm_conv, s_sb_k, s_sb_v)
```

```python
import functools
import math

import jax
import jax.numpy as jnp
from jax import lax
from jax.experimental import pallas as pl
from jax.experimental.pallas import tpu as pltpu

F32 = jnp.float32
BF16 = jnp.bfloat16

D_MODEL = 1024
DEPTH = 2
CHUNK = 64
N_META = 16
H_A = 4
DH_A = 128
W_A = H_A * DH_A
CONV_W = 4
H_B = 4
DH_B = 64
W_B = H_B * 2 * DH_B
H_C = 16
DH_C = 64
N_BUCKETS = 32
MAX_DISTANCE = 128
N_EXPERTS = 32
TOP_K = 4
D_EXPERT = D_MODEL // 2
SWIGLU_LIMIT = 7.0
SWIGLU_ALPHA = 1.702
DN_ALPHA = (2 * DEPTH) ** 0.25
LN_EPS = 1e-5
OFF_AIF = 3 * W_A
OFF_BQ = OFF_AIF + 2 * H_A

LANE = 128
KEY_TILE = 128
LOG2_KEY_TILE = 7
LOG2_CHUNK = 6
VMEM_LIMIT = 56 * 1024 * 1024
MOE_BLOCK = 512
TOKEN_TILE = 512
NEG_BIG = -1e30

COL_G = 3 * W_A
COL_Q = COL_G + LANE
COL_K = COL_Q + W_B
COL_V = COL_K + W_B
D_IN_AB_PAD = COL_V + W_B


def _dot(a, b):
    return jnp.dot(a, b, preferred_element_type=F32)


def _dot_nt(a, b, precision=None):
    return lax.dot_general(a, b, (((1,), (1,)), ((), ())), preferred_element_type=F32, precision=precision)


def _dot_tn(a, b):
    return lax.dot_general(a, b, (((0,), (0,)), ((), ())), preferred_element_type=F32)


def _log_sigmoid(x):
    return jnp.minimum(x, 0.0) - jnp.log(1.0 + jnp.exp(-jnp.abs(x)))


def _sigmoid(x):
    return 1.0 / (1.0 + jnp.exp(-x))


def _chunk_id(pos):
    return (pos + (CHUNK - N_META)) >> LOG2_CHUNK


def _params(sem):
    return pltpu.CompilerParams(dimension_semantics=sem, vmem_limit_bytes=VMEM_LIMIT)


def _inproj_ab_kernel(x_ref, w_ref, bg_ref, a_ref, g_ref, q_ref, k_ref, v_ref):
    xb = x_ref[...].astype(BF16)
    a_ref[...] = _dot(xb, w_ref[:, 0:COL_G])
    g_ref[...] = _dot(xb, w_ref[:, COL_G:COL_Q]) + bg_ref[...]
    q_ref[...] = _dot(xb, w_ref[:, COL_Q:COL_K]).astype(BF16)
    for h in range(H_B):
        k_ref[0, h] = _dot(xb, w_ref[:, COL_K + 2 * DH_B * h:COL_K + 2 * DH_B * (h + 1)])
        v_ref[0, h] = _dot(xb, w_ref[:, COL_V + 2 * DH_B * h:COL_V + 2 * DH_B * (h + 1)])


def _inproj_ab(x_all, row0, bsz, seq, tm, w, bg):
    nrt = seq // tm
    n = bsz * seq
    if row0 % tm:
        x_all, row0 = x_all[row0:row0 + n], 0
    off = row0 // tm
    return pl.pallas_call(
        _inproj_ab_kernel,
        grid=(bsz, nrt),
        in_specs=[
            pl.BlockSpec((tm, D_MODEL), lambda b, r: (off + b * nrt + r, 0)),
            pl.BlockSpec((D_MODEL, D_IN_AB_PAD), lambda b, r: (0, 0)),
            pl.BlockSpec((1, LANE), lambda b, r: (0, 0)),
        ],
        out_specs=[
            pl.BlockSpec((tm, COL_G), lambda b, r: (b * nrt + r, 0)),
            pl.BlockSpec((tm, LANE), lambda b, r: (b * nrt + r, 0)),
            pl.BlockSpec((tm, W_B), lambda b, r: (b * nrt + r, 0)),
            pl.BlockSpec((1, H_B, tm, 2 * DH_B), lambda b, r: (b, 0, r, 0)),
            pl.BlockSpec((1, H_B, tm, 2 * DH_B), lambda b, r: (b, 0, r, 0)),
        ],
        out_shape=[
            jax.ShapeDtypeStruct((n, COL_G), F32),
            jax.ShapeDtypeStruct((n, LANE), F32),
            jax.ShapeDtypeStruct((n, W_B), BF16),
            jax.ShapeDtypeStruct((bsz, H_B, seq, 2 * DH_B), F32),
            jax.ShapeDtypeStruct((bsz, H_B, seq, 2 * DH_B), F32),
        ],
        compiler_params=_params(("parallel", "parallel")),
        name="inproj_ab",
    )(x_all, w, bg)


def _mlstm_kernel(a_ref, g_ref, cw_ref, cb_ref, wq_ref, wk_ref, gain_ref, c0_ref, n0_ref, m0_ref, buf_ref,
                  out_ref, c1_ref, n1_ref, m1_ref, cs_ref, xp_ref, *, seq, lead):
    xp_ref[0:8, :] = jnp.zeros((8, W_A), F32)
    xp_ref[5:8, :] = buf_ref[0]
    xp_ref[8:8 + seq, :] = a_ref[:, 0:W_A]
    cs_ref[0] = xp_ref[5 + seq:8 + seq, :]
    c1_ref[...] = c0_ref[...]
    n1_ref[...] = n0_ref[...]
    m1_ref[...] = m0_ref[...]
    sel = (lax.broadcasted_iota(jnp.int32, (8, LANE), 0) == lax.broadcasted_iota(jnp.int32, (8, LANE), 1)).astype(F32)

    def chunk(r0, lc):
        win = xp_ref[pl.ds(r0, lc + 8), :]
        y = cb_ref[...]
        for j in range(CONV_W):
            y = y + win[5 + j:5 + j + lc, :] * cw_ref[j:j + 1, :]
        ca = y * _sigmoid(y)
        g = g_ref[pl.ds(r0, lc), :]
        g_rows = _dot_nt(sel, g, precision=lax.Precision.HIGHEST)
        ti = lax.broadcasted_iota(jnp.int32, (lc, lc), 0)
        si = lax.broadcasted_iota(jnp.int32, (lc, lc), 1)
        causal = si <= ti
        for h in range(H_A):
            hs = slice(DH_A * h, DH_A * (h + 1))
            ig_c = g[:, h:h + 1]
            lf_c = _log_sigmoid(g[:, H_A + h:H_A + h + 1])
            ig_r = g_rows[h:h + 1, :]
            lf_r = _log_sigmoid(g_rows[H_A + h:H_A + h + 1, :])
            b_c = jnp.sum(jnp.where(causal, lf_r, 0.0), axis=1, keepdims=True)
            b_r = jnp.sum(jnp.where(ti <= si, lf_c, 0.0), axis=0, keepdims=True)
            m_prev = m1_ref[0, h][:, 0:1]
            c_prev = c1_ref[0, h]
            n_prev = n1_ref[0, h]
            dmat = jnp.where(causal, b_c - b_r + ig_r, -jnp.inf)
            inter = b_c + m_prev
            m_t = jnp.maximum(inter, jnp.max(dmat, axis=1, keepdims=True))
            w = jnp.exp(dmat - m_t)
            gg = jnp.exp(inter - m_t)
            cab = ca[:, hs].astype(BF16)
            q = _dot(cab, wq_ref[h])
            k = _dot(cab, wk_ref[h]) * (DH_A ** -0.5)
            qb = q.astype(BF16)
            v = a_ref[pl.ds(r0, lc), W_A + DH_A * h:W_A + DH_A * (h + 1)]
            vb = v.astype(BF16)
            s = _dot_nt(qb, k.astype(BF16)) * w
            num = _dot(s.astype(BF16), vb) + gg * _dot(qb, c_prev.astype(BF16))
            den = jnp.sum(s, axis=1, keepdims=True) + gg * jnp.sum(q * n_prev, axis=1, keepdims=True)
            hh = num / jnp.maximum(jnp.abs(den), jnp.exp(-m_t))
            m_new = m_t[lc - 1:lc, :]
            b_last = b_c[lc - 1:lc, :]
            w_end = jnp.exp(b_last - b_c + ig_c - m_new)
            decay = jnp.exp(b_last + m_prev - m_new)
            kw = k * w_end
            c1_ref[0, h] = decay * c_prev + _dot_tn(kw.astype(BF16), vb)
            n1_ref[0, h] = decay * n_prev + jnp.sum(kw, axis=0, keepdims=True)
            m1_ref[0, h] = jnp.broadcast_to(m_new, (1, LANE))
            hn = hh * lax.rsqrt(jnp.mean(hh * hh, axis=1, keepdims=True) + LN_EPS) * gain_ref[:, hs]
            oa = a_ref[pl.ds(r0, lc), 2 * W_A + DH_A * h:2 * W_A + DH_A * (h + 1)]
            out_ref[pl.ds(r0, lc), hs] = (hn * _sigmoid(oa)).astype(BF16)

    if lead:
        chunk(0, lead)
    nch = (seq - lead) // CHUNK
    if nch == 1:
        chunk(lead, CHUNK)
    else:
        def body(c, carry):
            chunk(pl.multiple_of(lead + c * CHUNK, 16), CHUNK)
            return carry
        lax.fori_loop(0, nch, body, 0)


def _mlstm(a, g, bsz, seq, lead, cw, cb, wq, wk, gain, c0, n0, m0, buf):
    kern = functools.partial(_mlstm_kernel, seq=seq, lead=lead)
    full = lambda *shape: pl.BlockSpec(shape, lambda b: (0,) * len(shape))
    per_b = lambda *shape: pl.BlockSpec((1,) + shape, lambda b: (b,) + (0,) * len(shape))
    return pl.pallas_call(
        kern,
        grid=(bsz,),
        in_specs=[
            pl.BlockSpec((seq, COL_G), lambda b: (b, 0)),
            pl.BlockSpec((seq, LANE), lambda b: (b, 0)),
            full(CONV_W, W_A), full(1, W_A), full(H_A, DH_A, DH_A), full(H_A, DH_A, DH_A), full(1, W_A),
            per_b(H_A, DH_A, DH_A), per_b(H_A, 1, DH_A), per_b(H_A, 1, LANE), per_b(CONV_W - 1, W_A),
        ],
        out_specs=[
            pl.BlockSpec((seq, W_A), lambda b: (b, 0)),
            per_b(H_A, DH_A, DH_A), per_b(H_A, 1, DH_A), per_b(H_A, 1, LANE), per_b(CONV_W - 1, W_A),
        ],
        out_shape=[
            jax.ShapeDtypeStruct((bsz * seq, W_A), BF16),
            jax.ShapeDtypeStruct((bsz, H_A, DH_A, DH_A), F32),
            jax.ShapeDtypeStruct((bsz, H_A, 1, DH_A), F32),
            jax.ShapeDtypeStruct((bsz, H_A, 1, LANE), F32),
            jax.ShapeDtypeStruct((bsz, CONV_W - 1, W_A), F32),
        ],
        scratch_shapes=[pltpu.VMEM((seq + 8, W_A), F32)],
        compiler_params=_params(("parallel",)),
        name="mlstm",
    )(a, g, cw, cb, wq, wk, gain, c0, n0, m0, buf)


def _diff_kernel(lam_ref, q_ref, k_ref, v_ref, near_ref, gain_ref, o_ref, kb_ref, vb_ref, m_ref, l_ref, acc_ref,
                 *, tq, lq, lk, q_pos0, chunked, lam_init):
    qi = pl.program_id(2)
    nkt = kb_ref.shape[0] // KEY_TILE

    @pl.when(qi == 0)
    def _():
        kb_ref[0:lk, :] = k_ref[0, 0].astype(BF16)
        vb_ref[0:lk, :] = v_ref[0, 0].astype(BF16)
        if nkt * KEY_TILE > lk:
            kb_ref[lk:, :] = jnp.zeros((nkt * KEY_TILE - lk, 2 * DH_B), BF16)
            vb_ref[lk:, :] = jnp.zeros((nkt * KEY_TILE - lk, 2 * DH_B), BF16)

    q0 = q_pos0 + qi * tq
    qt = q0 >> LOG2_KEY_TILE
    if chunked:
        last = jnp.minimum(qi * tq + tq, lq) - 1 + q_pos0
        ke = jnp.minimum(lk, N_META + _chunk_id(last) * CHUNK)
        ntile = (ke + KEY_TILE - 1) >> LOG2_KEY_TILE
    else:
        ntile = nkt
    qq = q_ref[0] * (DH_B ** -0.5)
    q1 = qq[:, 0:DH_B]
    q2 = qq[:, DH_B:2 * DH_B]
    m_ref[...] = jnp.full((2 * tq, 1), NEG_BIG, F32)
    l_ref[...] = jnp.zeros((2 * tq, 1), F32)
    acc_ref[...] = jnp.zeros((2 * tq, 2 * DH_B), F32)
    qpos = q0 + lax.broadcasted_iota(jnp.int32, (tq, KEY_TILE), 0)
    kcol = lax.broadcasted_iota(jnp.int32, (tq, KEY_TILE), 1)

    def body(j, carry):
        ks = pl.multiple_of(j * KEY_TILE, KEY_TILE)
        kt = kb_ref[pl.ds(ks, KEY_TILE), :]
        vt = vb_ref[pl.ds(ks, KEY_TILE), :]
        bias = near_ref[0, jnp.maximum(j - qt + 2, 0)]
        kpos = ks + kcol
        if chunked:
            mask = _chunk_id(kpos) <= _chunk_id(qpos)
        else:
            mask = kpos < lk
        s1 = jnp.where(mask, _dot_nt(q1, kt[:, 0:DH_B]) + bias, NEG_BIG)
        s2 = jnp.where(mask, _dot_nt(q2, kt[:, DH_B:2 * DH_B]) + bias, NEG_BIG)
        s = jnp.concatenate([s1, s2], axis=0)
        m_old = m_ref[...]
        m_new = jnp.maximum(m_old, jnp.max(s, axis=1, keepdims=True))
        alpha = jnp.exp(m_old - m_new)
        p = jnp.exp(s - m_new)
        l_ref[...] = alpha * l_ref[...] + jnp.sum(p, axis=1, keepdims=True)
        acc_ref[...] = alpha * acc_ref[...] + _dot(p.astype(BF16), vt)
        m_ref[...] = m_new
        return carry

    lax.fori_loop(0, ntile, body, 0)
    o = acc_ref[...] / l_ref[...]
    lam = lam_ref[:, 0:1]
    o = o[0:tq] - lam * o[tq:2 * tq]
    o = o * lax.rsqrt(jnp.mean(o * o, axis=1, keepdims=True) + LN_EPS) * gain_ref[...] * (1.0 - lam_init)
    o_ref[0] = o.astype(BF16)


def _diff_attention(q, k, v, near, lam, gain, *, tq, q_pos0, chunked, lam_init):
    bsz, lq, _ = q.shape
    lk = k.shape[2]
    nq = -(-lq // tq)
    nkt = -(-lk // KEY_TILE)
    kern = functools.partial(_diff_kernel, tq=tq, lq=lq, lk=lk, q_pos0=q_pos0, chunked=chunked, lam_init=lam_init)
    return pl.pallas_call(
        kern,
        grid=(bsz, H_B, nq),
        in_specs=[
            pl.BlockSpec((1, LANE), lambda b, h, i: (0, 0)),
            pl.BlockSpec((1, tq, 2 * DH_B), lambda b, h, i: (b, i, h)),
            pl.BlockSpec((1, 1, lk, 2 * DH_B), lambda b, h, i: (b, h, 0, 0)),
            pl.BlockSpec((1, 1, lk, 2 * DH_B), lambda b, h, i: (b, h, 0, 0)),
            pl.BlockSpec((1, 4, tq, KEY_TILE), lambda b, h, i: (h, 0, 0, 0)),
            pl.BlockSpec((1, 2 * DH_B), lambda b, h, i: (0, 0)),
        ],
        out_specs=pl.BlockSpec((1, tq, 2 * DH_B), lambda b, h, i: (b, i, h)),
        out_shape=jax.ShapeDtypeStruct((bsz, lq, W_B), BF16),
        scratch_shapes=[
            pltpu.VMEM((nkt * KEY_TILE, 2 * DH_B), BF16),
            pltpu.VMEM((nkt * KEY_TILE, 2 * DH_B), BF16),
            pltpu.VMEM((2 * tq, 1), F32),
            pltpu.VMEM((2 * tq, 1), F32),
            pltpu.VMEM((2 * tq, 2 * DH_B), F32),
        ],
        compiler_params=_params(("parallel", "parallel", "arbitrary")),
        name="diff_attention",
    )(lam, q, k, v, near, gain)


def _inproj_c_kernel(x_ref, w_ref, q_ref, k_ref, v_ref):
    xb = x_ref[...].astype(BF16)
    q_ref[...] = _dot(xb, w_ref[:, 0:D_MODEL]).astype(BF16)
    yk = _dot(xb, w_ref[:, D_MODEL:2 * D_MODEL])
    for h in range(H_C):
        k_ref[0, h] = yk[:, DH_C * h:DH_C * (h + 1)]
    yv = _dot(xb, w_ref[:, 2 * D_MODEL:3 * D_MODEL])
    for h in range(H_C):
        v_ref[0, h] = yv[:, DH_C * h:DH_C * (h + 1)]


def _inproj_c(x_all, row0, bsz, seq, tm, w):
    nrt = seq // tm
    if row0 % tm:
        x_all, row0 = x_all[row0:row0 + bsz * seq], 0
    off = row0 // tm
    return pl.pallas_call(
        _inproj_c_kernel,
        grid=(bsz, nrt),
        in_specs=[
            pl.BlockSpec((tm, D_MODEL), lambda b, r: (off + b * nrt + r, 0)),
            pl.BlockSpec((D_MODEL, 3 * D_MODEL), lambda b, r: (0, 0)),
        ],
        out_specs=[
            pl.BlockSpec((tm, D_MODEL), lambda b, r: (b * nrt + r, 0)),
            pl.BlockSpec((1, H_C, tm, DH_C), lambda b, r: (b, 0, r, 0)),
            pl.BlockSpec((1, H_C, tm, DH_C), lambda b, r: (b, 0, r, 0)),
        ],
        out_shape=[
            jax.ShapeDtypeStruct((bsz * seq, D_MODEL), BF16),
            jax.ShapeDtypeStruct((bsz, H_C, seq, DH_C), F32),
            jax.ShapeDtypeStruct((bsz, H_C, seq, DH_C), F32),
        ],
        compiler_params=_params(("parallel", "parallel")),
        name="inproj_c",
    )(x_all, w)


def _sb_kernel(q_ref, k_ref, v_ref, o_ref, kb_ref, vb_ref, *, tq, lk, q_pos0):
    qi = pl.program_id(2)
    nkt = kb_ref.shape[1] // KEY_TILE

    @pl.when(qi == 0)
    def _():
        for hh in range(2):
            kb_ref[hh, 0:lk, :] = k_ref[0, hh].astype(BF16)
            vb_ref[hh, 0:lk, :] = v_ref[0, hh].astype(BF16)
            if nkt * KEY_TILE > lk:
                kb_ref[hh, lk:, :] = jnp.zeros((nkt * KEY_TILE - lk, DH_C), BF16)
                vb_ref[hh, lk:, :] = jnp.zeros((nkt * KEY_TILE - lk, DH_C), BF16)

    q0 = q_pos0 + qi * tq
    jd = q0 >> LOG2_KEY_TILE
    qpos = q0 + lax.broadcasted_iota(jnp.int32, (tq, KEY_TILE), 0)
    kcol = lax.broadcasted_iota(jnp.int32, (tq, KEY_TILE), 1)
    later = (lax.broadcasted_iota(jnp.int32, (KEY_TILE, KEY_TILE), 0)
             > lax.broadcasted_iota(jnp.int32, (KEY_TILE, KEY_TILE), 1)).astype(BF16)
    qq = q_ref[0] * (DH_C ** -0.5)
    outs = []
    for hh in range(2):
        q = qq[:, DH_C * hh:DH_C * (hh + 1)]

        def tile(j, acc, run, masked):
            ks = pl.multiple_of(j * KEY_TILE, KEY_TILE)
            kt = kb_ref[hh, pl.ds(ks, KEY_TILE), :]
            vt = vb_ref[hh, pl.ds(ks, KEY_TILE), :]
            z = _dot_nt(q, kt)
            log_stay = -(jnp.maximum(z, 0.0) + jnp.log(1.0 + jnp.exp(-jnp.abs(z))))
            if masked:
                mask = (ks + kcol) < qpos
                log_stay = jnp.where(mask, log_stay, 0.0)
            hi = log_stay.astype(BF16)
            lo = (log_stay - hi.astype(F32)).astype(BF16)
            log_after = _dot(hi, later) + _dot(lo, later) + run
            a = jnp.exp(z + log_stay + log_after)
            if masked:
                a = jnp.where(mask, a, 0.0)
            acc = acc + _dot(a.astype(BF16), vt)
            run = run + jnp.sum(log_stay, axis=1, keepdims=True)
            return acc, run

        acc, run = tile(jd, jnp.zeros((tq, DH_C), F32), jnp.zeros((tq, 1), F32), True)

        def body(jj, carry):
            return tile(jd - 1 - jj, carry[0], carry[1], False)

        acc, run = lax.fori_loop(0, jd, body, (acc, run))
        outs.append(acc)
    o_ref[0] = jnp.concatenate(outs, axis=1).astype(BF16)


def _stick_breaking(q, k, v, *, tq, q_pos0):
    bsz, lq, _ = q.shape
    lk = k.shape[2]
    nq = -(-lq // tq)
    nkt = -(-lk // KEY_TILE)
    kern = functools.partial(_sb_kernel, tq=tq, lk=lk, q_pos0=q_pos0)
    return pl.pallas_call(
        kern,
        grid=(bsz, H_C // 2, nq),
        in_specs=[
            pl.BlockSpec((1, tq, 2 * DH_C), lambda b, h, i: (b, i, h)),
            pl.BlockSpec((1, 2, lk, DH_C), lambda b, h, i: (b, h, 0, 0)),
            pl.BlockSpec((1, 2, lk, DH_C), lambda b, h, i: (b, h, 0, 0)),
        ],
        out_specs=pl.BlockSpec((1, tq, 2 * DH_C), lambda b, h, i: (b, i, h)),
        out_shape=jax.ShapeDtypeStruct((bsz, lq, D_MODEL), BF16),
        scratch_shapes=[
            pltpu.VMEM((2, nkt * KEY_TILE, DH_C), BF16),
            pltpu.VMEM((2, nkt * KEY_TILE, DH_C), BF16),
        ],
        compiler_params=_params(("parallel", "parallel", "arbitrary")),
        name="stick_breaking",
    )(q, k, v)


def _layer_norm(z, g, b):
    mu = jnp.mean(z, axis=1, keepdims=True)
    zc = z - mu
    var = jnp.mean(zc * zc, axis=1, keepdims=True)
    return zc * lax.rsqrt(var + LN_EPS) * g + b


def _outproj_kernel(*refs, n_in):
    mix_refs = refs[:n_in]
    (w_ref, x_ref, g_ref, b_ref, wr_ref, br_ref,
     x1_ref, x1b_ref, ti_ref, tg_ref, rk_ref, cnt_ref, tri_ref) = refs[n_in:]
    i = pl.program_id(0)
    tm = x_ref.shape[0]

    @pl.when(i == 0)
    def _():
        cnt_ref[...] = jnp.zeros_like(cnt_ref)
        tri_ref[...] = (lax.broadcasted_iota(jnp.int32, (tm, tm), 0)
                        < lax.broadcasted_iota(jnp.int32, (tm, tm), 1)).astype(BF16)

    y = None
    c0 = 0
    for r in mix_refs:
        wd = r.shape[1]
        part = _dot(r[...], w_ref[c0:c0 + wd, :])
        y = part if y is None else y + part
        c0 += wd
    x1 = _layer_norm(DN_ALPHA * x_ref[...] + y, g_ref[...], b_ref[...])
    x1_ref[...] = x1
    x1b_ref[...] = x1.astype(BF16)
    logits = _dot_nt(wr_ref[...], x1, precision=lax.Precision.HIGHEST) + br_ref[...]
    ei = lax.broadcasted_iota(jnp.int32, (N_EXPERTS, tm), 0)
    cur = logits
    vals, idxs = [], []
    for _k in range(TOP_K):
        mx = jnp.max(cur, axis=0, keepdims=True)
        ix = jnp.min(jnp.where(cur == mx, ei, N_EXPERTS), axis=0, keepdims=True)
        vals.append(mx)
        idxs.append(ix)
        cur = jnp.where(ei == ix, -jnp.inf, cur)
    es = [jnp.exp(vv - vals[0]) for vv in vals]
    tot = es[0] + es[1] + es[2] + es[3]
    ti_ref[...] = jnp.concatenate(idxs, axis=0)
    tg_ref[...] = jnp.concatenate([e / tot for e in es], axis=0)
    run = cnt_ref[...]
    ranks = []
    for kk in range(TOP_K):
        oh = (ei == idxs[kk])
        before = _dot(oh.astype(BF16), tri_ref[...])
        ranks.append(jnp.sum(jnp.where(oh, before + run, 0.0), axis=0, keepdims=True))
        run = run + jnp.sum(oh.astype(F32), axis=1, keepdims=True)
    rk_ref[...] = jnp.concatenate(ranks, axis=0).astype(jnp.int32)
    cnt_ref[...] = run


def _outproj_ln_router(mixes, w, x_all, g, b, wr_t, br):
    n = x_all.shape[0]
    tm = TOKEN_TILE
    kern = functools.partial(_outproj_kernel, n_in=len(mixes))
    row = lambda wd: pl.BlockSpec((tm, wd), lambda i: (i, 0))
    full = lambda *shape: pl.BlockSpec(shape, lambda i: (0,) * len(shape))
    lanes = lambda rows: pl.BlockSpec((rows, tm), lambda i: (0, i))
    return pl.pallas_call(
        kern,
        grid=(n // tm,),
        in_specs=[row(m.shape[1]) for m in mixes] + [
            full(D_MODEL, D_MODEL), row(D_MODEL), full(1, D_MODEL), full(1, D_MODEL),
            full(N_EXPERTS, D_MODEL), full(N_EXPERTS, 1)],
        out_specs=[row(D_MODEL), row(D_MODEL), lanes(TOP_K), lanes(TOP_K), lanes(TOP_K), full(N_EXPERTS, 1)],
        out_shape=[
            jax.ShapeDtypeStruct((n, D_MODEL), F32),
            jax.ShapeDtypeStruct((n, D_MODEL), BF16),
            jax.ShapeDtypeStruct((TOP_K, n), jnp.int32),
            jax.ShapeDtypeStruct((TOP_K, n), F32),
            jax.ShapeDtypeStruct((TOP_K, n), jnp.int32),
            jax.ShapeDtypeStruct((N_EXPERTS, 1), F32),
        ],
        scratch_shapes=[pltpu.VMEM((tm, tm), BF16)],
        compiler_params=_params(("arbitrary",)),
        name="outproj_ln_router",
    )(*mixes, w, x_all, g, b, wr_t, br)


def _moe_kernel(be_ref, nu_ref, x_ref, wgu_ref, bgu_ref, wdn_ref, bdn_ref, o_ref):
    @pl.when(pl.program_id(0) < nu_ref[0])
    def _():
        h = _dot(x_ref[...], wgu_ref[0]) + bgu_ref[0]
        glu = jnp.minimum(h[:, 0:D_EXPERT], SWIGLU_LIMIT)
        lin = jnp.clip(h[:, D_EXPERT:2 * D_EXPERT], -SWIGLU_LIMIT, SWIGLU_LIMIT)
        act = glu * _sigmoid(SWIGLU_ALPHA * glu) * (lin + 1.0)
        o_ref[...] = (_dot(act.astype(BF16), wdn_ref[0]) + bdn_ref[0]).astype(BF16)


def _moe_experts(blk_e, n_used, xg, wgu, bgu, wdn, bdn):
    nb = blk_e.shape[0]
    grid_spec = pltpu.PrefetchScalarGridSpec(
        num_scalar_prefetch=2,
        grid=(nb,),
        in_specs=[
            pl.BlockSpec((MOE_BLOCK, D_MODEL), lambda i, be, nu: (i, 0)),
            pl.BlockSpec((1, D_MODEL, 2 * D_EXPERT), lambda i, be, nu: (be[i], 0, 0)),
            pl.BlockSpec((1, 1, 2 * D_EXPERT), lambda i, be, nu: (be[i], 0, 0)),
            pl.BlockSpec((1, D_EXPERT, D_MODEL), lambda i, be, nu: (be[i], 0, 0)),
            pl.BlockSpec((1, 1, D_MODEL), lambda i, be, nu: (be[i], 0, 0)),
        ],
        out_specs=pl.BlockSpec((MOE_BLOCK, D_MODEL), lambda i, be, nu: (i, 0)),
    )
    return pl.pallas_call(
        _moe_kernel,
        grid_spec=grid_spec,
        out_shape=jax.ShapeDtypeStruct((nb * MOE_BLOCK, D_MODEL), BF16),
        compiler_params=_params(("arbitrary",)),
        name="moe_experts",
    )(blk_e, n_used, xg, wgu, bgu, wdn, bdn)


def _combine_ln_kernel(x_ref, f_ref, g_ref, b_ref, o_ref):
    o_ref[...] = _layer_norm(DN_ALPHA * x_ref[...] + f_ref[...], g_ref[...], b_ref[...])


def _combine_ln(x1, f, g, b):
    n = x1.shape[0]
    tm = TOKEN_TILE
    row = pl.BlockSpec((tm, D_MODEL), lambda i: (i, 0))
    vec = pl.BlockSpec((1, D_MODEL), lambda i: (0, 0))
    return pl.pallas_call(
        _combine_ln_kernel,
        grid=(n // tm,),
        in_specs=[row, row, vec, vec],
        out_specs=row,
        out_shape=jax.ShapeDtypeStruct((n, D_MODEL), F32),
        compiler_params=_params(("parallel",)),
        name="combine_ln",
    )(x1, f, g, b)


def _moe_layer(x1, x1b, top_i, top_g, rank, sizes, wgu, bgu, wdn, bdn, g, b):
    n = x1.shape[0]
    nb = -(-(n * TOP_K) // MOE_BLOCK) + N_EXPERTS
    sizes = sizes[:, 0].astype(jnp.int32)
    nblk = (sizes + MOE_BLOCK - 1) // MOE_BLOCK
    blk_end = jnp.cumsum(nblk)
    pad_starts = (blk_end - nblk) * MOE_BLOCK
    n_used = blk_end[-1]
    blk_ids = jnp.arange(nb, dtype=jnp.int32)
    blk_e = jnp.searchsorted(blk_end, jnp.minimum(blk_ids, n_used - 1), side='right').astype(jnp.int32)
    blk_e = jnp.clip(blk_e, 0, N_EXPERTS - 1)
    dest = pad_starts[top_i] + rank
    tok = jnp.broadcast_to(jnp.arange(n, dtype=jnp.int32)[None], (TOP_K, n))
    tok_of_row = jnp.zeros((nb * MOE_BLOCK,), jnp.int32).at[dest.reshape(-1)].set(
        tok.reshape(-1), unique_indices=True, indices_are_sorted=False)
    xg = jnp.take(x1b, tok_of_row, axis=0)
    out = _moe_experts(blk_e, n_used.reshape(1), xg, wgu, bgu, wdn, bdn)
    f = jnp.zeros((n, D_MODEL), F32)
    for kk in range(TOP_K):
        f = f + jnp.take(out, dest[kk], axis=0).astype(F32) * top_g[kk][:, None]
    return _combine_ln(x1, f, g, b)


def _rel_bucket(rel):
    half = N_BUCKETS // 2
    exact = half // 2
    ret = jnp.where(rel > 0, half, 0)
    n = jnp.abs(rel)
    large = exact + (jnp.log(jnp.maximum(n, 1).astype(F32) / exact)
                     / math.log(MAX_DISTANCE / exact) * (half - exact)).astype(jnp.int32)
    large = jnp.minimum(large, half - 1)
    return ret + jnp.where(n < exact, n, large)


def _near_bias_tiles(rel_table, tq, q_pos0):
    base = q_pos0 % KEY_TILE
    i = jnp.arange(tq, dtype=jnp.int32)[:, None] + base
    j = jnp.arange(KEY_TILE, dtype=jnp.int32)[None, :]
    tiles = [jnp.full((tq, KEY_TILE), -2 * MAX_DISTANCE, jnp.int32)]
    for d in (-KEY_TILE, 0, KEY_TILE):
        tiles.append(d + j - i)
    rel = jnp.stack(tiles)
    return jnp.transpose(rel_table[_rel_bucket(rel)], (3, 0, 1, 2)).astype(F32)


def kernel(x_prompt, x_sample, cache_diff_k, cache_diff_v, state_mlstm_C, state_mlstm_n, state_mlstm_m, state_mlstm_conv, cache_sb_k, cache_sb_v, meta_tokens, rel_bias, w_in_ab, w_out_ab, conv_w_a, conv_b_a, w_aq_a, w_ak_a, b_if_a, mh_gain_a, lam_q1, lam_k1, lam_q2, lam_k2, subln_gain_b, w_in_c, w_out_c, ln_g, ln_b, w_router, b_router, w_gu, b_gu, w_down, b_down):
    bp, sp, _ = x_prompt.shape
    bs, ss, _ = x_sample.shape
    lp = N_META + sp
    past = cache_diff_k.shape[3]
    n_p = bp * lp
    n_s = bs * ss
    tm_p = lp // 3 if (lp % 3 == 0 and (lp // 3) % 16 == 0) else lp
    tq_p = KEY_TILE

    x0 = jnp.concatenate([jnp.broadcast_to(meta_tokens[None], (bp, N_META, D_MODEL)).astype(x_prompt.dtype),
                          x_prompt], axis=1)
    x_all = jnp.concatenate([x0.reshape(n_p, D_MODEL), x_sample.reshape(n_s, D_MODEL)], axis=0)

    j = 0
    lam_init = 0.8 - 0.6 * math.exp(-0.3 * 0)
    w_ab = w_in_ab[j]
    w_perm = jnp.concatenate([
        w_ab[:, 0:OFF_AIF],
        jnp.pad(w_ab[:, OFF_AIF:OFF_BQ], ((0, 0), (0, LANE - 2 * H_A))),
        w_ab[:, OFF_BQ:]], axis=1).astype(BF16)
    bg = jnp.pad(b_if_a[j], (0, LANE - 2 * H_A)).reshape(1, LANE)
    lam = (jnp.exp(jnp.sum(lam_q1[j] * lam_k1[j])) - jnp.exp(jnp.sum(lam_q2[j] * lam_k2[j]))).astype(F32) + lam_init
    lam_v = jnp.broadcast_to(lam.reshape(1, 1), (1, LANE))
    cw = conv_w_a[j]
    cb = conv_b_a[j].reshape(1, W_A)
    wq = w_aq_a[j].astype(BF16)
    wk = w_ak_a[j].astype(BF16)
    gain_a = mh_gain_a[j].reshape(1, W_A)
    gain_b = subln_gain_b[j].reshape(1, 2 * DH_B)

    def ab_group(row0, bsz, seq, tm, lead, state, past_kv, tq, q_pos0, chunked):
        a, g, q, k_new, v_new = _inproj_ab(x_all, row0, bsz, seq, tm, w_perm, bg)
        c0, n0, m0, buf = state
        mix_a, c1, n1, m1, cs = _mlstm(
            a, g, bsz, seq, lead, cw, cb, wq, wk, gain_a,
            c0, n0.reshape(bsz, H_A, 1, DH_A),
            jnp.broadcast_to(m0[:, :, None, None], (bsz, H_A, 1, LANE)), buf)
        if past_kv is None:
            k_all, v_all = k_new, v_new
        else:
            k_all = jnp.concatenate([past_kv[0], k_new], axis=2)
            v_all = jnp.concatenate([past_kv[1], v_new], axis=2)
        near = _near_bias_tiles(rel_bias, tq, q_pos0)
        mix_b = _diff_attention(q.reshape(bsz, seq, W_B), k_all, v_all, near, lam_v, gain_b,
                                tq=tq, q_pos0=q_pos0, chunked=chunked, lam_init=lam_init)
        outs = (k_new, v_new, c1, n1.reshape(bsz, H_A, DH_A), m1[:, :, 0, 0], cs)
        return mix_a, mix_b.reshape(bsz * seq, W_B), outs

    zero_state = (jnp.zeros((bp, H_A, DH_A, DH_A), F32), jnp.zeros((bp, H_A, DH_A), F32),
                  jnp.zeros((bp, H_A), F32), jnp.zeros((bp, CONV_W - 1, W_A), F32))
    mix_a_p, mix_b_p, ab_p = ab_group(0, bp, lp, tm_p, N_META, zero_state, None, tq_p, 0, True)
    s_state = (state_mlstm_C[j], state_mlstm_n[j], state_mlstm_m[j], state_mlstm_conv[j])
    mix_a_s, mix_b_s, ab_s = ab_group(n_p, bs, ss, ss, 0, s_state, (cache_diff_k[j], cache_diff_v[j]),
                                      ss, past, False)
    mix_a = jnp.concatenate([mix_a_p, mix_a_s], axis=0)
    mix_b = jnp.concatenate([mix_b_p, mix_b_s], axis=0)

    def token_stage(layer, mixes, w_out, x_in):
        wr_t = jnp.transpose(w_router[layer])
        br = b_router[layer].reshape(N_EXPERTS, 1)
        x1, x1b, top_i, top_g, rank, sizes = _outproj_ln_router(
            mixes, w_out.astype(BF16), x_in, ln_g[layer, 0].reshape(1, D_MODEL), ln_b[layer, 0].reshape(1, D_MODEL),
            wr_t, br)
        wgu = jnp.concatenate([w_gu[layer][:, :, 0::2], w_gu[layer][:, :, 1::2]], axis=2).astype(BF16)
        bgu = jnp.concatenate([b_gu[layer][:, 0::2], b_gu[layer][:, 1::2]], axis=1).reshape(N_EXPERTS, 1, 2 * D_EXPERT)
        wdn = w_down[layer].astype(BF16)
        bdn = b_down[layer].reshape(N_EXPERTS, 1, D_MODEL)
        return _moe_layer(x1, x1b, top_i, top_g, rank, sizes, wgu, bgu, wdn, bdn,
                          ln_g[layer, 1].reshape(1, D_MODEL), ln_b[layer, 1].reshape(1, D_MODEL))

    x_all = token_stage(0, [mix_a, mix_b], w_out_ab[j], x_all)

    w_c = w_in_c[j].astype(BF16)

    def c_group(row0, bsz, seq, tm, past_kv, tq, q_pos0):
        q, k_new, v_new = _inproj_c(x_all, row0, bsz, seq, tm, w_c)
        if past_kv is None:
            k_all, v_all = k_new, v_new
        else:
            k_all = jnp.concatenate([past_kv[0], k_new], axis=2)
            v_all = jnp.concatenate([past_kv[1], v_new], axis=2)
        o = _stick_breaking(q.reshape(bsz, seq, D_MODEL), k_all, v_all, tq=tq, q_pos0=q_pos0)
        return o.reshape(bsz * seq, D_MODEL), (k_new, v_new)

    mix_p, c_p = c_group(0, bp, lp, tm_p, None, tq_p, 0)
    mix_s, c_s = c_group(n_p, bs, ss, ss, (cache_sb_k[j], cache_sb_v[j]), ss, past)
    x_all = token_stage(1, [jnp.concatenate([mix_p, mix_s], axis=0)], w_out_c[j], x_all)

    y_prompt = x_all[:n_p].reshape(bp, lp, D_MODEL)[:, N_META:]
    y_sample = x_all[n_p:].reshape(bs, ss, D_MODEL)
    stack = lambda t: t[None]
    return (y_prompt, y_sample,
            stack(ab_p[0]), stack(ab_p[1]), stack(ab_p[2]), stack(ab_p[3]), stack(ab_p[4]), stack(ab_p[5]),
            stack(c_p[0]), stack(c_p[1]),
            stack(ab_s[0]), stack(ab_s[1]), stack(ab_s[2]), stack(ab_s[3]), stack(ab_s[4]), stack(ab_s[5]),
            stack(c_s[0]), stack(c_s[1]))
```

```python
import functools
import math

import jax
import jax.numpy as jnp
from jax import lax
from jax.experimental import pallas as pl
from jax.experimental.pallas import tpu as pltpu

F32 = jnp.float32
BF16 = jnp.bfloat16

D_MODEL = 1024
DEPTH = 2
CHUNK = 64
N_META = 16
H_A = 4
DH_A = 128
W_A = H_A * DH_A
CONV_W = 4
H_B = 4
DH_B = 64
W_B = H_B * 2 * DH_B
H_C = 16
DH_C = 64
N_BUCKETS = 32
MAX_DISTANCE = 128
N_EXPERTS = 32
TOP_K = 4
D_EXPERT = D_MODEL // 2
SWIGLU_LIMIT = 7.0
SWIGLU_ALPHA = 1.702
DN_ALPHA = (2 * DEPTH) ** 0.25
LN_EPS = 1e-5
OFF_AIF = 3 * W_A
OFF_BQ = OFF_AIF + 2 * H_A

LANE = 128
KEY_TILE = 128
LOG2_KEY_TILE = 7
LOG2_CHUNK = 6
VMEM_LIMIT = 56 * 1024 * 1024
MOE_BLOCK = 512
TOKEN_TILE = 512
NEG_BIG = -1e30

COL_G = 3 * W_A
COL_Q = COL_G + LANE
COL_K = COL_Q + W_B
COL_V = COL_K + W_B
D_IN_AB_PAD = COL_V + W_B


def _dot(a, b):
    return jnp.dot(a, b, preferred_element_type=F32)


def _dot_nt(a, b, precision=None):
    return lax.dot_general(a, b, (((1,), (1,)), ((), ())), preferred_element_type=F32, precision=precision)


def _dot_tn(a, b):
    return lax.dot_general(a, b, (((0,), (0,)), ((), ())), preferred_element_type=F32)


def _log_sigmoid(x):
    return jnp.minimum(x, 0.0) - jnp.log(1.0 + jnp.exp(-jnp.abs(x)))


def _sigmoid(x):
    return 1.0 / (1.0 + jnp.exp(-x))


def _chunk_id(pos):
    return (pos + (CHUNK - N_META)) >> LOG2_CHUNK


def _params(sem):
    return pltpu.CompilerParams(dimension_semantics=sem, vmem_limit_bytes=VMEM_LIMIT)


def _inproj_ab_kernel(x_ref, w_ref, bg_ref, a_ref, g_ref, q_ref, k_ref, v_ref):
    xb = x_ref[...].astype(BF16)
    a_ref[...] = _dot(xb, w_ref[:, 0:COL_G])
    g_ref[...] = _dot(xb, w_ref[:, COL_G:COL_Q]) + bg_ref[...]
    q_ref[...] = _dot(xb, w_ref[:, COL_Q:COL_K]).astype(BF16)
    for h in range(H_B):
        k_ref[0, h] = _dot(xb, w_ref[:, COL_K + 2 * DH_B * h:COL_K + 2 * DH_B * (h + 1)])
        v_ref[0, h] = _dot(xb, w_ref[:, COL_V + 2 * DH_B * h:COL_V + 2 * DH_B * (h + 1)])


def _inproj_ab(x_all, row0, bsz, seq, tm, w, bg):
    nrt = seq // tm
    n = bsz * seq
    if row0 % tm:
        x_all, row0 = x_all[row0:row0 + n], 0
    off = row0 // tm
    return pl.pallas_call(
        _inproj_ab_kernel,
        grid=(bsz, nrt),
        in_specs=[
            pl.BlockSpec((tm, D_MODEL), lambda b, r: (off + b * nrt + r, 0)),
            pl.BlockSpec((D_MODEL, D_IN_AB_PAD), lambda b, r: (0, 0)),
            pl.BlockSpec((1, LANE), lambda b, r: (0, 0)),
        ],
        out_specs=[
            pl.BlockSpec((tm, COL_G), lambda b, r: (b * nrt + r, 0)),
            pl.BlockSpec((tm, LANE), lambda b, r: (b * nrt + r, 0)),
            pl.BlockSpec((tm, W_B), lambda b, r: (b * nrt + r, 0)),
            pl.BlockSpec((1, H_B, tm, 2 * DH_B), lambda b, r: (b, 0, r, 0)),
            pl.BlockSpec((1, H_B, tm, 2 * DH_B), lambda b, r: (b, 0, r, 0)),
        ],
        out_shape=[
            jax.ShapeDtypeStruct((n, COL_G), F32),
            jax.ShapeDtypeStruct((n, LANE), F32),
            jax.ShapeDtypeStruct((n, W_B), BF16),
            jax.ShapeDtypeStruct((bsz, H_B, seq, 2 * DH_B), F32),
            jax.ShapeDtypeStruct((bsz, H_B, seq, 2 * DH_B), F32),
        ],
        compiler_params=_params(("parallel", "parallel")),
        name="inproj_ab",
    )(x_all, w, bg)


def _mlstm_kernel(a_ref, g_ref, cw_ref, cb_ref, wq_ref, wk_ref, gain_ref, c0_ref, n0_ref, m0_ref, buf_ref,
                  out_ref, c1_ref, n1_ref, m1_ref, cs_ref, xp_ref, *, seq, lead):
    xp_ref[0:8, :] = jnp.zeros((8, W_A), F32)
    xp_ref[5:8, :] = buf_ref[0]
    xp_ref[8:8 + seq, :] = a_ref[:, 0:W_A]
    cs_ref[0] = xp_ref[5 + seq:8 + seq, :]
    c1_ref[...] = c0_ref[...]
    n1_ref[...] = n0_ref[...]
    m1_ref[...] = m0_ref[...]
    sel = (lax.broadcasted_iota(jnp.int32, (8, LANE), 0) == lax.broadcasted_iota(jnp.int32, (8, LANE), 1)).astype(F32)

    def chunk(r0, lc):
        win = xp_ref[pl.ds(r0, lc + 8), :]
        y = cb_ref[...]
        for j in range(CONV_W):
            y = y + win[5 + j:5 + j + lc, :] * cw_ref[j:j + 1, :]
        ca = y * _sigmoid(y)
        g = g_ref[pl.ds(r0, lc), :]
        g_rows = _dot_nt(sel, g, precision=lax.Precision.HIGHEST)
        ti = lax.broadcasted_iota(jnp.int32, (lc, lc), 0)
        si = lax.broadcasted_iota(jnp.int32, (lc, lc), 1)
        causal = si <= ti
        for h in range(H_A):
            hs = slice(DH_A * h, DH_A * (h + 1))
            ig_c = g[:, h:h + 1]
            lf_c = _log_sigmoid(g[:, H_A + h:H_A + h + 1])
            ig_r = g_rows[h:h + 1, :]
            lf_r = _log_sigmoid(g_rows[H_A + h:H_A + h + 1, :])
            b_c = jnp.sum(jnp.where(causal, lf_r, 0.0), axis=1, keepdims=True)
            b_r = jnp.sum(jnp.where(ti <= si, lf_c, 0.0), axis=0, keepdims=True)
            m_prev = m1_ref[0, h][:, 0:1]
            c_prev = c1_ref[0, h]
            n_prev = n1_ref[0, h]
            dmat = jnp.where(causal, b_c - b_r + ig_r, -jnp.inf)
            inter = b_c + m_prev
            m_t = jnp.maximum(inter, jnp.max(dmat, axis=1, keepdims=True))
            w = jnp.exp(dmat - m_t)
            gg = jnp.exp(inter - m_t)
            cab = ca[:, hs].astype(BF16)
            q = _dot(cab, wq_ref[h])
            k = _dot(cab, wk_ref[h]) * (DH_A ** -0.5)
            qb = q.astype(BF16)
            v = a_ref[pl.ds(r0, lc), W_A + DH_A * h:W_A + DH_A * (h + 1)]
            vb = v.astype(BF16)
            s = _dot_nt(qb, k.astype(BF16)) * w
            num = _dot(s.astype(BF16), vb) + gg * _dot(qb, c_prev.astype(BF16))
            den = jnp.sum(s, axis=1, keepdims=True) + gg * jnp.sum(q * n_prev, axis=1, keepdims=True)
            hh = num / jnp.maximum(jnp.abs(den), jnp.exp(-m_t))
            m_new = m_t[lc - 1:lc, :]
            b_last = b_c[lc - 1:lc, :]
            w_end = jnp.exp(b_last - b_c + ig_c - m_new)
            decay = jnp.exp(b_last + m_prev - m_new)
            kw = k * w_end
            c1_ref[0, h] = decay * c_prev + _dot_tn(kw.astype(BF16), vb)
            n1_ref[0, h] = decay * n_prev + jnp.sum(kw, axis=0, keepdims=True)
            m1_ref[0, h] = jnp.broadcast_to(m_new, (1, LANE))
            hn = hh * lax.rsqrt(jnp.mean(hh * hh, axis=1, keepdims=True) + LN_EPS) * gain_ref[:, hs]
            oa = a_ref[pl.ds(r0, lc), 2 * W_A + DH_A * h:2 * W_A + DH_A * (h + 1)]
            out_ref[pl.ds(r0, lc), hs] = (hn * _sigmoid(oa)).astype(BF16)

    if lead:
        chunk(0, lead)
    nch = (seq - lead) // CHUNK
    if nch == 1:
        chunk(lead, CHUNK)
    else:
        def body(c, carry):
            chunk(pl.multiple_of(lead + c * CHUNK, 16), CHUNK)
            return carry
        lax.fori_loop(0, nch, body, 0)


def _mlstm(a, g, bsz, seq, lead, cw, cb, wq, wk, gain, c0, n0, m0, buf):
    kern = functools.partial(_mlstm_kernel, seq=seq, lead=lead)
    full = lambda *shape: pl.BlockSpec(shape, lambda b: (0,) * len(shape))
    per_b = lambda *shape: pl.BlockSpec((1,) + shape, lambda b: (b,) + (0,) * len(shape))
    return pl.pallas_call(
        kern,
        grid=(bsz,),
        in_specs=[
            pl.BlockSpec((seq, COL_G), lambda b: (b, 0)),
            pl.BlockSpec((seq, LANE), lambda b: (b, 0)),
            full(CONV_W, W_A), full(1, W_A), full(H_A, DH_A, DH_A), full(H_A, DH_A, DH_A), full(1, W_A),
            per_b(H_A, DH_A, DH_A), per_b(H_A, 1, DH_A), per_b(H_A, 1, LANE), per_b(CONV_W - 1, W_A),
        ],
        out_specs=[
            pl.BlockSpec((seq, W_A), lambda b: (b, 0)),
            per_b(H_A, DH_A, DH_A), per_b(H_A, 1, DH_A), per_b(H_A, 1, LANE), per_b(CONV_W - 1, W_A),
        ],
        out_shape=[
            jax.ShapeDtypeStruct((bsz * seq, W_A), BF16),
            jax.ShapeDtypeStruct((bsz, H_A, DH_A, DH_A), F32),
            jax.ShapeDtypeStruct((bsz, H_A, 1, DH_A), F32),
            jax.ShapeDtypeStruct((bsz, H_A, 1, LANE), F32),
            jax.ShapeDtypeStruct((bsz, CONV_W - 1, W_A), F32),
        ],
        scratch_shapes=[pltpu.VMEM((seq + 8, W_A), F32)],
        compiler_params=_params(("parallel",)),
        name="mlstm",
    )(a, g, cw, cb, wq, wk, gain, c0, n0, m0, buf)


def _diff_kernel(lam_ref, q_ref, k_ref, v_ref, near_ref, gain_ref, o_ref, kb_ref, vb_ref, m_ref, l_ref, acc_ref,
                 *, tq, lq, lk, q_pos0, chunked, lam_init):
    qi = pl.program_id(2)
    nq = pl.num_programs(2)
    rows_k = kb_ref.shape[0]

    @pl.when(qi == 0)
    def _():
        kb_ref[0:lk, :] = k_ref[0, 0].astype(BF16)
        vb_ref[0:lk, :] = v_ref[0, 0].astype(BF16)
        kb_ref[lk:, :] = jnp.zeros((rows_k - lk, 2 * DH_B), BF16)
        vb_ref[lk:, :] = jnp.zeros((rows_k - lk, 2 * DH_B), BF16)

    q0 = q_pos0 + qi * tq
    qt = q0 >> LOG2_KEY_TILE
    lane = lax.broadcasted_iota(jnp.int32, (1, 2 * DH_B), 1)
    bias_far = near_ref[0, 0, 0:1, 0:1]

    def process(rows):
        qf = q_ref[0, 0:rows, :].astype(F32) * (DH_B ** -0.5)
        qs = jnp.concatenate([jnp.where(lane < DH_B, qf, 0.0), jnp.where(lane >= DH_B, qf, 0.0)],
                             axis=0).astype(BF16)
        m_ref[0:2 * rows, :] = jnp.full((2 * rows, 1), NEG_BIG, F32)
        l_ref[0:2 * rows, :] = jnp.zeros((2 * rows, 1), F32)
        acc_ref[0:2 * rows, :] = jnp.zeros((2 * rows, 2 * DH_B), F32)

        def step(j0, width, mode):
            ks = pl.multiple_of(j0 * KEY_TILE, KEY_TILE)
            kt = kb_ref[pl.ds(ks, width * KEY_TILE), :]
            vt = vb_ref[pl.ds(ks, width * KEY_TILE), :]
            s = _dot_nt(qs, kt)
            if mode == "far":
                s = s + bias_far
            elif mode == "prev":
                bias = near_ref[0, 1, 0:rows, :]
                s = s + jnp.concatenate([bias, bias], axis=0)
            else:
                bias = jnp.concatenate([near_ref[0, 2, 0:rows, :], near_ref[0, 3, 0:rows, :]], axis=1)
                kpos = ks + lax.broadcasted_iota(jnp.int32, (rows, 2 * KEY_TILE), 1)
                if chunked:
                    qpos = q0 + lax.broadcasted_iota(jnp.int32, (rows, 2 * KEY_TILE), 0)
                    mask = _chunk_id(kpos) <= _chunk_id(qpos)
                else:
                    mask = kpos < lk
                s = jnp.where(jnp.concatenate([mask, mask], axis=0),
                              s + jnp.concatenate([bias, bias], axis=0), NEG_BIG)
            m_old = m_ref[0:2 * rows, :]
            m_new = jnp.maximum(m_old, jnp.max(s, axis=1, keepdims=True))
            alpha = jnp.exp(m_old - m_new)
            p = jnp.exp(s - m_new)
            l_ref[0:2 * rows, :] = alpha * l_ref[0:2 * rows, :] + jnp.sum(p, axis=1, keepdims=True)
            acc_ref[0:2 * rows, :] = alpha * acc_ref[0:2 * rows, :] + _dot(p.astype(BF16), vt)
            m_ref[0:2 * rows, :] = m_new

        step(qt, 2, "near")

        @pl.when(qt >= 1)
        def _():
            step(qt - 1, 1, "prev")

        nfar = jnp.maximum(qt - 1, 0)

        def body(g, carry):
            step(4 * g, 4, "far")
            return carry

        lax.fori_loop(0, nfar >> 2, body, 0)
        base = (nfar >> 2) << 2

        @pl.when((nfar & 2) != 0)
        def _():
            step(base, 2, "far")

        @pl.when((nfar & 1) != 0)
        def _():
            step(base + (nfar & 2), 1, "far")

        o = acc_ref[0:2 * rows, :] / l_ref[0:2 * rows, :]
        lam = lam_ref[:, 0:1]
        o = o[0:rows] - lam * o[rows:2 * rows]
        o = o * lax.rsqrt(jnp.mean(o * o, axis=1, keepdims=True) + LN_EPS) * gain_ref[...] * (1.0 - lam_init)
        o_ref[0, 0:rows, :] = o.astype(BF16)

    tail = lq % tq
    if tail == 0:
        process(tq)
    else:
        @pl.when(qi < nq - 1)
        def _():
            process(tq)

        @pl.when(qi == nq - 1)
        def _():
            process(tail)


def _diff_attention(q, k, v, near, lam, gain, *, tq, q_pos0, chunked, lam_init):
    bsz, lq, _ = q.shape
    lk = k.shape[2]
    nq = -(-lq // tq)
    nkt = -(-lk // KEY_TILE)
    kern = functools.partial(_diff_kernel, tq=tq, lq=lq, lk=lk, q_pos0=q_pos0, chunked=chunked, lam_init=lam_init)
    return pl.pallas_call(
        kern,
        grid=(bsz, H_B, nq),
        in_specs=[
            pl.BlockSpec((1, LANE), lambda b, h, i: (0, 0)),
            pl.BlockSpec((1, tq, 2 * DH_B), lambda b, h, i: (b, i, h)),
            pl.BlockSpec((1, 1, lk, 2 * DH_B), lambda b, h, i: (b, h, 0, 0)),
            pl.BlockSpec((1, 1, lk, 2 * DH_B), lambda b, h, i: (b, h, 0, 0)),
            pl.BlockSpec((1, 4, tq, KEY_TILE), lambda b, h, i: (h, 0, 0, 0)),
            pl.BlockSpec((1, 2 * DH_B), lambda b, h, i: (0, 0)),
        ],
        out_specs=pl.BlockSpec((1, tq, 2 * DH_B), lambda b, h, i: (b, i, h)),
        out_shape=jax.ShapeDtypeStruct((bsz, lq, W_B), BF16),
        scratch_shapes=[
            pltpu.VMEM(((nkt + 1) * KEY_TILE, 2 * DH_B), BF16),
            pltpu.VMEM(((nkt + 1) * KEY_TILE, 2 * DH_B), BF16),
            pltpu.VMEM((2 * tq, 1), F32),
            pltpu.VMEM((2 * tq, 1), F32),
            pltpu.VMEM((2 * tq, 2 * DH_B), F32),
        ],
        compiler_params=_params(("parallel", "parallel", "arbitrary")),
        name="diff_attention",
    )(lam, q, k, v, near, gain)


def _inproj_c_kernel(x_ref, w_ref, q_ref, k_ref, v_ref):
    xb = x_ref[...].astype(BF16)
    q_ref[...] = _dot(xb, w_ref[:, 0:D_MODEL]).astype(BF16)
    yk = _dot(xb, w_ref[:, D_MODEL:2 * D_MODEL])
    for h in range(H_C):
        k_ref[0, h] = yk[:, DH_C * h:DH_C * (h + 1)]
    yv = _dot(xb, w_ref[:, 2 * D_MODEL:3 * D_MODEL])
    for h in range(H_C):
        v_ref[0, h] = yv[:, DH_C * h:DH_C * (h + 1)]


def _inproj_c(x_all, row0, bsz, seq, tm, w):
    nrt = seq // tm
    if row0 % tm:
        x_all, row0 = x_all[row0:row0 + bsz * seq], 0
    off = row0 // tm
    return pl.pallas_call(
        _inproj_c_kernel,
        grid=(bsz, nrt),
        in_specs=[
            pl.BlockSpec((tm, D_MODEL), lambda b, r: (off + b * nrt + r, 0)),
            pl.BlockSpec((D_MODEL, 3 * D_MODEL), lambda b, r: (0, 0)),
        ],
        out_specs=[
            pl.BlockSpec((tm, D_MODEL), lambda b, r: (b * nrt + r, 0)),
            pl.BlockSpec((1, H_C, tm, DH_C), lambda b, r: (b, 0, r, 0)),
            pl.BlockSpec((1, H_C, tm, DH_C), lambda b, r: (b, 0, r, 0)),
        ],
        out_shape=[
            jax.ShapeDtypeStruct((bsz * seq, D_MODEL), BF16),
            jax.ShapeDtypeStruct((bsz, H_C, seq, DH_C), F32),
            jax.ShapeDtypeStruct((bsz, H_C, seq, DH_C), F32),
        ],
        compiler_params=_params(("parallel", "parallel")),
        name="inproj_c",
    )(x_all, w)


def _sb_kernel(q_ref, k_ref, v_ref, o_ref, kb_ref, vb_ref, acc_ref, run_ref, *, tq, lq, lk, q_pos0):
    qi = pl.program_id(2)
    nq = pl.num_programs(2)
    rows_k = kb_ref.shape[0]

    @pl.when(qi == 0)
    def _():
        kb_ref[0:lk, :] = jnp.concatenate([k_ref[0, 0], k_ref[0, 1]], axis=1).astype(BF16)
        vb_ref[0:lk, :] = jnp.concatenate([v_ref[0, 0], v_ref[0, 1]], axis=1).astype(BF16)
        if rows_k > lk:
            kb_ref[lk:, :] = jnp.zeros((rows_k - lk, 2 * DH_C), BF16)
            vb_ref[lk:, :] = jnp.zeros((rows_k - lk, 2 * DH_C), BF16)

    q0 = q_pos0 + qi * tq
    jd = q0 >> LOG2_KEY_TILE
    later = (lax.broadcasted_iota(jnp.int32, (KEY_TILE, KEY_TILE), 0)
             > lax.broadcasted_iota(jnp.int32, (KEY_TILE, KEY_TILE), 1)).astype(BF16)
    lane = lax.broadcasted_iota(jnp.int32, (1, 2 * DH_C), 1)

    def process(rows):
        qf = q_ref[0, 0:rows, :].astype(F32) * (DH_C ** -0.5)
        qh = [jnp.where(lane < DH_C, qf, 0.0).astype(BF16), jnp.where(lane >= DH_C, qf, 0.0).astype(BF16)]
        acc_ref[:, 0:rows, :] = jnp.zeros((2, rows, 2 * DH_C), F32)
        run_ref[:, 0:rows, :] = jnp.zeros((2, rows, 1), F32)
        qpos = q0 + lax.broadcasted_iota(jnp.int32, (rows, KEY_TILE), 0)
        kcol = lax.broadcasted_iota(jnp.int32, (rows, KEY_TILE), 1)

        def step(j0, width, masked):
            ks = pl.multiple_of(j0 * KEY_TILE, KEY_TILE)
            kt = kb_ref[pl.ds(ks, width * KEY_TILE), :]
            vt = vb_ref[pl.ds(ks, width * KEY_TILE), :]
            for h in range(2):
                z = _dot_nt(qh[h], kt)
                log_stay = -(jnp.maximum(z, 0.0) + jnp.log(1.0 + jnp.exp(-jnp.abs(z))))
                if masked:
                    mask = (ks + kcol) < qpos
                    log_stay = jnp.where(mask, log_stay, 0.0)
                off = run_ref[h, 0:rows, :]
                parts = [None] * width
                for g in reversed(range(width)):
                    cols = slice(g * KEY_TILE, (g + 1) * KEY_TILE)
                    ls = log_stay[:, cols]
                    hi = ls.astype(BF16)
                    lo = (ls - hi.astype(F32)).astype(BF16)
                    log_after = _dot(hi, later) + _dot(lo, later) + off
                    a = jnp.exp(z[:, cols] + ls + log_after)
                    if masked:
                        a = jnp.where(mask, a, 0.0)
                    parts[g] = a.astype(BF16)
                    off = off + jnp.sum(ls, axis=1, keepdims=True)
                a_all = parts[0] if width == 1 else jnp.concatenate(parts, axis=1)
                acc_ref[h, 0:rows, :] += _dot(a_all, vt)
                run_ref[h, 0:rows, :] = off

        step(jd, 1, True)

        def body(g, carry):
            step(jd - 4 * (g + 1), 4, False)
            return carry

        lax.fori_loop(0, jd >> 2, body, 0)
        rem = jd & 3

        @pl.when((rem & 2) != 0)
        def _():
            step(rem & 1, 2, False)

        @pl.when((rem & 1) != 0)
        def _():
            step(0, 1, False)

        o_ref[0, 0:rows, :] = jnp.where(lane < DH_C, acc_ref[0, 0:rows, :], acc_ref[1, 0:rows, :]).astype(BF16)

    tail = lq % tq
    if tail == 0:
        process(tq)
    else:
        @pl.when(qi < nq - 1)
        def _():
            process(tq)

        @pl.when(qi == nq - 1)
        def _():
            process(tail)


def _stick_breaking(q, k, v, *, tq, q_pos0):
    bsz, lq, _ = q.shape
    lk = k.shape[2]
    nq = -(-lq // tq)
    nkt = -(-lk // KEY_TILE)
    kern = functools.partial(_sb_kernel, tq=tq, lq=lq, lk=lk, q_pos0=q_pos0)
    return pl.pallas_call(
        kern,
        grid=(bsz, H_C // 2, nq),
        in_specs=[
            pl.BlockSpec((1, tq, 2 * DH_C), lambda b, h, i: (b, i, h)),
            pl.BlockSpec((1, 2, lk, DH_C), lambda b, h, i: (b, h, 0, 0)),
            pl.BlockSpec((1, 2, lk, DH_C), lambda b, h, i: (b, h, 0, 0)),
        ],
        out_specs=pl.BlockSpec((1, tq, 2 * DH_C), lambda b, h, i: (b, i, h)),
        out_shape=jax.ShapeDtypeStruct((bsz, lq, D_MODEL), BF16),
        scratch_shapes=[
            pltpu.VMEM((nkt * KEY_TILE, 2 * DH_C), BF16),
            pltpu.VMEM((nkt * KEY_TILE, 2 * DH_C), BF16),
            pltpu.VMEM((2, tq, 2 * DH_C), F32),
            pltpu.VMEM((2, tq, 1), F32),
        ],
        compiler_params=_params(("parallel", "parallel", "arbitrary")),
        name="stick_breaking",
    )(q, k, v)


def _layer_norm(z, g, b):
    mu = jnp.mean(z, axis=1, keepdims=True)
    zc = z - mu
    var = jnp.mean(zc * zc, axis=1, keepdims=True)
    return zc * lax.rsqrt(var + LN_EPS) * g + b


def _outproj_kernel(*refs, n_in):
    mix_refs = refs[:n_in]
    (w_ref, x_ref, g_ref, b_ref, wr_ref, br_ref,
     x1_ref, x1b_ref, ti_ref, tg_ref, rk_ref, cnt_ref, tri_ref) = refs[n_in:]
    i = pl.program_id(0)
    tm = x_ref.shape[0]

    @pl.when(i == 0)
    def _():
        cnt_ref[...] = jnp.zeros_like(cnt_ref)
        tri_ref[...] = (lax.broadcasted_iota(jnp.int32, (tm, tm), 0)
                        < lax.broadcasted_iota(jnp.int32, (tm, tm), 1)).astype(BF16)

    y = None
    c0 = 0
    for r in mix_refs:
        wd = r.shape[1]
        part = _dot(r[...], w_ref[c0:c0 + wd, :])
        y = part if y is None else y + part
        c0 += wd
    x1 = _layer_norm(DN_ALPHA * x_ref[...] + y, g_ref[...], b_ref[...])
    x1_ref[...] = x1
    x1b_ref[...] = x1.astype(BF16)
    logits = _dot_nt(wr_ref[...], x1, precision=lax.Precision.HIGHEST) + br_ref[...]
    ei = lax.broadcasted_iota(jnp.int32, (N_EXPERTS, tm), 0)
    cur = logits
    vals, idxs = [], []
    for _k in range(TOP_K):
        mx = jnp.max(cur, axis=0, keepdims=True)
        ix = jnp.min(jnp.where(cur == mx, ei, N_EXPERTS), axis=0, keepdims=True)
        vals.append(mx)
        idxs.append(ix)
        cur = jnp.where(ei == ix, -jnp.inf, cur)
    es = [jnp.exp(vv - vals[0]) for vv in vals]
    tot = es[0] + es[1] + es[2] + es[3]
    ti_ref[...] = jnp.concatenate(idxs, axis=0)
    tg_ref[...] = jnp.concatenate([e / tot for e in es], axis=0)
    run = cnt_ref[...]
    ranks = []
    for kk in range(TOP_K):
        oh = (ei == idxs[kk])
        before = _dot(oh.astype(BF16), tri_ref[...])
        ranks.append(jnp.sum(jnp.where(oh, before + run, 0.0), axis=0, keepdims=True))
        run = run + jnp.sum(oh.astype(F32), axis=1, keepdims=True)
    rk_ref[...] = jnp.concatenate(ranks, axis=0).astype(jnp.int32)
    cnt_ref[...] = run


def _outproj_ln_router(mixes, w, x_all, g, b, wr_t, br):
    n = x_all.shape[0]
    tm = TOKEN_TILE
    kern = functools.partial(_outproj_kernel, n_in=len(mixes))
    row = lambda wd: pl.BlockSpec((tm, wd), lambda i: (i, 0))
    full = lambda *shape: pl.BlockSpec(shape, lambda i: (0,) * len(shape))
    lanes = lambda rows: pl.BlockSpec((rows, tm), lambda i: (0, i))
    return pl.pallas_call(
        kern,
        grid=(n // tm,),
        in_specs=[row(m.shape[1]) for m in mixes] + [
            full(D_MODEL, D_MODEL), row(D_MODEL), full(1, D_MODEL), full(1, D_MODEL),
            full(N_EXPERTS, D_MODEL), full(N_EXPERTS, 1)],
        out_specs=[row(D_MODEL), row(D_MODEL), lanes(TOP_K), lanes(TOP_K), lanes(TOP_K), full(N_EXPERTS, 1)],
        out_shape=[
            jax.ShapeDtypeStruct((n, D_MODEL), F32),
            jax.ShapeDtypeStruct((n, D_MODEL), BF16),
            jax.ShapeDtypeStruct((TOP_K, n), jnp.int32),
            jax.ShapeDtypeStruct((TOP_K, n), F32),
            jax.ShapeDtypeStruct((TOP_K, n), jnp.int32),
            jax.ShapeDtypeStruct((N_EXPERTS, 1), F32),
        ],
        scratch_shapes=[pltpu.VMEM((tm, tm), BF16)],
        compiler_params=_params(("arbitrary",)),
        name="outproj_ln_router",
    )(*mixes, w, x_all, g, b, wr_t, br)


def _moe_kernel(be_ref, nu_ref, x_ref, wgu_ref, bgu_ref, wdn_ref, bdn_ref, o_ref):
    @pl.when(pl.program_id(0) < nu_ref[0])
    def _():
        h = _dot(x_ref[...], wgu_ref[0]) + bgu_ref[0]
        glu = jnp.minimum(h[:, 0:D_EXPERT], SWIGLU_LIMIT)
        lin = jnp.clip(h[:, D_EXPERT:2 * D_EXPERT], -SWIGLU_LIMIT, SWIGLU_LIMIT)
        act = glu * _sigmoid(SWIGLU_ALPHA * glu) * (lin + 1.0)
        o_ref[...] = (_dot(act.astype(BF16), wdn_ref[0]) + bdn_ref[0]).astype(BF16)


def _moe_experts(blk_e, n_used, xg, wgu, bgu, wdn, bdn):
    nb = blk_e.shape[0]
    grid_spec = pltpu.PrefetchScalarGridSpec(
        num_scalar_prefetch=2,
        grid=(nb,),
        in_specs=[
            pl.BlockSpec((MOE_BLOCK, D_MODEL), lambda i, be, nu: (i, 0)),
            pl.BlockSpec((1, D_MODEL, 2 * D_EXPERT), lambda i, be, nu: (be[i], 0, 0)),
            pl.BlockSpec((1, 1, 2 * D_EXPERT), lambda i, be, nu: (be[i], 0, 0)),
            pl.BlockSpec((1, D_EXPERT, D_MODEL), lambda i, be, nu: (be[i], 0, 0)),
            pl.BlockSpec((1, 1, D_MODEL), lambda i, be, nu: (be[i], 0, 0)),
        ],
        out_specs=pl.BlockSpec((MOE_BLOCK, D_MODEL), lambda i, be, nu: (i, 0)),
    )
    return pl.pallas_call(
        _moe_kernel,
        grid_spec=grid_spec,
        out_shape=jax.ShapeDtypeStruct((nb * MOE_BLOCK, D_MODEL), BF16),
        compiler_params=_params(("arbitrary",)),
        name="moe_experts",
    )(blk_e, n_used, xg, wgu, bgu, wdn, bdn)


def _deinterleave_kernel(w_ref, p_ref, o_ref):
    o_ref[0] = _dot(w_ref[0].astype(BF16), p_ref[...]).astype(BF16)


def _deinterleave_gu(w):
    n, d, f2 = w.shape
    col = jnp.arange(f2, dtype=jnp.int32)
    src = jnp.where(col < f2 // 2, 2 * col, 2 * (col - f2 // 2) + 1)
    perm = (col[:, None] == src[None, :]).astype(BF16)
    rows = 512
    return pl.pallas_call(
        _deinterleave_kernel,
        grid=(n, d // rows),
        in_specs=[
            pl.BlockSpec((1, rows, f2), lambda e, r: (e, r, 0)),
            pl.BlockSpec((f2, f2), lambda e, r: (0, 0)),
        ],
        out_specs=pl.BlockSpec((1, rows, f2), lambda e, r: (e, r, 0)),
        out_shape=jax.ShapeDtypeStruct((n, d, f2), BF16),
        compiler_params=_params(("parallel", "parallel")),
        name="deinterleave_gu",
    )(w, perm)


def _combine_ln_kernel(x_ref, f_ref, g_ref, b_ref, o_ref):
    o_ref[...] = _layer_norm(DN_ALPHA * x_ref[...] + f_ref[...], g_ref[...], b_ref[...])


def _combine_ln(x1, f, g, b):
    n = x1.shape[0]
    tm = TOKEN_TILE
    row = pl.BlockSpec((tm, D_MODEL), lambda i: (i, 0))
    vec = pl.BlockSpec((1, D_MODEL), lambda i: (0, 0))
    return pl.pallas_call(
        _combine_ln_kernel,
        grid=(n // tm,),
        in_specs=[row, row, vec, vec],
        out_specs=row,
        out_shape=jax.ShapeDtypeStruct((n, D_MODEL), F32),
        compiler_params=_params(("parallel",)),
        name="combine_ln",
    )(x1, f, g, b)


def _moe_layer(x1, x1b, top_i, top_g, rank, sizes, e0, wgu, bgu, wdn, bdn, g, b):
    n = x1.shape[0]
    nb = -(-(n * TOP_K) // MOE_BLOCK) + N_EXPERTS
    sizes = sizes[:, 0].astype(jnp.int32)
    nblk = (sizes + MOE_BLOCK - 1) // MOE_BLOCK
    blk_end = jnp.cumsum(nblk)
    pad_starts = (blk_end - nblk) * MOE_BLOCK
    n_used = blk_end[-1]
    blk_ids = jnp.arange(nb, dtype=jnp.int32)
    blk_e = jnp.searchsorted(blk_end, jnp.minimum(blk_ids, n_used - 1), side='right').astype(jnp.int32)
    blk_e = jnp.clip(blk_e, 0, N_EXPERTS - 1)
    dest = pad_starts[top_i] + rank
    tok = jnp.broadcast_to(jnp.arange(n, dtype=jnp.int32)[None], (TOP_K, n))
    tok_of_row = jnp.zeros((nb * MOE_BLOCK,), jnp.int32).at[dest.reshape(-1)].set(
        tok.reshape(-1), unique_indices=True, indices_are_sorted=False)
    xg = jnp.take(x1b, tok_of_row, axis=0)
    out = _moe_experts(blk_e + e0, n_used.reshape(1), xg, wgu, bgu, wdn, bdn)
    f = jnp.zeros((n, D_MODEL), F32)
    for kk in range(TOP_K):
        f = f + jnp.take(out, dest[kk], axis=0).astype(F32) * top_g[kk][:, None]
    return _combine_ln(x1, f, g, b)


def _rel_bucket(rel):
    half = N_BUCKETS // 2
    exact = half // 2
    ret = jnp.where(rel > 0, half, 0)
    n = jnp.abs(rel)
    large = exact + (jnp.log(jnp.maximum(n, 1).astype(F32) / exact)
                     / math.log(MAX_DISTANCE / exact) * (half - exact)).astype(jnp.int32)
    large = jnp.minimum(large, half - 1)
    return ret + jnp.where(n < exact, n, large)


def _near_bias_tiles(rel_table, tq, q_pos0):
    base = q_pos0 % KEY_TILE
    i = jnp.arange(tq, dtype=jnp.int32)[:, None] + base
    j = jnp.arange(KEY_TILE, dtype=jnp.int32)[None, :]
    tiles = [jnp.full((tq, KEY_TILE), -2 * MAX_DISTANCE, jnp.int32)]
    for d in (-KEY_TILE, 0, KEY_TILE):
        tiles.append(d + j - i)
    rel = jnp.stack(tiles)
    return jnp.transpose(rel_table[_rel_bucket(rel)], (3, 0, 1, 2)).astype(F32)


def kernel(x_prompt, x_sample, cache_diff_k, cache_diff_v, state_mlstm_C, state_mlstm_n, state_mlstm_m, state_mlstm_conv, cache_sb_k, cache_sb_v, meta_tokens, rel_bias, w_in_ab, w_out_ab, conv_w_a, conv_b_a, w_aq_a, w_ak_a, b_if_a, mh_gain_a, lam_q1, lam_k1, lam_q2, lam_k2, subln_gain_b, w_in_c, w_out_c, ln_g, ln_b, w_router, b_router, w_gu, b_gu, w_down, b_down):
    bp, sp, _ = x_prompt.shape
    bs, ss, _ = x_sample.shape
    lp = N_META + sp
    past = cache_diff_k.shape[3]
    n_p = bp * lp
    n_s = bs * ss
    tm_p = lp // 3 if (lp % 3 == 0 and (lp // 3) % 16 == 0) else lp
    tq_p = KEY_TILE

    x0 = jnp.concatenate([jnp.broadcast_to(meta_tokens[None], (bp, N_META, D_MODEL)).astype(x_prompt.dtype),
                          x_prompt], axis=1)
    x_all = jnp.concatenate([x0.reshape(n_p, D_MODEL), x_sample.reshape(n_s, D_MODEL)], axis=0)

    j = 0
    lam_init = 0.8 - 0.6 * math.exp(-0.3 * 0)
    w_ab = w_in_ab[j]
    w_perm = jnp.concatenate([
        w_ab[:, 0:OFF_AIF],
        jnp.pad(w_ab[:, OFF_AIF:OFF_BQ], ((0, 0), (0, LANE - 2 * H_A))),
        w_ab[:, OFF_BQ:]], axis=1).astype(BF16)
    bg = jnp.pad(b_if_a[j], (0, LANE - 2 * H_A)).reshape(1, LANE)
    lam = (jnp.exp(jnp.sum(lam_q1[j] * lam_k1[j])) - jnp.exp(jnp.sum(lam_q2[j] * lam_k2[j]))).astype(F32) + lam_init
    lam_v = jnp.broadcast_to(lam.reshape(1, 1), (1, LANE))
    cw = conv_w_a[j]
    cb = conv_b_a[j].reshape(1, W_A)
    wq = w_aq_a[j].astype(BF16)
    wk = w_ak_a[j].astype(BF16)
    gain_a = mh_gain_a[j].reshape(1, W_A)
    gain_b = subln_gain_b[j].reshape(1, 2 * DH_B)

    def ab_group(row0, bsz, seq, tm, lead, state, past_kv, tq, q_pos0, chunked):
        a, g, q, k_new, v_new = _inproj_ab(x_all, row0, bsz, seq, tm, w_perm, bg)
        c0, n0, m0, buf = state
        mix_a, c1, n1, m1, cs = _mlstm(
            a, g, bsz, seq, lead, cw, cb, wq, wk, gain_a,
            c0, n0.reshape(bsz, H_A, 1, DH_A),
            jnp.broadcast_to(m0[:, :, None, None], (bsz, H_A, 1, LANE)), buf)
        if past_kv is None:
            k_all, v_all = k_new, v_new
        else:
            k_all = jnp.concatenate([past_kv[0], k_new], axis=2)
            v_all = jnp.concatenate([past_kv[1], v_new], axis=2)
        near = _near_bias_tiles(rel_bias, tq, q_pos0)
        mix_b = _diff_attention(q.reshape(bsz, seq, W_B), k_all, v_all, near, lam_v, gain_b,
                                tq=tq, q_pos0=q_pos0, chunked=chunked, lam_init=lam_init)
        outs = (k_new, v_new, c1, n1.reshape(bsz, H_A, DH_A), m1[:, :, 0, 0], cs)
        return mix_a, mix_b.reshape(bsz * seq, W_B), outs

    zero_state = (jnp.zeros((bp, H_A, DH_A, DH_A), F32), jnp.zeros((bp, H_A, DH_A), F32),
                  jnp.zeros((bp, H_A), F32), jnp.zeros((bp, CONV_W - 1, W_A), F32))
    mix_a_p, mix_b_p, ab_p = ab_group(0, bp, lp, tm_p, N_META, zero_state, None, tq_p, 0, True)
    s_state = (state_mlstm_C[j], state_mlstm_n[j], state_mlstm_m[j], state_mlstm_conv[j])
    mix_a_s, mix_b_s, ab_s = ab_group(n_p, bs, ss, ss, 0, s_state, (cache_diff_k[j], cache_diff_v[j]),
                                      ss, past, False)
    mix_a = jnp.concatenate([mix_a_p, mix_a_s], axis=0)
    mix_b = jnp.concatenate([mix_b_p, mix_b_s], axis=0)

    n_le = w_gu.shape[0] * N_EXPERTS
    wgu = _deinterleave_gu(w_gu.reshape(n_le, D_MODEL, 2 * D_EXPERT))
    bgu = jnp.concatenate([b_gu[..., 0::2], b_gu[..., 1::2]], axis=-1).reshape(n_le, 1, 2 * D_EXPERT)
    wdn = w_down.reshape(n_le, D_EXPERT, D_MODEL).astype(BF16)
    bdn = b_down.reshape(n_le, 1, D_MODEL)

    def token_stage(layer, mixes, w_out, x_in):
        wr_t = jnp.transpose(w_router[layer])
        br = b_router[layer].reshape(N_EXPERTS, 1)
        x1, x1b, top_i, top_g, rank, sizes = _outproj_ln_router(
            mixes, w_out.astype(BF16), x_in, ln_g[layer, 0].reshape(1, D_MODEL), ln_b[layer, 0].reshape(1, D_MODEL),
            wr_t, br)
        return _moe_layer(x1, x1b, top_i, top_g, rank, sizes, layer * N_EXPERTS, wgu, bgu, wdn, bdn,
                          ln_g[layer, 1].reshape(1, D_MODEL), ln_b[layer, 1].reshape(1, D_MODEL))

    x_all = token_stage(0, [mix_a, mix_b], w_out_ab[j], x_all)

    w_c = w_in_c[j].astype(BF16)

    def c_group(row0, bsz, seq, tm, past_kv, tq, q_pos0):
        q, k_new, v_new = _inproj_c(x_all, row0, bsz, seq, tm, w_c)
        if past_kv is None:
            k_all, v_all = k_new, v_new
        else:
            k_all = jnp.concatenate([past_kv[0], k_new], axis=2)
            v_all = jnp.concatenate([past_kv[1], v_new], axis=2)
        o = _stick_breaking(q.reshape(bsz, seq, D_MODEL), k_all, v_all, tq=tq, q_pos0=q_pos0)
        return o.reshape(bsz * seq, D_MODEL), (k_new, v_new)

    mix_p, c_p = c_group(0, bp, lp, tm_p, None, tq_p, 0)
    mix_s, c_s = c_group(n_p, bs, ss, ss, (cache_sb_k[j], cache_sb_v[j]), ss, past)
    x_all = token_stage(1, [jnp.concatenate([mix_p, mix_s], axis=0)], w_out_c[j], x_all)

    y_prompt = x_all[:n_p].reshape(bp, lp, D_MODEL)[:, N_META:]
    y_sample = x_all[n_p:].reshape(bs, ss, D_MODEL)
    stack = lambda t: t[None]
    return (y_prompt, y_sample,
            stack(ab_p[0]), stack(ab_p[1]), stack(ab_p[2]), stack(ab_p[3]), stack(ab_p[4]), stack(ab_p[5]),
            stack(c_p[0]), stack(c_p[1]),
            stack(ab_s[0]), stack(ab_s[1]), stack(ab_s[2]), stack(ab_s[3]), stack(ab_s[4]), stack(ab_s[5]),
            stack(c_s[0]), stack(c_s[1]))
```

```python
import functools
import math

import jax
import jax.numpy as jnp
from jax import lax
from jax.experimental import pallas as pl
from jax.experimental.pallas import tpu as pltpu

F32 = jnp.float32
BF16 = jnp.bfloat16

D_MODEL = 1024
DEPTH = 2
CHUNK = 64
N_META = 16
H_A = 4
DH_A = 128
W_A = H_A * DH_A
CONV_W = 4
H_B = 4
DH_B = 64
W_B = H_B * 2 * DH_B
H_C = 16
DH_C = 64
N_BUCKETS = 32
MAX_DISTANCE = 128
N_EXPERTS = 32
TOP_K = 4
D_EXPERT = D_MODEL // 2
SWIGLU_LIMIT = 7.0
SWIGLU_ALPHA = 1.702
DN_ALPHA = (2 * DEPTH) ** 0.25
LN_EPS = 1e-5
OFF_AIF = 3 * W_A
OFF_BQ = OFF_AIF + 2 * H_A

LANE = 128
KEY_TILE = 128
LOG2_KEY_TILE = 7
LOG2_CHUNK = 6
VMEM_LIMIT = 56 * 1024 * 1024
MOE_BLOCK = 512
TOKEN_TILE = 512
SB_GROUP = 8
NEG_BIG = -1e30
LOG2_E = 1.4426950408889634

COL_G = 3 * W_A
COL_Q = COL_G + LANE
COL_K = COL_Q + W_B
COL_V = COL_K + W_B
D_IN_AB_PAD = COL_V + W_B


def _dot(a, b):
    return jnp.dot(a, b, preferred_element_type=F32)


def _dot_nt(a, b, precision=None):
    return lax.dot_general(a, b, (((1,), (1,)), ((), ())), preferred_element_type=F32, precision=precision)


def _dot_tn(a, b):
    return lax.dot_general(a, b, (((0,), (0,)), ((), ())), preferred_element_type=F32)


def _log_sigmoid(x):
    return jnp.minimum(x, 0.0) - jnp.log(1.0 + jnp.exp(-jnp.abs(x)))


def _sigmoid(x):
    return 1.0 / (1.0 + jnp.exp(-x))


def _chunk_id(pos):
    return (pos + (CHUNK - N_META)) >> LOG2_CHUNK


def _params(sem):
    return pltpu.CompilerParams(dimension_semantics=sem, vmem_limit_bytes=VMEM_LIMIT)


def _inproj_ab_kernel(x_ref, w_ref, bg_ref, a_ref, g_ref, q_ref, k_ref, v_ref):
    xb = x_ref[...].astype(BF16)
    a_ref[...] = _dot(xb, w_ref[:, 0:COL_G])
    g_ref[...] = _dot(xb, w_ref[:, COL_G:COL_Q]) + bg_ref[...]
    q_ref[...] = _dot(xb, w_ref[:, COL_Q:COL_K]).astype(BF16)
    for h in range(H_B):
        k_ref[0, h] = _dot(xb, w_ref[:, COL_K + 2 * DH_B * h:COL_K + 2 * DH_B * (h + 1)])
        v_ref[0, h] = _dot(xb, w_ref[:, COL_V + 2 * DH_B * h:COL_V + 2 * DH_B * (h + 1)])


def _inproj_ab(x_all, row0, bsz, seq, tm, w, bg):
    nrt = seq // tm
    n = bsz * seq
    if row0 % tm:
        x_all, row0 = x_all[row0:row0 + n], 0
    off = row0 // tm
    return pl.pallas_call(
        _inproj_ab_kernel,
        grid=(bsz, nrt),
        in_specs=[
            pl.BlockSpec((tm, D_MODEL), lambda b, r: (off + b * nrt + r, 0)),
            pl.BlockSpec((D_MODEL, D_IN_AB_PAD), lambda b, r: (0, 0)),
            pl.BlockSpec((1, LANE), lambda b, r: (0, 0)),
        ],
        out_specs=[
            pl.BlockSpec((tm, COL_G), lambda b, r: (b * nrt + r, 0)),
            pl.BlockSpec((tm, LANE), lambda b, r: (b * nrt + r, 0)),
            pl.BlockSpec((tm, W_B), lambda b, r: (b * nrt + r, 0)),
            pl.BlockSpec((1, H_B, tm, 2 * DH_B), lambda b, r: (b, 0, r, 0)),
            pl.BlockSpec((1, H_B, tm, 2 * DH_B), lambda b, r: (b, 0, r, 0)),
        ],
        out_shape=[
            jax.ShapeDtypeStruct((n, COL_G), F32),
            jax.ShapeDtypeStruct((n, LANE), F32),
            jax.ShapeDtypeStruct((n, W_B), BF16),
            jax.ShapeDtypeStruct((bsz, H_B, seq, 2 * DH_B), F32),
            jax.ShapeDtypeStruct((bsz, H_B, seq, 2 * DH_B), F32),
        ],
        compiler_params=_params(("parallel", "parallel")),
        name="inproj_ab",
    )(x_all, w, bg)


def _mlstm_kernel(a_ref, g_ref, cw_ref, cb_ref, wq_ref, wk_ref, gain_ref, c0_ref, n0_ref, m0_ref, buf_ref,
                  out_ref, c1_ref, n1_ref, m1_ref, cs_ref, xp_ref, *, seq, lead):
    xp_ref[0:8, :] = jnp.zeros((8, W_A), F32)
    xp_ref[5:8, :] = buf_ref[0]
    xp_ref[8:8 + seq, :] = a_ref[:, 0:W_A]
    cs_ref[0] = xp_ref[5 + seq:8 + seq, :]
    c1_ref[...] = c0_ref[...]
    n1_ref[...] = n0_ref[...]
    m1_ref[...] = m0_ref[...]
    sel = (lax.broadcasted_iota(jnp.int32, (8, LANE), 0) == lax.broadcasted_iota(jnp.int32, (8, LANE), 1)).astype(F32)

    def chunk(r0, lc):
        win = xp_ref[pl.ds(r0, lc + 8), :]
        y = cb_ref[...]
        for j in range(CONV_W):
            y = y + win[5 + j:5 + j + lc, :] * cw_ref[j:j + 1, :]
        ca = y * _sigmoid(y)
        g = g_ref[pl.ds(r0, lc), :]
        g_rows = _dot_nt(sel, g, precision=lax.Precision.HIGHEST)
        ti = lax.broadcasted_iota(jnp.int32, (lc, lc), 0)
        si = lax.broadcasted_iota(jnp.int32, (lc, lc), 1)
        causal = si <= ti
        for h in range(H_A):
            hs = slice(DH_A * h, DH_A * (h + 1))
            ig_c = g[:, h:h + 1]
            lf_c = _log_sigmoid(g[:, H_A + h:H_A + h + 1])
            ig_r = g_rows[h:h + 1, :]
            lf_r = _log_sigmoid(g_rows[H_A + h:H_A + h + 1, :])
            b_c = jnp.sum(jnp.where(causal, lf_r, 0.0), axis=1, keepdims=True)
            b_r = jnp.sum(jnp.where(ti <= si, lf_c, 0.0), axis=0, keepdims=True)
            m_prev = m1_ref[0, h][:, 0:1]
            c_prev = c1_ref[0, h]
            n_prev = n1_ref[0, h]
            dmat = jnp.where(causal, b_c - b_r + ig_r, -jnp.inf)
            inter = b_c + m_prev
            m_t = jnp.maximum(inter, jnp.max(dmat, axis=1, keepdims=True))
            w = jnp.exp(dmat - m_t)
            gg = jnp.exp(inter - m_t)
            cab = ca[:, hs].astype(BF16)
            q = _dot(cab, wq_ref[h])
            k = _dot(cab, wk_ref[h]) * (DH_A ** -0.5)
            qb = q.astype(BF16)
            v = a_ref[pl.ds(r0, lc), W_A + DH_A * h:W_A + DH_A * (h + 1)]
            vb = v.astype(BF16)
            s = _dot_nt(qb, k.astype(BF16)) * w
            num = _dot(s.astype(BF16), vb) + gg * _dot(qb, c_prev.astype(BF16))
            den = jnp.sum(s, axis=1, keepdims=True) + gg * jnp.sum(q * n_prev, axis=1, keepdims=True)
            hh = num / jnp.maximum(jnp.abs(den), jnp.exp(-m_t))
            m_new = m_t[lc - 1:lc, :]
            b_last = b_c[lc - 1:lc, :]
            w_end = jnp.exp(b_last - b_c + ig_c - m_new)
            decay = jnp.exp(b_last + m_prev - m_new)
            kw = k * w_end
            c1_ref[0, h] = decay * c_prev + _dot_tn(kw.astype(BF16), vb)
            n1_ref[0, h] = decay * n_prev + jnp.sum(kw, axis=0, keepdims=True)
            m1_ref[0, h] = jnp.broadcast_to(m_new, (1, LANE))
            hn = hh * lax.rsqrt(jnp.mean(hh * hh, axis=1, keepdims=True) + LN_EPS) * gain_ref[:, hs]
            oa = a_ref[pl.ds(r0, lc), 2 * W_A + DH_A * h:2 * W_A + DH_A * (h + 1)]
            out_ref[pl.ds(r0, lc), hs] = (hn * _sigmoid(oa)).astype(BF16)

    if lead:
        chunk(0, lead)
    nch = (seq - lead) // CHUNK
    if nch == 1:
        chunk(lead, CHUNK)
    else:
        def body(c, carry):
            chunk(pl.multiple_of(lead + c * CHUNK, 16), CHUNK)
            return carry
        lax.fori_loop(0, nch, body, 0)


def _mlstm(a, g, bsz, seq, lead, cw, cb, wq, wk, gain, c0, n0, m0, buf):
    kern = functools.partial(_mlstm_kernel, seq=seq, lead=lead)
    full = lambda *shape: pl.BlockSpec(shape, lambda b: (0,) * len(shape))
    per_b = lambda *shape: pl.BlockSpec((1,) + shape, lambda b: (b,) + (0,) * len(shape))
    return pl.pallas_call(
        kern,
        grid=(bsz,),
        in_specs=[
            pl.BlockSpec((seq, COL_G), lambda b: (b, 0)),
            pl.BlockSpec((seq, LANE), lambda b: (b, 0)),
            full(CONV_W, W_A), full(1, W_A), full(H_A, DH_A, DH_A), full(H_A, DH_A, DH_A), full(1, W_A),
            per_b(H_A, DH_A, DH_A), per_b(H_A, 1, DH_A), per_b(H_A, 1, LANE), per_b(CONV_W - 1, W_A),
        ],
        out_specs=[
            pl.BlockSpec((seq, W_A), lambda b: (b, 0)),
            per_b(H_A, DH_A, DH_A), per_b(H_A, 1, DH_A), per_b(H_A, 1, LANE), per_b(CONV_W - 1, W_A),
        ],
        out_shape=[
            jax.ShapeDtypeStruct((bsz * seq, W_A), BF16),
            jax.ShapeDtypeStruct((bsz, H_A, DH_A, DH_A), F32),
            jax.ShapeDtypeStruct((bsz, H_A, 1, DH_A), F32),
            jax.ShapeDtypeStruct((bsz, H_A, 1, LANE), F32),
            jax.ShapeDtypeStruct((bsz, CONV_W - 1, W_A), F32),
        ],
        scratch_shapes=[pltpu.VMEM((seq + 8, W_A), F32)],
        compiler_params=_params(("parallel",)),
        name="mlstm",
    )(a, g, cw, cb, wq, wk, gain, c0, n0, m0, buf)


def _diff_kernel(lam_ref, q_ref, k_ref, v_ref, near_ref, gain_ref, o_ref, kb_ref, vb_ref, s_ref, mx_ref,
                 *, tq, lq, lk, q_pos0, chunked, lam_init):
    qi = pl.program_id(2)
    nq = pl.num_programs(2)
    rows_k = kb_ref.shape[0]

    @pl.when(qi == 0)
    def _():
        kb_ref[0:lk, :] = k_ref[0, 0].astype(BF16)
        vb_ref[0:lk, :] = v_ref[0, 0].astype(BF16)
        kb_ref[lk:, :] = jnp.zeros((rows_k - lk, 2 * DH_B), BF16)
        vb_ref[lk:, :] = jnp.zeros((rows_k - lk, 2 * DH_B), BF16)

    q0 = q_pos0 + qi * tq
    qt = q0 >> LOG2_KEY_TILE
    lane = lax.broadcasted_iota(jnp.int32, (1, 2 * DH_B), 1)
    bias_far = near_ref[0, 0, 0:1, 0:1]

    def process(rows):
        r2 = 2 * rows
        qf = q_ref[0, 0:rows, :].astype(F32)
        qs = jnp.concatenate([jnp.where(lane < DH_B, qf, 0.0), jnp.where(lane >= DH_B, qf, 0.0)],
                             axis=0).astype(BF16)
        mx_ref[0:r2, :] = jnp.full((r2, KEY_TILE), NEG_BIG, F32)
        ntile = qt + 2
        for g in range(3):
            s_ref[ntile + g, 0:r2, :] = jnp.full((r2, KEY_TILE), NEG_BIG, F32)

        def step(j0, width, mode):
            ks = pl.multiple_of(j0 * KEY_TILE, KEY_TILE)
            kt = kb_ref[pl.ds(ks, width * KEY_TILE), :]
            s = _dot_nt(qs, kt)
            if mode == "far":
                s = s + bias_far
            elif mode == "prev":
                bias = near_ref[0, 1, 0:rows, :]
                s = s + jnp.concatenate([bias, bias], axis=0)
            else:
                bias = jnp.concatenate([near_ref[0, 2, 0:rows, :], near_ref[0, 3, 0:rows, :]], axis=1)
                kpos = ks + lax.broadcasted_iota(jnp.int32, (rows, 2 * KEY_TILE), 1)
                if chunked:
                    qpos = q0 + lax.broadcasted_iota(jnp.int32, (rows, 2 * KEY_TILE), 0)
                    mask = _chunk_id(kpos) <= _chunk_id(qpos)
                else:
                    mask = kpos < lk
                s = jnp.where(jnp.concatenate([mask, mask], axis=0),
                              s + jnp.concatenate([bias, bias], axis=0), NEG_BIG)
            mx = mx_ref[0:r2, :]
            for g in range(width):
                blk = s[:, g * KEY_TILE:(g + 1) * KEY_TILE]
                s_ref[j0 + g, 0:r2, :] = blk
                mx = jnp.maximum(mx, blk)
            mx_ref[0:r2, :] = mx

        step(qt, 2, "near")

        @pl.when(qt >= 1)
        def _():
            step(qt - 1, 1, "prev")

        nfar = jnp.maximum(qt - 1, 0)

        def body(g, carry):
            step(4 * g, 4, "far")
            return carry

        lax.fori_loop(0, nfar >> 2, body, 0)
        base = (nfar >> 2) << 2

        @pl.when((nfar & 2) != 0)
        def _():
            step(base, 2, "far")

        @pl.when((nfar & 1) != 0)
        def _():
            step(base + (nfar & 2), 1, "far")

        m = jnp.max(mx_ref[0:r2, :], axis=1, keepdims=True)

        def group(g, carry):
            lsum, acc = carry
            j0 = 4 * g
            vt = vb_ref[pl.ds(pl.multiple_of(j0 * KEY_TILE, KEY_TILE), 4 * KEY_TILE), :]
            ps = []
            for t in range(4):
                p = jnp.exp2(s_ref[j0 + t, 0:r2, :] - m)
                lsum = lsum + p
                ps.append(p.astype(BF16))
            return lsum, acc + _dot(jnp.concatenate(ps, axis=1), vt)

        zero = jnp.zeros((r2, KEY_TILE), F32)
        lsum, acc = lax.fori_loop(0, (ntile + 3) >> 2, group, (zero, zero))
        o = acc / jnp.sum(lsum, axis=1, keepdims=True)
        lam = lam_ref[:, 0:1]
        o = o[0:rows] - lam * o[rows:2 * rows]
        o = o * lax.rsqrt(jnp.mean(o * o, axis=1, keepdims=True) + LN_EPS) * gain_ref[...] * (1.0 - lam_init)
        o_ref[0, 0:rows, :] = o.astype(BF16)

    tail = lq % tq
    if tail == 0:
        process(tq)
    else:
        @pl.when(qi < nq - 1)
        def _():
            process(tq)

        @pl.when(qi == nq - 1)
        def _():
            process(tail)


def _diff_attention(q, k, v, near, lam, gain, *, tq, q_pos0, chunked, lam_init):
    bsz, lq, _ = q.shape
    lk = k.shape[2]
    nq = -(-lq // tq)
    nkt = -(-lk // KEY_TILE)
    kern = functools.partial(_diff_kernel, tq=tq, lq=lq, lk=lk, q_pos0=q_pos0, chunked=chunked, lam_init=lam_init)
    return pl.pallas_call(
        kern,
        grid=(bsz, H_B, nq),
        in_specs=[
            pl.BlockSpec((1, LANE), lambda b, h, i: (0, 0)),
            pl.BlockSpec((1, tq, 2 * DH_B), lambda b, h, i: (b, i, h)),
            pl.BlockSpec((1, 1, lk, 2 * DH_B), lambda b, h, i: (b, h, 0, 0)),
            pl.BlockSpec((1, 1, lk, 2 * DH_B), lambda b, h, i: (b, h, 0, 0)),
            pl.BlockSpec((1, 4, tq, KEY_TILE), lambda b, h, i: (h, 0, 0, 0)),
            pl.BlockSpec((1, 2 * DH_B), lambda b, h, i: (0, 0)),
        ],
        out_specs=pl.BlockSpec((1, tq, 2 * DH_B), lambda b, h, i: (b, i, h)),
        out_shape=jax.ShapeDtypeStruct((bsz, lq, W_B), BF16),
        scratch_shapes=[
            pltpu.VMEM(((nkt + 4) * KEY_TILE, 2 * DH_B), BF16),
            pltpu.VMEM(((nkt + 4) * KEY_TILE, 2 * DH_B), BF16),
            pltpu.VMEM((nkt + 4, 2 * tq, KEY_TILE), F32),
            pltpu.VMEM((2 * tq, KEY_TILE), F32),
        ],
        compiler_params=_params(("parallel", "parallel", "arbitrary")),
        name="diff_attention",
    )(lam, q, k, v, near, gain)


def _inproj_c_kernel(x_ref, w_ref, q_ref, k_ref, v_ref):
    xb = x_ref[...].astype(BF16)
    q_ref[...] = _dot(xb, w_ref[:, 0:D_MODEL]).astype(BF16)
    yk = _dot(xb, w_ref[:, D_MODEL:2 * D_MODEL])
    for h in range(H_C):
        k_ref[0, h] = yk[:, DH_C * h:DH_C * (h + 1)]
    yv = _dot(xb, w_ref[:, 2 * D_MODEL:3 * D_MODEL])
    for h in range(H_C):
        v_ref[0, h] = yv[:, DH_C * h:DH_C * (h + 1)]


def _inproj_c(x_all, row0, bsz, seq, tm, w):
    nrt = seq // tm
    if row0 % tm:
        x_all, row0 = x_all[row0:row0 + bsz * seq], 0
    off = row0 // tm
    return pl.pallas_call(
        _inproj_c_kernel,
        grid=(bsz, nrt),
        in_specs=[
            pl.BlockSpec((tm, D_MODEL), lambda b, r: (off + b * nrt + r, 0)),
            pl.BlockSpec((D_MODEL, 3 * D_MODEL), lambda b, r: (0, 0)),
        ],
        out_specs=[
            pl.BlockSpec((tm, D_MODEL), lambda b, r: (b * nrt + r, 0)),
            pl.BlockSpec((1, H_C, tm, DH_C), lambda b, r: (b, 0, r, 0)),
            pl.BlockSpec((1, H_C, tm, DH_C), lambda b, r: (b, 0, r, 0)),
        ],
        out_shape=[
            jax.ShapeDtypeStruct((bsz * seq, D_MODEL), BF16),
            jax.ShapeDtypeStruct((bsz, H_C, seq, DH_C), F32),
            jax.ShapeDtypeStruct((bsz, H_C, seq, DH_C), F32),
        ],
        compiler_params=_params(("parallel", "parallel")),
        name="inproj_c",
    )(x_all, w)


def _sb_kernel(q_ref, k_ref, v_ref, o_ref, kb_ref, vb_ref, acc_ref, run_ref, u_ref, t_ref, hl_ref,
               *, tq, lq, lk, q_pos0):
    qi = pl.program_id(2)
    nq = pl.num_programs(2)
    rows_k = kb_ref.shape[0]

    @pl.when(qi == 0)
    def _():
        kb_ref[0:lk, :] = jnp.concatenate([k_ref[0, 0], k_ref[0, 1]], axis=1).astype(BF16)
        vb_ref[0:lk, :] = jnp.concatenate([v_ref[0, 0], v_ref[0, 1]], axis=1).astype(BF16)
        if rows_k > lk:
            kb_ref[lk:, :] = jnp.zeros((rows_k - lk, 2 * DH_C), BF16)
            vb_ref[lk:, :] = jnp.zeros((rows_k - lk, 2 * DH_C), BF16)

    q0 = q_pos0 + qi * tq
    jd = q0 >> LOG2_KEY_TILE
    jj = lax.broadcasted_iota(jnp.int32, (2 * KEY_TILE, 2 * KEY_TILE), 0) & (KEY_TILE - 1)
    ss = lax.broadcasted_iota(jnp.int32, (2 * KEY_TILE, 2 * KEY_TILE), 1)
    later = jnp.where((jj > ss) | (ss >= KEY_TILE), 1.0, 0.0).astype(BF16)
    lane = lax.broadcasted_iota(jnp.int32, (1, 2 * DH_C), 1)

    def process(rows):
        qf = q_ref[0, 0:rows, :].astype(F32)
        qh = [jnp.where(lane < DH_C, qf, 0.0).astype(BF16), jnp.where(lane >= DH_C, qf, 0.0).astype(BF16)]
        acc_ref[:, 0:rows, :] = jnp.zeros((2, rows, 2 * DH_C), F32)
        run_ref[:, 0:rows, :] = jnp.zeros((2, rows, KEY_TILE), F32)

        def stage_scores(j0, width, masked):
            ks = pl.multiple_of(j0 * KEY_TILE, KEY_TILE)
            kt = kb_ref[pl.ds(ks, width * KEY_TILE), :]
            for h in range(2):
                z = _dot_nt(qh[h], kt)
                sp = jnp.maximum(z, 0.0) + jnp.log(1.0 + jnp.exp2(-jnp.abs(z))) * LOG2_E
                u = z - sp
                if masked:
                    mask = (ks + lax.broadcasted_iota(jnp.int32, (rows, width * KEY_TILE), 1)) < (
                        q0 + lax.broadcasted_iota(jnp.int32, (rows, width * KEY_TILE), 0))
                    sp = jnp.where(mask, sp, 0.0)
                    u = jnp.where(mask, u, NEG_BIG)
                hi = sp.astype(BF16)
                lo = (sp - hi.astype(F32)).astype(BF16)
                for g in range(width):
                    cols = slice(g * KEY_TILE, (g + 1) * KEY_TILE)
                    u_ref[h, j0 + g, 0:rows, :] = u[:, cols]
                    hl_ref[h, j0 + g, 0:rows, :] = jnp.concatenate([hi[:, cols], lo[:, cols]], axis=1)

        def stage_weights(j0, width):
            ks = pl.multiple_of(j0 * KEY_TILE, KEY_TILE)
            vt = vb_ref[pl.ds(ks, width * KEY_TILE), :]
            for h in range(2):
                off = run_ref[h, 0:rows, :]
                parts = [None] * width
                for g in reversed(range(width)):
                    parts[g] = jnp.exp2(u_ref[h, j0 + g, 0:rows, :] - off).astype(BF16)
                    off = off + t_ref[h, j0 + g, 0:rows, :]
                a_all = parts[0] if width == 1 else jnp.concatenate(parts, axis=1)
                acc_ref[h, 0:rows, :] += _dot(a_all, vt)
                run_ref[h, 0:rows, :] = off

        def stage_sums(j0, width):
            for h in range(2):
                hl = hl_ref[h, pl.ds(j0, width), 0:rows, :].reshape(width * rows, 2 * KEY_TILE)
                cs = _dot(hl, later)
                for g in range(width):
                    blk = cs[g * rows:(g + 1) * rows]
                    u_ref[h, j0 + g, 0:rows, :] = u_ref[h, j0 + g, 0:rows, :] - blk[:, 0:KEY_TILE]
                    t_ref[h, j0 + g, 0:rows, :] = blk[:, KEY_TILE:2 * KEY_TILE]

        nfull = jd // SB_GROUP
        dgrp = nfull * SB_GROUP

        def scores_body(g, carry):
            stage_scores(SB_GROUP * g, SB_GROUP, False)
            return carry

        lax.fori_loop(0, nfull, scores_body, 0)

        def sums_body(g, carry):
            stage_sums(SB_GROUP * g, SB_GROUP)
            return carry

        lax.fori_loop(0, nfull, sums_body, 0)
        for w in range(1, SB_GROUP + 1):
            @pl.when(jd - dgrp == w - 1)
            def _():
                stage_scores(dgrp, w, True)
                stage_sums(dgrp, w)
                stage_weights(dgrp, w)

        def weights_body(g, carry):
            stage_weights(SB_GROUP * (nfull - 1 - g), SB_GROUP)
            return carry

        lax.fori_loop(0, nfull, weights_body, 0)

        o_ref[0, 0:rows, :] = jnp.where(lane < DH_C, acc_ref[0, 0:rows, :], acc_ref[1, 0:rows, :]).astype(BF16)

    tail = lq % tq
    if tail == 0:
        process(tq)
    else:
        @pl.when(qi < nq - 1)
        def _():
            process(tq)

        @pl.when(qi == nq - 1)
        def _():
            process(tail)


def _stick_breaking(q, k, v, *, tq, q_pos0):
    bsz, lq, _ = q.shape
    lk = k.shape[2]
    nq = -(-lq // tq)
    nkt = SB_GROUP * (-(-lk // (SB_GROUP * KEY_TILE)))
    kern = functools.partial(_sb_kernel, tq=tq, lq=lq, lk=lk, q_pos0=q_pos0)
    return pl.pallas_call(
        kern,
        grid=(bsz, H_C // 2, nq),
        in_specs=[
            pl.BlockSpec((1, tq, 2 * DH_C), lambda b, h, i: (b, i, h)),
            pl.BlockSpec((1, 2, lk, DH_C), lambda b, h, i: (b, h, 0, 0)),
            pl.BlockSpec((1, 2, lk, DH_C), lambda b, h, i: (b, h, 0, 0)),
        ],
        out_specs=pl.BlockSpec((1, tq, 2 * DH_C), lambda b, h, i: (b, i, h)),
        out_shape=jax.ShapeDtypeStruct((bsz, lq, D_MODEL), BF16),
        scratch_shapes=[
            pltpu.VMEM((nkt * KEY_TILE, 2 * DH_C), BF16),
            pltpu.VMEM((nkt * KEY_TILE, 2 * DH_C), BF16),
            pltpu.VMEM((2, tq, 2 * DH_C), F32),
            pltpu.VMEM((2, tq, KEY_TILE), F32),
            pltpu.VMEM((2, nkt, tq, KEY_TILE), F32),
            pltpu.VMEM((2, nkt, tq, KEY_TILE), F32),
            pltpu.VMEM((2, nkt, tq, 2 * KEY_TILE), BF16),
        ],
        compiler_params=_params(("parallel", "parallel", "arbitrary")),
        name="stick_breaking",
    )(q, k, v)


def _layer_norm(z, g, b):
    mu = jnp.mean(z, axis=1, keepdims=True)
    zc = z - mu
    var = jnp.mean(zc * zc, axis=1, keepdims=True)
    return zc * lax.rsqrt(var + LN_EPS) * g + b


def _outproj_kernel(*refs, n_in):
    mix_refs = refs[:n_in]
    (w_ref, x_ref, g_ref, b_ref, wr_ref, br_ref,
     x1_ref, x1b_ref, ti_ref, tg_ref, rk_ref, cnt_ref, tri_ref) = refs[n_in:]
    i = pl.program_id(0)
    tm = x_ref.shape[0]

    @pl.when(i == 0)
    def _():
        cnt_ref[...] = jnp.zeros_like(cnt_ref)
        tri_ref[...] = (lax.broadcasted_iota(jnp.int32, (tm, tm), 0)
                        < lax.broadcasted_iota(jnp.int32, (tm, tm), 1)).astype(BF16)

    y = None
    c0 = 0
    for r in mix_refs:
        wd = r.shape[1]
        part = _dot(r[...], w_ref[c0:c0 + wd, :])
        y = part if y is None else y + part
        c0 += wd
    x1 = _layer_norm(DN_ALPHA * x_ref[...] + y, g_ref[...], b_ref[...])
    x1_ref[...] = x1
    x1b_ref[...] = x1.astype(BF16)
    logits = _dot_nt(wr_ref[...], x1, precision=lax.Precision.HIGHEST) + br_ref[...]
    ei = lax.broadcasted_iota(jnp.int32, (N_EXPERTS, tm), 0)
    cur = logits
    vals, idxs = [], []
    for _k in range(TOP_K):
        mx = jnp.max(cur, axis=0, keepdims=True)
        ix = jnp.min(jnp.where(cur == mx, ei, N_EXPERTS), axis=0, keepdims=True)
        vals.append(mx)
        idxs.append(ix)
        cur = jnp.where(ei == ix, -jnp.inf, cur)
    es = [jnp.exp(vv - vals[0]) for vv in vals]
    tot = es[0] + es[1] + es[2] + es[3]
    ti_ref[...] = jnp.concatenate(idxs, axis=0)
    tg_ref[...] = jnp.concatenate([e / tot for e in es], axis=0)
    run = cnt_ref[...]
    ranks = []
    for kk in range(TOP_K):
        oh = (ei == idxs[kk])
        before = _dot(oh.astype(BF16), tri_ref[...])
        ranks.append(jnp.sum(jnp.where(oh, before + run, 0.0), axis=0, keepdims=True))
        run = run + jnp.sum(oh.astype(F32), axis=1, keepdims=True)
    rk_ref[...] = jnp.concatenate(ranks, axis=0).astype(jnp.int32)
    cnt_ref[...] = run


def _outproj_ln_router(mixes, w, x_all, g, b, wr_t, br):
    n = x_all.shape[0]
    tm = TOKEN_TILE
    kern = functools.partial(_outproj_kernel, n_in=len(mixes))
    row = lambda wd: pl.BlockSpec((tm, wd), lambda i: (i, 0))
    full = lambda *shape: pl.BlockSpec(shape, lambda i: (0,) * len(shape))
    lanes = lambda rows: pl.BlockSpec((rows, tm), lambda i: (0, i))
    return pl.pallas_call(
        kern,
        grid=(n // tm,),
        in_specs=[row(m.shape[1]) for m in mixes] + [
            full(D_MODEL, D_MODEL), row(D_MODEL), full(1, D_MODEL), full(1, D_MODEL),
            full(N_EXPERTS, D_MODEL), full(N_EXPERTS, 1)],
        out_specs=[row(D_MODEL), row(D_MODEL), lanes(TOP_K), lanes(TOP_K), lanes(TOP_K), full(N_EXPERTS, 1)],
        out_shape=[
            jax.ShapeDtypeStruct((n, D_MODEL), F32),
            jax.ShapeDtypeStruct((n, D_MODEL), BF16),
            jax.ShapeDtypeStruct((TOP_K, n), jnp.int32),
            jax.ShapeDtypeStruct((TOP_K, n), F32),
            jax.ShapeDtypeStruct((TOP_K, n), jnp.int32),
            jax.ShapeDtypeStruct((N_EXPERTS, 1), F32),
        ],
        scratch_shapes=[pltpu.VMEM((tm, tm), BF16)],
        compiler_params=_params(("arbitrary",)),
        name="outproj_ln_router",
    )(*mixes, w, x_all, g, b, wr_t, br)


def _moe_kernel(be_ref, nu_ref, x_ref, wgu_ref, bgu_ref, wdn_ref, bdn_ref, o_ref):
    @pl.when(pl.program_id(0) < nu_ref[0])
    def _():
        h = _dot(x_ref[...], wgu_ref[0]) + bgu_ref[0]
        glu = jnp.minimum(h[:, 0:D_EXPERT], SWIGLU_LIMIT)
        lin = jnp.clip(h[:, D_EXPERT:2 * D_EXPERT], -SWIGLU_LIMIT, SWIGLU_LIMIT)
        act = glu * _sigmoid(SWIGLU_ALPHA * glu) * (lin + 1.0)
        o_ref[...] = (_dot(act.astype(BF16), wdn_ref[0]) + bdn_ref[0]).astype(BF16)


def _moe_experts(blk_e, n_used, xg, wgu, bgu, wdn, bdn):
    nb = blk_e.shape[0]
    grid_spec = pltpu.PrefetchScalarGridSpec(
        num_scalar_prefetch=2,
        grid=(nb,),
        in_specs=[
            pl.BlockSpec((MOE_BLOCK, D_MODEL), lambda i, be, nu: (i, 0)),
            pl.BlockSpec((1, D_MODEL, 2 * D_EXPERT), lambda i, be, nu: (be[i], 0, 0)),
            pl.BlockSpec((1, 1, 2 * D_EXPERT), lambda i, be, nu: (be[i], 0, 0)),
            pl.BlockSpec((1, D_EXPERT, D_MODEL), lambda i, be, nu: (be[i], 0, 0)),
            pl.BlockSpec((1, 1, D_MODEL), lambda i, be, nu: (be[i], 0, 0)),
        ],
        out_specs=pl.BlockSpec((MOE_BLOCK, D_MODEL), lambda i, be, nu: (i, 0)),
    )
    return pl.pallas_call(
        _moe_kernel,
        grid_spec=grid_spec,
        out_shape=jax.ShapeDtypeStruct((nb * MOE_BLOCK, D_MODEL), BF16),
        compiler_params=_params(("arbitrary",)),
        name="moe_experts",
    )(blk_e, n_used, xg, wgu, bgu, wdn, bdn)


def _deinterleave_kernel(w_ref, p_ref, o_ref):
    o_ref[0] = _dot(w_ref[0].astype(BF16), p_ref[...]).astype(BF16)


def _deinterleave_gu(w):
    n, d, f2 = w.shape
    col = jnp.arange(f2, dtype=jnp.int32)
    src = jnp.where(col < f2 // 2, 2 * col, 2 * (col - f2 // 2) + 1)
    perm = (col[:, None] == src[None, :]).astype(BF16)
    rows = 512
    return pl.pallas_call(
        _deinterleave_kernel,
        grid=(n, d // rows),
        in_specs=[
            pl.BlockSpec((1, rows, f2), lambda e, r: (e, r, 0)),
            pl.BlockSpec((f2, f2), lambda e, r: (0, 0)),
        ],
        out_specs=pl.BlockSpec((1, rows, f2), lambda e, r: (e, r, 0)),
        out_shape=jax.ShapeDtypeStruct((n, d, f2), BF16),
        compiler_params=_params(("parallel", "parallel")),
        name="deinterleave_gu",
    )(w, perm)


def _combine_ln_kernel(x_ref, r_ref, tg_ref, g_ref, b_ref, o_ref):
    tg = tg_ref[...]
    f = r_ref[0].astype(F32) * tg[:, 0:1]
    for kk in range(1, TOP_K):
        f = f + r_ref[kk].astype(F32) * tg[:, kk:kk + 1]
    o_ref[...] = _layer_norm(DN_ALPHA * x_ref[...] + f, g_ref[...], b_ref[...])


def _combine_ln(x1, rows, gates, g, b):
    n = x1.shape[0]
    tm = TOKEN_TILE
    row = pl.BlockSpec((tm, D_MODEL), lambda i: (i, 0))
    vec = pl.BlockSpec((1, D_MODEL), lambda i: (0, 0))
    return pl.pallas_call(
        _combine_ln_kernel,
        grid=(n // tm,),
        in_specs=[row, pl.BlockSpec((TOP_K, tm, D_MODEL), lambda i: (0, i, 0)),
                  pl.BlockSpec((tm, TOP_K), lambda i: (i, 0)), vec, vec],
        out_specs=row,
        out_shape=jax.ShapeDtypeStruct((n, D_MODEL), F32),
        compiler_params=_params(("parallel",)),
        name="combine_ln",
    )(x1, rows, gates, g, b)


def _moe_layer(x1, x1b, top_i, top_g, rank, sizes, e0, wgu, bgu, wdn, bdn, g, b):
    n = x1.shape[0]
    nb = -(-(n * TOP_K) // MOE_BLOCK) + N_EXPERTS
    sizes = sizes[:, 0].astype(jnp.int32)
    nblk = (sizes + MOE_BLOCK - 1) // MOE_BLOCK
    blk_end = jnp.cumsum(nblk)
    pad_starts = (blk_end - nblk) * MOE_BLOCK
    n_used = blk_end[-1]
    blk_ids = jnp.minimum(jnp.arange(nb, dtype=jnp.int32), n_used - 1)
    blk_e = jnp.sum((blk_ids[:, None] >= blk_end[None, :]).astype(jnp.int32), axis=1)
    blk_e = jnp.clip(blk_e, 0, N_EXPERTS - 1)
    experts = jnp.arange(N_EXPERTS, dtype=jnp.int32)
    dest = jnp.sum(jnp.where(top_i[:, :, None] == experts, pad_starts, 0), axis=2) + rank
    tok = jnp.broadcast_to(jnp.arange(n, dtype=jnp.int32)[None], (TOP_K, n))
    tok_of_row = jnp.zeros((nb * MOE_BLOCK,), jnp.int32).at[dest.reshape(-1)].set(
        tok.reshape(-1), unique_indices=True, indices_are_sorted=False)
    xg = jnp.take(x1b, tok_of_row, axis=0)
    out = _moe_experts(blk_e + e0, n_used.reshape(1), xg, wgu, bgu, wdn, bdn)
    rows = jnp.take(out, dest.reshape(-1), axis=0).reshape(TOP_K, n, D_MODEL)
    return _combine_ln(x1, rows, jnp.transpose(top_g), g, b)


def _rel_bucket(rel):
    half = N_BUCKETS // 2
    exact = half // 2
    ret = jnp.where(rel > 0, half, 0)
    n = jnp.abs(rel)
    large = exact + (jnp.log(jnp.maximum(n, 1).astype(F32) / exact)
                     / math.log(MAX_DISTANCE / exact) * (half - exact)).astype(jnp.int32)
    large = jnp.minimum(large, half - 1)
    return ret + jnp.where(n < exact, n, large)


def _near_bias_tiles(rel_table, tq, q_pos0):
    base = q_pos0 % KEY_TILE
    i = jnp.arange(tq, dtype=jnp.int32)[:, None] + base
    j = jnp.arange(KEY_TILE, dtype=jnp.int32)[None, :]
    tiles = [jnp.full((tq, KEY_TILE), -2 * MAX_DISTANCE, jnp.int32)]
    for d in (-KEY_TILE, 0, KEY_TILE):
        tiles.append(d + j - i)
    rel = jnp.stack(tiles)
    return jnp.transpose(rel_table[_rel_bucket(rel)], (3, 0, 1, 2)).astype(F32) * LOG2_E


def kernel(x_prompt, x_sample, cache_diff_k, cache_diff_v, state_mlstm_C, state_mlstm_n, state_mlstm_m, state_mlstm_conv, cache_sb_k, cache_sb_v, meta_tokens, rel_bias, w_in_ab, w_out_ab, conv_w_a, conv_b_a, w_aq_a, w_ak_a, b_if_a, mh_gain_a, lam_q1, lam_k1, lam_q2, lam_k2, subln_gain_b, w_in_c, w_out_c, ln_g, ln_b, w_router, b_router, w_gu, b_gu, w_down, b_down):
    bp, sp, _ = x_prompt.shape
    bs, ss, _ = x_sample.shape
    lp = N_META + sp
    past = cache_diff_k.shape[3]
    n_p = bp * lp
    n_s = bs * ss
    tm_p = lp // 3 if (lp % 3 == 0 and (lp // 3) % 16 == 0) else lp
    tq_p = KEY_TILE

    x0 = jnp.concatenate([jnp.broadcast_to(meta_tokens[None], (bp, N_META, D_MODEL)).astype(x_prompt.dtype),
                          x_prompt], axis=1)
    x_all = jnp.concatenate([x0.reshape(n_p, D_MODEL), x_sample.reshape(n_s, D_MODEL)], axis=0)

    j = 0
    lam_init = 0.8 - 0.6 * math.exp(-0.3 * 0)
    w_ab = w_in_ab[j]
    w_perm = jnp.concatenate([
        w_ab[:, 0:OFF_AIF],
        jnp.pad(w_ab[:, OFF_AIF:OFF_BQ], ((0, 0), (0, LANE - 2 * H_A))),
        w_ab[:, OFF_BQ:OFF_BQ + W_B] * (DH_B ** -0.5 * LOG2_E),
        w_ab[:, OFF_BQ + W_B:]], axis=1).astype(BF16)
    bg = jnp.pad(b_if_a[j], (0, LANE - 2 * H_A)).reshape(1, LANE)
    lam = (jnp.exp(jnp.sum(lam_q1[j] * lam_k1[j])) - jnp.exp(jnp.sum(lam_q2[j] * lam_k2[j]))).astype(F32) + lam_init
    lam_v = jnp.broadcast_to(lam.reshape(1, 1), (1, LANE))
    cw = conv_w_a[j]
    cb = conv_b_a[j].reshape(1, W_A)
    wq = w_aq_a[j].astype(BF16)
    wk = w_ak_a[j].astype(BF16)
    gain_a = mh_gain_a[j].reshape(1, W_A)
    gain_b = subln_gain_b[j].reshape(1, 2 * DH_B)

    def ab_group(row0, bsz, seq, tm, lead, state, past_kv, tq, q_pos0, chunked):
        a, g, q, k_new, v_new = _inproj_ab(x_all, row0, bsz, seq, tm, w_perm, bg)
        c0, n0, m0, buf = state
        mix_a, c1, n1, m1, cs = _mlstm(
            a, g, bsz, seq, lead, cw, cb, wq, wk, gain_a,
            c0, n0.reshape(bsz, H_A, 1, DH_A),
            jnp.broadcast_to(m0[:, :, None, None], (bsz, H_A, 1, LANE)), buf)
        if past_kv is None:
            k_all, v_all = k_new, v_new
        else:
            k_all = jnp.concatenate([past_kv[0], k_new], axis=2)
            v_all = jnp.concatenate([past_kv[1], v_new], axis=2)
        near = _near_bias_tiles(rel_bias, tq, q_pos0)
        mix_b = _diff_attention(q.reshape(bsz, seq, W_B), k_all, v_all, near, lam_v, gain_b,
                                tq=tq, q_pos0=q_pos0, chunked=chunked, lam_init=lam_init)
        outs = (k_new, v_new, c1, n1.reshape(bsz, H_A, DH_A), m1[:, :, 0, 0], cs)
        return mix_a, mix_b.reshape(bsz * seq, W_B), outs

    zero_state = (jnp.zeros((bp, H_A, DH_A, DH_A), F32), jnp.zeros((bp, H_A, DH_A), F32),
                  jnp.zeros((bp, H_A), F32), jnp.zeros((bp, CONV_W - 1, W_A), F32))
    mix_a_p, mix_b_p, ab_p = ab_group(0, bp, lp, tm_p, N_META, zero_state, None, tq_p, 0, True)
    s_state = (state_mlstm_C[j], state_mlstm_n[j], state_mlstm_m[j], state_mlstm_conv[j])
    mix_a_s, mix_b_s, ab_s = ab_group(n_p, bs, ss, ss, 0, s_state, (cache_diff_k[j], cache_diff_v[j]),
                                      ss, past, False)
    mix_a = jnp.concatenate([mix_a_p, mix_a_s], axis=0)
    mix_b = jnp.concatenate([mix_b_p, mix_b_s], axis=0)

    n_le = w_gu.shape[0] * N_EXPERTS
    wgu = _deinterleave_gu(w_gu.reshape(n_le, D_MODEL, 2 * D_EXPERT))
    bgu = jnp.concatenate([b_gu[..., 0::2], b_gu[..., 1::2]], axis=-1).reshape(n_le, 1, 2 * D_EXPERT)
    wdn = w_down.reshape(n_le, D_EXPERT, D_MODEL).astype(BF16)
    bdn = b_down.reshape(n_le, 1, D_MODEL)

    def token_stage(layer, mixes, w_out, x_in):
        wr_t = jnp.transpose(w_router[layer])
        br = b_router[layer].reshape(N_EXPERTS, 1)
        x1, x1b, top_i, top_g, rank, sizes = _outproj_ln_router(
            mixes, w_out.astype(BF16), x_in, ln_g[layer, 0].reshape(1, D_MODEL), ln_b[layer, 0].reshape(1, D_MODEL),
            wr_t, br)
        return _moe_layer(x1, x1b, top_i, top_g, rank, sizes, layer * N_EXPERTS, wgu, bgu, wdn, bdn,
                          ln_g[layer, 1].reshape(1, D_MODEL), ln_b[layer, 1].reshape(1, D_MODEL))

    x_all = token_stage(0, [mix_a, mix_b], w_out_ab[j], x_all)

    q_scale = jnp.where(jnp.arange(3 * D_MODEL) < D_MODEL, DH_C ** -0.5 * LOG2_E, 1.0).astype(F32)
    w_c = (w_in_c[j] * q_scale).astype(BF16)

    def c_group(row0, bsz, seq, tm, past_kv, tq, q_pos0):
        q, k_new, v_new = _inproj_c(x_all, row0, bsz, seq, tm, w_c)
        if past_kv is None:
            k_all, v_all = k_new, v_new
        else:
            k_all = jnp.concatenate([past_kv[0], k_new], axis=2)
            v_all = jnp.concatenate([past_kv[1], v_new], axis=2)
        o = _stick_breaking(q.reshape(bsz, seq, D_MODEL), k_all, v_all, tq=tq, q_pos0=q_pos0)
        return o.reshape(bsz * seq, D_MODEL), (k_new, v_new)

    mix_p, c_p = c_group(0, bp, lp, tm_p, None, tq_p, 0)
    mix_s, c_s = c_group(n_p, bs, ss, ss, (cache_sb_k[j], cache_sb_v[j]), ss, past)
    x_all = token_stage(1, [jnp.concatenate([mix_p, mix_s], axis=0)], w_out_c[j], x_all)

    y_prompt = x_all[:n_p].reshape(bp, lp, D_MODEL)[:, N_META:]
    y_sample = x_all[n_p:].reshape(bs, ss, D_MODEL)
    stack = lambda t: t[None]
    return (y_prompt, y_sample,
            stack(ab_p[0]), stack(ab_p[1]), stack(ab_p[2]), stack(ab_p[3]), stack(ab_p[4]), stack(ab_p[5]),
            stack(c_p[0]), stack(c_p[1]),
            stack(ab_s[0]), stack(ab_s[1]), stack(ab_s[2]), stack(ab_s[3]), stack(ab_s[4]), stack(ab_s[5]),
            stack(c_s[0]), stack(c_s[1]))
```

```python
import functools
import math

import jax
import jax.numpy as jnp
from jax import lax
from jax.experimental import pallas as pl
from jax.experimental.pallas import tpu as pltpu

F32 = jnp.float32
BF16 = jnp.bfloat16

D_MODEL = 1024
DEPTH = 2
CHUNK = 64
N_META = 16
H_A = 4
DH_A = 128
W_A = H_A * DH_A
CONV_W = 4
H_B = 4
DH_B = 64
W_B = H_B * 2 * DH_B
H_C = 16
DH_C = 64
N_BUCKETS = 32
MAX_DISTANCE = 128
N_EXPERTS = 32
TOP_K = 4
D_EXPERT = D_MODEL // 2
SWIGLU_LIMIT = 7.0
SWIGLU_ALPHA = 1.702
DN_ALPHA = (2 * DEPTH) ** 0.25
LN_EPS = 1e-5
OFF_AIF = 3 * W_A
OFF_BQ = OFF_AIF + 2 * H_A

LANE = 128
KEY_TILE = 128
LOG2_KEY_TILE = 7
LOG2_CHUNK = 6
VMEM_LIMIT = 56 * 1024 * 1024
MOE_BLOCK = 512
TOKEN_TILE = 512
SB_GROUP = 8
NEG_BIG = -1e30
LOG2_E = 1.4426950408889634

COL_G = 3 * W_A
COL_Q = COL_G + LANE
COL_K = COL_Q + W_B
COL_V = COL_K + W_B
D_IN_AB_PAD = COL_V + W_B


def _dot(a, b):
    return jnp.dot(a, b, preferred_element_type=F32)


def _dot_nt(a, b, precision=None):
    return lax.dot_general(a, b, (((1,), (1,)), ((), ())), preferred_element_type=F32, precision=precision)


def _dot_tn(a, b):
    return lax.dot_general(a, b, (((0,), (0,)), ((), ())), preferred_element_type=F32)


def _log_sigmoid(x):
    return jnp.minimum(x, 0.0) - jnp.log(1.0 + jnp.exp(-jnp.abs(x)))


def _sigmoid(x):
    return 1.0 / (1.0 + jnp.exp(-x))


def _chunk_id(pos):
    return (pos + (CHUNK - N_META)) >> LOG2_CHUNK


def _params(sem):
    return pltpu.CompilerParams(dimension_semantics=sem, vmem_limit_bytes=VMEM_LIMIT)


def _inproj_ab_kernel(x_ref, w_ref, bg_ref, a_ref, g_ref, q_ref, k_ref, v_ref):
    xb = x_ref[...].astype(BF16)
    a_ref[...] = _dot(xb, w_ref[:, 0:COL_G])
    g_ref[...] = _dot(xb, w_ref[:, COL_G:COL_Q]) + bg_ref[...]
    q_ref[...] = _dot(xb, w_ref[:, COL_Q:COL_K]).astype(BF16)
    for h in range(H_B):
        k_ref[0, h] = _dot(xb, w_ref[:, COL_K + 2 * DH_B * h:COL_K + 2 * DH_B * (h + 1)])
        v_ref[0, h] = _dot(xb, w_ref[:, COL_V + 2 * DH_B * h:COL_V + 2 * DH_B * (h + 1)])


def _inproj_ab(x_all, row0, bsz, seq, tm, w, bg):
    nrt = seq // tm
    n = bsz * seq
    if row0 % tm:
        x_all, row0 = x_all[row0:row0 + n], 0
    off = row0 // tm
    return pl.pallas_call(
        _inproj_ab_kernel,
        grid=(bsz, nrt),
        in_specs=[
            pl.BlockSpec((tm, D_MODEL), lambda b, r: (off + b * nrt + r, 0)),
            pl.BlockSpec((D_MODEL, D_IN_AB_PAD), lambda b, r: (0, 0)),
            pl.BlockSpec((1, LANE), lambda b, r: (0, 0)),
        ],
        out_specs=[
            pl.BlockSpec((tm, COL_G), lambda b, r: (b * nrt + r, 0)),
            pl.BlockSpec((tm, LANE), lambda b, r: (b * nrt + r, 0)),
            pl.BlockSpec((tm, W_B), lambda b, r: (b * nrt + r, 0)),
            pl.BlockSpec((1, H_B, tm, 2 * DH_B), lambda b, r: (b, 0, r, 0)),
            pl.BlockSpec((1, H_B, tm, 2 * DH_B), lambda b, r: (b, 0, r, 0)),
        ],
        out_shape=[
            jax.ShapeDtypeStruct((n, COL_G), F32),
            jax.ShapeDtypeStruct((n, LANE), F32),
            jax.ShapeDtypeStruct((n, W_B), BF16),
            jax.ShapeDtypeStruct((bsz, H_B, seq, 2 * DH_B), F32),
            jax.ShapeDtypeStruct((bsz, H_B, seq, 2 * DH_B), F32),
        ],
        compiler_params=_params(("parallel", "parallel")),
        name="inproj_ab",
    )(x_all, w, bg)


def _mlstm_kernel(a_ref, g_ref, cw_ref, cb_ref, wq_ref, wk_ref, gain_ref, c0_ref, n0_ref, m0_ref, buf_ref,
                  out_ref, c1_ref, n1_ref, m1_ref, cs_ref, q_s, num_s, rs_s, u_s, nv_s, *, seq, lead):
    cs_ref[0] = a_ref[seq - (CONV_W - 1):seq, 0:W_A]
    c1_ref[...] = c0_ref[...]
    n1_ref[...] = n0_ref[...]
    m1_ref[...] = m0_ref[...]
    sel = (lax.broadcasted_iota(jnp.int32, (8, LANE), 0) == lax.broadcasted_iota(jnp.int32, (8, LANE), 1)).astype(F32)
    lane = lax.broadcasted_iota(jnp.int32, (1, LANE), 1)

    def local(c, r0, lc, first):
        if first:
            win = jnp.concatenate([jnp.zeros((5, W_A), F32), buf_ref[0], a_ref[0:lc, 0:W_A]], axis=0)
        else:
            win = a_ref[pl.ds(pl.multiple_of(r0 - 8, 8), lc + 8), 0:W_A]
        y = cb_ref[...]
        for j in range(CONV_W):
            y = y + win[5 + j:5 + j + lc, :] * cw_ref[j:j + 1, :]
        ca = y * _sigmoid(y)
        g = g_ref[pl.ds(r0, lc), :]
        g_rows = _dot_nt(sel, g, precision=lax.Precision.HIGHEST)
        ti = lax.broadcasted_iota(jnp.int32, (lc, lc), 0)
        si = lax.broadcasted_iota(jnp.int32, (lc, lc), 1)
        causal = si <= ti
        rs = jnp.zeros((lc, LANE), F32)
        for h in range(H_A):
            hs = slice(DH_A * h, DH_A * (h + 1))
            ig_c = g[:, h:h + 1]
            lf_c = _log_sigmoid(g[:, H_A + h:H_A + h + 1])
            ig_r = g_rows[h:h + 1, :]
            lf_r = _log_sigmoid(g_rows[H_A + h:H_A + h + 1, :])
            b_c = jnp.sum(jnp.where(causal, lf_r, 0.0), axis=1, keepdims=True)
            b_r = jnp.sum(jnp.where(ti <= si, lf_c, 0.0), axis=0, keepdims=True)
            dmat = jnp.where(causal, b_c - b_r + ig_r, -jnp.inf)
            m_loc = jnp.max(dmat, axis=1, keepdims=True)
            w = jnp.exp(dmat - m_loc)
            cab = ca[:, hs].astype(BF16)
            qb = _dot(cab, wq_ref[h]).astype(BF16)
            k = _dot(cab, wk_ref[h]) * (DH_A ** -0.5)
            vb = a_ref[pl.ds(r0, lc), W_A + DH_A * h:W_A + DH_A * (h + 1)].astype(BF16)
            s = _dot_nt(qb, k.astype(BF16)) * w
            q_s[pl.ds(r0, lc), hs] = qb
            num_s[pl.ds(r0, lc), hs] = _dot(s.astype(BF16), vb)
            den_loc = jnp.sum(s, axis=1, keepdims=True)
            rs = jnp.where(lane == h, den_loc, rs)
            rs = jnp.where(lane == H_A + h, m_loc, rs)
            rs = jnp.where(lane == 2 * H_A + h, b_c, rs)
            w_end = jnp.exp(b_c[lc - 1:lc, :] - b_c + ig_c - m_loc[lc - 1:lc, :])
            kw = k * w_end
            u_s[c, h] = _dot_tn(kw.astype(BF16), vb)
            nv_s[c, h] = jnp.sum(kw, axis=0, keepdims=True)
        rs_s[pl.ds(r0, lc), :] = rs

    def carry(c, r0, lc):
        rs = rs_s[pl.ds(r0, lc), :]
        for h in range(H_A):
            hs = slice(DH_A * h, DH_A * (h + 1))
            den_loc = jnp.broadcast_to(rs[:, h:h + 1], (lc, DH_A))
            m_loc = jnp.broadcast_to(rs[:, H_A + h:H_A + h + 1], (lc, DH_A))
            b_c = jnp.broadcast_to(rs[:, 2 * H_A + h:2 * H_A + h + 1], (lc, DH_A))
            m_prev = m1_ref[0, h]
            c_prev = c1_ref[0, h]
            n_prev = n1_ref[0, h]
            inter = b_c + m_prev
            m_t = jnp.maximum(inter, m_loc)
            gg = jnp.exp(inter - m_t)
            sc = jnp.exp(m_loc - m_t)
            qb = q_s[pl.ds(r0, lc), hs]
            num = sc * num_s[pl.ds(r0, lc), hs] + gg * _dot(qb, c_prev.astype(BF16))
            qn = jnp.broadcast_to(jnp.sum(qb.astype(F32) * n_prev, axis=1, keepdims=True), (lc, DH_A))
            den = sc * den_loc + gg * qn
            hh = num / jnp.maximum(jnp.abs(den), jnp.exp(-m_t))
            m_new = m_t[lc - 1:lc, :]
            decay = jnp.exp(inter[lc - 1:lc, :] - m_new)
            grow = sc[lc - 1:lc, :]
            c1_ref[0, h] = decay * c_prev + grow * u_s[c, h]
            n1_ref[0, h] = decay * n_prev + grow * nv_s[c, h]
            m1_ref[0, h] = m_new
            hn = hh * lax.rsqrt(jnp.mean(hh * hh, axis=1, keepdims=True) + LN_EPS) * gain_ref[:, hs]
            oa = a_ref[pl.ds(r0, lc), 2 * W_A + DH_A * h:2 * W_A + DH_A * (h + 1)]
            out_ref[pl.ds(r0, lc), hs] = (hn * _sigmoid(oa)).astype(BF16)

    first_len = lead if lead else CHUNK
    nrest = (seq - first_len) // CHUNK
    start = lambda i: pl.multiple_of(first_len + i * CHUNK, 16)
    local(0, 0, first_len, True)

    def local_body(i, c):
        local(2 * i + 1, start(2 * i), CHUNK, False)
        local(2 * i + 2, start(2 * i + 1), CHUNK, False)
        return c

    lax.fori_loop(0, nrest // 2, local_body, 0)
    if nrest % 2:
        local(nrest, start(nrest - 1), CHUNK, False)
    carry(0, 0, first_len)

    def carry_body(i, c):
        carry(i + 1, start(i), CHUNK)
        return c

    lax.fori_loop(0, nrest, carry_body, 0)


def _mlstm(a, g, bsz, seq, lead, cw, cb, wq, wk, gain, c0, n0, m0, buf):
    kern = functools.partial(_mlstm_kernel, seq=seq, lead=lead)
    assert seq >= CONV_W - 1
    nchunks = 1 + (seq - (lead if lead else CHUNK)) // CHUNK
    full = lambda *shape: pl.BlockSpec(shape, lambda b: (0,) * len(shape))
    per_b = lambda *shape: pl.BlockSpec((1,) + shape, lambda b: (b,) + (0,) * len(shape))
    return pl.pallas_call(
        kern,
        grid=(bsz,),
        in_specs=[
            pl.BlockSpec((seq, COL_G), lambda b: (b, 0)),
            pl.BlockSpec((seq, LANE), lambda b: (b, 0)),
            full(CONV_W, W_A), full(1, W_A), full(H_A, DH_A, DH_A), full(H_A, DH_A, DH_A), full(1, W_A),
            per_b(H_A, DH_A, DH_A), per_b(H_A, 1, DH_A), per_b(H_A, 1, LANE), per_b(CONV_W - 1, W_A),
        ],
        out_specs=[
            pl.BlockSpec((seq, W_A), lambda b: (b, 0)),
            per_b(H_A, DH_A, DH_A), per_b(H_A, 1, DH_A), per_b(H_A, 1, LANE), per_b(CONV_W - 1, W_A),
        ],
        out_shape=[
            jax.ShapeDtypeStruct((bsz * seq, W_A), BF16),
            jax.ShapeDtypeStruct((bsz, H_A, DH_A, DH_A), F32),
            jax.ShapeDtypeStruct((bsz, H_A, 1, DH_A), F32),
            jax.ShapeDtypeStruct((bsz, H_A, 1, LANE), F32),
            jax.ShapeDtypeStruct((bsz, CONV_W - 1, W_A), F32),
        ],
        scratch_shapes=[
            pltpu.VMEM((seq, W_A), BF16),
            pltpu.VMEM((seq, W_A), F32),
            pltpu.VMEM((seq, LANE), F32),
            pltpu.VMEM((nchunks, H_A, DH_A, DH_A), F32),
            pltpu.VMEM((nchunks, H_A, 1, DH_A), F32),
        ],
        compiler_params=_params(("parallel",)),
        name="mlstm",
    )(a, g, cw, cb, wq, wk, gain, c0, n0, m0, buf)


def _diff_kernel(lam_ref, q_ref, k_ref, v_ref, near_ref, gain_ref, o_ref, kb_ref, vb_ref, s_ref, mx_ref, l_ref,
                 acc_ref, *, tq, lq, lk, q_pos0, chunked, lam_init):
    qi = pl.program_id(2)
    nq = pl.num_programs(2)
    rows_k = kb_ref.shape[0]

    @pl.when(qi == 0)
    def _():
        kb_ref[0:lk, :] = k_ref[0, 0].astype(BF16)
        vb_ref[0:lk, :] = v_ref[0, 0].astype(BF16)
        kb_ref[lk:, :] = jnp.zeros((rows_k - lk, 2 * DH_B), BF16)
        vb_ref[lk:, :] = jnp.zeros((rows_k - lk, 2 * DH_B), BF16)

    q0 = q_pos0 + qi * tq
    qt = q0 >> LOG2_KEY_TILE
    lane = lax.broadcasted_iota(jnp.int32, (1, 2 * DH_B), 1)
    bias_far = near_ref[0, 0, 0:1, 0:1]

    def process(rows):
        r2 = 2 * rows
        qf = q_ref[0, 0:rows, :].astype(F32)
        qs = jnp.concatenate([jnp.where(lane < DH_B, qf, 0.0), jnp.where(lane >= DH_B, qf, 0.0)],
                             axis=0).astype(BF16)
        mx_ref[0:r2, :] = jnp.full((r2, KEY_TILE), NEG_BIG, F32)

        def step(j0, width, mode):
            ks = pl.multiple_of(j0 * KEY_TILE, KEY_TILE)
            kt = kb_ref[pl.ds(ks, width * KEY_TILE), :]
            s = _dot_nt(qs, kt)
            if mode == "far":
                s = s + bias_far
            elif mode == "prev":
                bias = near_ref[0, 1, 0:rows, :]
                s = s + jnp.concatenate([bias, bias], axis=0)
            else:
                bias = jnp.concatenate([near_ref[0, 2, 0:rows, :], near_ref[0, 3, 0:rows, :]], axis=1)
                kpos = ks + lax.broadcasted_iota(jnp.int32, (rows, 2 * KEY_TILE), 1)
                if chunked:
                    qpos = q0 + lax.broadcasted_iota(jnp.int32, (rows, 2 * KEY_TILE), 0)
                    mask = _chunk_id(kpos) <= _chunk_id(qpos)
                else:
                    mask = kpos < lk
                s = jnp.where(jnp.concatenate([mask, mask], axis=0),
                              s + jnp.concatenate([bias, bias], axis=0), NEG_BIG)
            mx = mx_ref[0:r2, :]
            for g in range(width):
                blk = s[:, g * KEY_TILE:(g + 1) * KEY_TILE]
                s_ref[j0 + g, 0:r2, :] = blk
                mx = jnp.maximum(mx, blk)
            mx_ref[0:r2, :] = mx

        step(qt, 2, "near")

        @pl.when(qt >= 1)
        def _():
            step(qt - 1, 1, "prev")

        def for_tiles(count, fn):
            def body(g, carry):
                fn(8 * g, 8)
                return carry

            lax.fori_loop(0, count >> 3, body, 0)
            base = (count >> 3) << 3
            for w, done in ((4, 0), (2, 4), (1, 6)):
                @pl.when((count & w) != 0)
                def _():
                    fn(base + (count & done), w)

        for_tiles(jnp.maximum(qt - 1, 0), lambda j0, w: step(j0, w, "far"))
        m = jnp.max(mx_ref[0:r2, :], axis=1, keepdims=True)
        l_ref[0:r2, :] = jnp.zeros((r2, KEY_TILE), F32)
        acc_ref[0:r2, :] = jnp.zeros((r2, 2 * DH_B), F32)

        def weights(j0, width):
            vt = vb_ref[pl.ds(pl.multiple_of(j0 * KEY_TILE, KEY_TILE), width * KEY_TILE), :]
            lsum = l_ref[0:r2, :]
            ps = []
            for t in range(width):
                p = jnp.exp2(s_ref[j0 + t, 0:r2, :] - m)
                lsum = lsum + p
                ps.append(p.astype(BF16))
            l_ref[0:r2, :] = lsum
            acc_ref[0:r2, :] += _dot(ps[0] if width == 1 else jnp.concatenate(ps, axis=1), vt)

        for_tiles(qt + 2, weights)
        o = acc_ref[0:r2, :] / jnp.sum(l_ref[0:r2, :], axis=1, keepdims=True)
        lam = lam_ref[:, 0:1]
        o = o[0:rows] - lam * o[rows:2 * rows]
        o = o * lax.rsqrt(jnp.mean(o * o, axis=1, keepdims=True) + LN_EPS) * gain_ref[...] * (1.0 - lam_init)
        o_ref[0, 0:rows, :] = o.astype(BF16)

    tail = lq % tq
    if tail == 0:
        process(tq)
    else:
        @pl.when(qi < nq - 1)
        def _():
            process(tq)

        @pl.when(qi == nq - 1)
        def _():
            process(tail)


def _diff_attention(q, k, v, near, lam, gain, *, tq, q_pos0, chunked, lam_init):
    bsz, lq, _ = q.shape
    lk = k.shape[2]
    nq = -(-lq // tq)
    nkt = -(-lk // KEY_TILE)
    kern = functools.partial(_diff_kernel, tq=tq, lq=lq, lk=lk, q_pos0=q_pos0, chunked=chunked, lam_init=lam_init)
    return pl.pallas_call(
        kern,
        grid=(bsz, H_B, nq),
        in_specs=[
            pl.BlockSpec((1, LANE), lambda b, h, i: (0, 0)),
            pl.BlockSpec((1, tq, 2 * DH_B), lambda b, h, i: (b, i, h)),
            pl.BlockSpec((1, 1, lk, 2 * DH_B), lambda b, h, i: (b, h, 0, 0)),
            pl.BlockSpec((1, 1, lk, 2 * DH_B), lambda b, h, i: (b, h, 0, 0)),
            pl.BlockSpec((1, 4, tq, KEY_TILE), lambda b, h, i: (h, 0, 0, 0)),
            pl.BlockSpec((1, 2 * DH_B), lambda b, h, i: (0, 0)),
        ],
        out_specs=pl.BlockSpec((1, tq, 2 * DH_B), lambda b, h, i: (b, i, h)),
        out_shape=jax.ShapeDtypeStruct((bsz, lq, W_B), BF16),
        scratch_shapes=[
            pltpu.VMEM(((nkt + 1) * KEY_TILE, 2 * DH_B), BF16),
            pltpu.VMEM(((nkt + 1) * KEY_TILE, 2 * DH_B), BF16),
            pltpu.VMEM((nkt + 1, 2 * tq, KEY_TILE), F32),
            pltpu.VMEM((2 * tq, KEY_TILE), F32),
            pltpu.VMEM((2 * tq, KEY_TILE), F32),
            pltpu.VMEM((2 * tq, 2 * DH_B), F32),
        ],
        compiler_params=_params(("parallel", "parallel", "arbitrary")),
        name="diff_attention",
    )(lam, q, k, v, near, gain)


def _inproj_c_kernel(x_ref, w_ref, q_ref, k_ref, v_ref):
    xb = x_ref[...].astype(BF16)
    q_ref[...] = _dot(xb, w_ref[:, 0:D_MODEL]).astype(BF16)
    yk = _dot(xb, w_ref[:, D_MODEL:2 * D_MODEL])
    for h in range(H_C):
        k_ref[0, h] = yk[:, DH_C * h:DH_C * (h + 1)]
    yv = _dot(xb, w_ref[:, 2 * D_MODEL:3 * D_MODEL])
    for h in range(H_C):
        v_ref[0, h] = yv[:, DH_C * h:DH_C * (h + 1)]


def _inproj_c(x_all, row0, bsz, seq, tm, w):
    nrt = seq // tm
    if row0 % tm:
        x_all, row0 = x_all[row0:row0 + bsz * seq], 0
    off = row0 // tm
    return pl.pallas_call(
        _inproj_c_kernel,
        grid=(bsz, nrt),
        in_specs=[
            pl.BlockSpec((tm, D_MODEL), lambda b, r: (off + b * nrt + r, 0)),
            pl.BlockSpec((D_MODEL, 3 * D_MODEL), lambda b, r: (0, 0)),
        ],
        out_specs=[
            pl.BlockSpec((tm, D_MODEL), lambda b, r: (b * nrt + r, 0)),
            pl.BlockSpec((1, H_C, tm, DH_C), lambda b, r: (b, 0, r, 0)),
            pl.BlockSpec((1, H_C, tm, DH_C), lambda b, r: (b, 0, r, 0)),
        ],
        out_shape=[
            jax.ShapeDtypeStruct((bsz * seq, D_MODEL), BF16),
            jax.ShapeDtypeStruct((bsz, H_C, seq, DH_C), F32),
            jax.ShapeDtypeStruct((bsz, H_C, seq, DH_C), F32),
        ],
        compiler_params=_params(("parallel", "parallel")),
        name="inproj_c",
    )(x_all, w)


def _sb_kernel(q_ref, k_ref, v_ref, o_ref, kb_ref, vb_ref, acc_ref, run_ref, u_ref, t_ref, hl_ref,
               *, tq, lq, lk, q_pos0):
    qi = pl.program_id(2)
    nq = pl.num_programs(2)
    rows_k = kb_ref.shape[0]

    @pl.when(qi == 0)
    def _():
        kb_ref[0:lk, :] = jnp.concatenate([k_ref[0, 0], k_ref[0, 1]], axis=1).astype(BF16)
        vb_ref[0:lk, :] = jnp.concatenate([v_ref[0, 0], v_ref[0, 1]], axis=1).astype(BF16)
        if rows_k > lk:
            kb_ref[lk:, :] = jnp.zeros((rows_k - lk, 2 * DH_C), BF16)
            vb_ref[lk:, :] = jnp.zeros((rows_k - lk, 2 * DH_C), BF16)

    q0 = q_pos0 + qi * tq
    jd = q0 >> LOG2_KEY_TILE
    jj = lax.broadcasted_iota(jnp.int32, (2 * KEY_TILE, 2 * KEY_TILE), 0) & (KEY_TILE - 1)
    ss = lax.broadcasted_iota(jnp.int32, (2 * KEY_TILE, 2 * KEY_TILE), 1)
    later = jnp.where((jj > ss) | (ss >= KEY_TILE), 1.0, 0.0).astype(BF16)
    lane = lax.broadcasted_iota(jnp.int32, (1, 2 * DH_C), 1)

    def process(rows):
        qf = q_ref[0, 0:rows, :].astype(F32)
        qh = [jnp.where(lane < DH_C, qf, 0.0).astype(BF16), jnp.where(lane >= DH_C, qf, 0.0).astype(BF16)]
        acc_ref[:, 0:rows, :] = jnp.zeros((2, rows, 2 * DH_C), F32)
        run_ref[:, 0:rows, :] = jnp.zeros((2, rows, KEY_TILE), F32)

        def stage_scores(j0, width, masked):
            ks = pl.multiple_of(j0 * KEY_TILE, KEY_TILE)
            kt = kb_ref[pl.ds(ks, width * KEY_TILE), :]
            for h in range(2):
                z = _dot_nt(qh[h], kt)
                sp = jnp.maximum(z, 0.0) + jnp.log(1.0 + jnp.exp2(-jnp.abs(z))) * LOG2_E
                u = z - sp
                if masked:
                    mask = (ks + lax.broadcasted_iota(jnp.int32, (rows, width * KEY_TILE), 1)) < (
                        q0 + lax.broadcasted_iota(jnp.int32, (rows, width * KEY_TILE), 0))
                    sp = jnp.where(mask, sp, 0.0)
                    u = jnp.where(mask, u, NEG_BIG)
                hi = sp.astype(BF16)
                lo = (sp - hi.astype(F32)).astype(BF16)
                for g in range(width):
                    cols = slice(g * KEY_TILE, (g + 1) * KEY_TILE)
                    u_ref[h, j0 + g, 0:rows, :] = u[:, cols]
                    hl_ref[h, j0 + g, 0:rows, :] = jnp.concatenate([hi[:, cols], lo[:, cols]], axis=1)

        def stage_weights(j0, width):
            ks = pl.multiple_of(j0 * KEY_TILE, KEY_TILE)
            vt = vb_ref[pl.ds(ks, width * KEY_TILE), :]
            for h in range(2):
                off = run_ref[h, 0:rows, :]
                parts = [None] * width
                for g in reversed(range(width)):
                    parts[g] = jnp.exp2(u_ref[h, j0 + g, 0:rows, :] - off).astype(BF16)
                    off = off + t_ref[h, j0 + g, 0:rows, :]
                a_all = parts[0] if width == 1 else jnp.concatenate(parts, axis=1)
                acc_ref[h, 0:rows, :] += _dot(a_all, vt)
                run_ref[h, 0:rows, :] = off

        def stage_sums(j0, width):
            for h in range(2):
                hl = hl_ref[h, pl.ds(j0, width), 0:rows, :].reshape(width * rows, 2 * KEY_TILE)
                cs = _dot(hl, later)
                for g in range(width):
                    blk = cs[g * rows:(g + 1) * rows]
                    u_ref[h, j0 + g, 0:rows, :] = u_ref[h, j0 + g, 0:rows, :] - blk[:, 0:KEY_TILE]
                    t_ref[h, j0 + g, 0:rows, :] = blk[:, KEY_TILE:2 * KEY_TILE]

        nfull = jd // SB_GROUP
        dgrp = nfull * SB_GROUP

        def scores_body(g, carry):
            stage_scores(SB_GROUP * g, SB_GROUP, False)
            return carry

        lax.fori_loop(0, nfull, scores_body, 0)

        def sums_body(g, carry):
            stage_sums(SB_GROUP * g, SB_GROUP)
            return carry

        lax.fori_loop(0, nfull, sums_body, 0)
        for w in range(1, SB_GROUP + 1):
            @pl.when(jd - dgrp == w - 1)
            def _():
                stage_scores(dgrp, w, True)
                stage_sums(dgrp, w)
                stage_weights(dgrp, w)

        def weights_body(g, carry):
            stage_weights(SB_GROUP * (nfull - 1 - g), SB_GROUP)
            return carry

        lax.fori_loop(0, nfull, weights_body, 0)

        o_ref[0, 0:rows, :] = jnp.where(lane < DH_C, acc_ref[0, 0:rows, :], acc_ref[1, 0:rows, :]).astype(BF16)

    tail = lq % tq
    if tail == 0:
        process(tq)
    else:
        @pl.when(qi < nq - 1)
        def _():
            process(tq)

        @pl.when(qi == nq - 1)
        def _():
            process(tail)


def _stick_breaking(q, k, v, *, tq, q_pos0):
    bsz, lq, _ = q.shape
    lk = k.shape[2]
    nq = -(-lq // tq)
    nkt = SB_GROUP * (-(-lk // (SB_GROUP * KEY_TILE)))
    kern = functools.partial(_sb_kernel, tq=tq, lq=lq, lk=lk, q_pos0=q_pos0)
    return pl.pallas_call(
        kern,
        grid=(bsz, H_C // 2, nq),
        in_specs=[
            pl.BlockSpec((1, tq, 2 * DH_C), lambda b, h, i: (b, i, h)),
            pl.BlockSpec((1, 2, lk, DH_C), lambda b, h, i: (b, h, 0, 0)),
            pl.BlockSpec((1, 2, lk, DH_C), lambda b, h, i: (b, h, 0, 0)),
        ],
        out_specs=pl.BlockSpec((1, tq, 2 * DH_C), lambda b, h, i: (b, i, h)),
        out_shape=jax.ShapeDtypeStruct((bsz, lq, D_MODEL), BF16),
        scratch_shapes=[
            pltpu.VMEM((nkt * KEY_TILE, 2 * DH_C), BF16),
            pltpu.VMEM((nkt * KEY_TILE, 2 * DH_C), BF16),
            pltpu.VMEM((2, tq, 2 * DH_C), F32),
            pltpu.VMEM((2, tq, KEY_TILE), F32),
            pltpu.VMEM((2, nkt, tq, KEY_TILE), F32),
            pltpu.VMEM((2, nkt, tq, KEY_TILE), F32),
            pltpu.VMEM((2, nkt, tq, 2 * KEY_TILE), BF16),
        ],
        compiler_params=_params(("parallel", "parallel", "arbitrary")),
        name="stick_breaking",
    )(q, k, v)


def _layer_norm(z, g, b):
    mu = jnp.mean(z, axis=1, keepdims=True)
    zc = z - mu
    var = jnp.mean(zc * zc, axis=1, keepdims=True)
    return zc * lax.rsqrt(var + LN_EPS) * g + b


def _outproj_kernel(*refs, n_in):
    mix_refs = refs[:n_in]
    (w_ref, x_ref, g_ref, b_ref, wr_ref, br_ref,
     x1_ref, x1b_ref, ti_ref, tg_ref, rk_ref, cnt_ref, tri_ref) = refs[n_in:]
    i = pl.program_id(0)
    tm = x_ref.shape[0]

    @pl.when(i == 0)
    def _():
        cnt_ref[...] = jnp.zeros_like(cnt_ref)
        tri_ref[...] = (lax.broadcasted_iota(jnp.int32, (tm, tm), 0)
                        < lax.broadcasted_iota(jnp.int32, (tm, tm), 1)).astype(BF16)

    y = None
    c0 = 0
    for r in mix_refs:
        wd = r.shape[1]
        part = _dot(r[...], w_ref[c0:c0 + wd, :])
        y = part if y is None else y + part
        c0 += wd
    x1 = _layer_norm(DN_ALPHA * x_ref[...] + y, g_ref[...], b_ref[...])
    x1_ref[...] = x1
    x1b_ref[...] = x1.astype(BF16)
    logits = _dot_nt(wr_ref[...], x1, precision=lax.Precision.HIGHEST) + br_ref[...]
    ei = lax.broadcasted_iota(jnp.int32, (N_EXPERTS, tm), 0)
    cur = logits
    vals, idxs = [], []
    for _k in range(TOP_K):
        mx = jnp.max(cur, axis=0, keepdims=True)
        ix = jnp.min(jnp.where(cur == mx, ei, N_EXPERTS), axis=0, keepdims=True)
        vals.append(mx)
        idxs.append(ix)
        cur = jnp.where(ei == ix, -jnp.inf, cur)
    es = [jnp.exp(vv - vals[0]) for vv in vals]
    tot = es[0] + es[1] + es[2] + es[3]
    ti_ref[...] = jnp.concatenate(idxs, axis=0)
    tg_ref[...] = jnp.concatenate([e / tot for e in es], axis=0)
    run = cnt_ref[...]
    ranks = []
    for kk in range(TOP_K):
        oh = (ei == idxs[kk])
        before = _dot(oh.astype(BF16), tri_ref[...])
        ranks.append(jnp.sum(jnp.where(oh, before + run, 0.0), axis=0, keepdims=True))
        run = run + jnp.sum(oh.astype(F32), axis=1, keepdims=True)
    rk_ref[...] = jnp.concatenate(ranks, axis=0).astype(jnp.int32)
    cnt_ref[...] = run


def _outproj_ln_router(mixes, w, x_all, g, b, wr_t, br):
    n = x_all.shape[0]
    tm = TOKEN_TILE
    kern = functools.partial(_outproj_kernel, n_in=len(mixes))
    row = lambda wd: pl.BlockSpec((tm, wd), lambda i: (i, 0))
    full = lambda *shape: pl.BlockSpec(shape, lambda i: (0,) * len(shape))
    lanes = lambda rows: pl.BlockSpec((rows, tm), lambda i: (0, i))
    return pl.pallas_call(
        kern,
        grid=(n // tm,),
        in_specs=[row(m.shape[1]) for m in mixes] + [
            full(D_MODEL, D_MODEL), row(D_MODEL), full(1, D_MODEL), full(1, D_MODEL),
            full(N_EXPERTS, D_MODEL), full(N_EXPERTS, 1)],
        out_specs=[row(D_MODEL), row(D_MODEL), lanes(TOP_K), lanes(TOP_K), lanes(TOP_K), full(N_EXPERTS, 1)],
        out_shape=[
            jax.ShapeDtypeStruct((n, D_MODEL), F32),
            jax.ShapeDtypeStruct((n, D_MODEL), BF16),
            jax.ShapeDtypeStruct((TOP_K, n), jnp.int32),
            jax.ShapeDtypeStruct((TOP_K, n), F32),
            jax.ShapeDtypeStruct((TOP_K, n), jnp.int32),
            jax.ShapeDtypeStruct((N_EXPERTS, 1), F32),
        ],
        scratch_shapes=[pltpu.VMEM((tm, tm), BF16)],
        compiler_params=_params(("arbitrary",)),
        name="outproj_ln_router",
    )(*mixes, w, x_all, g, b, wr_t, br)


def _moe_kernel(be_ref, nu_ref, x_ref, wgu_ref, bgu_ref, wdn_ref, bdn_ref, o_ref):
    @pl.when(pl.program_id(0) < nu_ref[0])
    def _():
        h = _dot(x_ref[...], wgu_ref[0]) + bgu_ref[0]
        glu = jnp.minimum(h[:, 0:D_EXPERT], SWIGLU_LIMIT)
        lin = jnp.clip(h[:, D_EXPERT:2 * D_EXPERT], -SWIGLU_LIMIT, SWIGLU_LIMIT)
        act = glu * _sigmoid(SWIGLU_ALPHA * glu) * (lin + 1.0)
        o_ref[...] = (_dot(act.astype(BF16), wdn_ref[0]) + bdn_ref[0]).astype(BF16)


def _moe_experts(blk_e, n_used, xg, wgu, bgu, wdn, bdn):
    nb = blk_e.shape[0]
    grid_spec = pltpu.PrefetchScalarGridSpec(
        num_scalar_prefetch=2,
        grid=(nb,),
        in_specs=[
            pl.BlockSpec((MOE_BLOCK, D_MODEL), lambda i, be, nu: (i, 0)),
            pl.BlockSpec((1, D_MODEL, 2 * D_EXPERT), lambda i, be, nu: (be[i], 0, 0)),
            pl.BlockSpec((1, 1, 2 * D_EXPERT), lambda i, be, nu: (be[i], 0, 0)),
            pl.BlockSpec((1, D_EXPERT, D_MODEL), lambda i, be, nu: (be[i], 0, 0)),
            pl.BlockSpec((1, 1, D_MODEL), lambda i, be, nu: (be[i], 0, 0)),
        ],
        out_specs=pl.BlockSpec((MOE_BLOCK, D_MODEL), lambda i, be, nu: (i, 0)),
    )
    return pl.pallas_call(
        _moe_kernel,
        grid_spec=grid_spec,
        out_shape=jax.ShapeDtypeStruct((nb * MOE_BLOCK, D_MODEL), BF16),
        compiler_params=_params(("arbitrary",)),
        name="moe_experts",
    )(blk_e, n_used, xg, wgu, bgu, wdn, bdn)


def _deinterleave_kernel(w_ref, p_ref, o_ref):
    o_ref[0] = _dot(w_ref[0].astype(BF16), p_ref[...]).astype(BF16)


def _deinterleave_gu(w):
    n, d, f2 = w.shape
    col = jnp.arange(f2, dtype=jnp.int32)
    src = jnp.where(col < f2 // 2, 2 * col, 2 * (col - f2 // 2) + 1)
    perm = (col[:, None] == src[None, :]).astype(BF16)
    rows = 512
    return pl.pallas_call(
        _deinterleave_kernel,
        grid=(n, d // rows),
        in_specs=[
            pl.BlockSpec((1, rows, f2), lambda e, r: (e, r, 0)),
            pl.BlockSpec((f2, f2), lambda e, r: (0, 0)),
        ],
        out_specs=pl.BlockSpec((1, rows, f2), lambda e, r: (e, r, 0)),
        out_shape=jax.ShapeDtypeStruct((n, d, f2), BF16),
        compiler_params=_params(("parallel", "parallel")),
        name="deinterleave_gu",
    )(w, perm)


def _combine_ln_kernel(x_ref, r_ref, tg_ref, g_ref, b_ref, o_ref):
    tg = tg_ref[...]
    f = r_ref[0].astype(F32) * tg[:, 0:1]
    for kk in range(1, TOP_K):
        f = f + r_ref[kk].astype(F32) * tg[:, kk:kk + 1]
    o_ref[...] = _layer_norm(DN_ALPHA * x_ref[...] + f, g_ref[...], b_ref[...])


def _combine_ln(x1, rows, gates, g, b):
    n = x1.shape[0]
    tm = TOKEN_TILE
    row = pl.BlockSpec((tm, D_MODEL), lambda i: (i, 0))
    vec = pl.BlockSpec((1, D_MODEL), lambda i: (0, 0))
    return pl.pallas_call(
        _combine_ln_kernel,
        grid=(n // tm,),
        in_specs=[row, pl.BlockSpec((TOP_K, tm, D_MODEL), lambda i: (0, i, 0)),
                  pl.BlockSpec((tm, TOP_K), lambda i: (i, 0)), vec, vec],
        out_specs=row,
        out_shape=jax.ShapeDtypeStruct((n, D_MODEL), F32),
        compiler_params=_params(("parallel",)),
        name="combine_ln",
    )(x1, rows, gates, g, b)


def _moe_layer(x1, x1b, top_i, top_g, rank, sizes, e0, wgu, bgu, wdn, bdn, g, b):
    n = x1.shape[0]
    nb = -(-(n * TOP_K) // MOE_BLOCK) + N_EXPERTS
    sizes = sizes[:, 0].astype(jnp.int32)
    nblk = (sizes + MOE_BLOCK - 1) // MOE_BLOCK
    blk_end = jnp.cumsum(nblk)
    pad_starts = (blk_end - nblk) * MOE_BLOCK
    n_used = blk_end[-1]
    blk_ids = jnp.minimum(jnp.arange(nb, dtype=jnp.int32), n_used - 1)
    blk_e = jnp.sum((blk_ids[:, None] >= blk_end[None, :]).astype(jnp.int32), axis=1)
    blk_e = jnp.clip(blk_e, 0, N_EXPERTS - 1)
    experts = jnp.arange(N_EXPERTS, dtype=jnp.int32)
    dest = jnp.sum(jnp.where(top_i[:, :, None] == experts, pad_starts, 0), axis=2) + rank
    tok = jnp.broadcast_to(jnp.arange(n, dtype=jnp.int32)[None], (TOP_K, n))
    tok_of_row = jnp.zeros((nb * MOE_BLOCK,), jnp.int32).at[dest.reshape(-1)].set(
        tok.reshape(-1), unique_indices=True, indices_are_sorted=False)
    xg = x1b.at[tok_of_row].get(mode="promise_in_bounds")
    out = _moe_experts(blk_e + e0, n_used.reshape(1), xg, wgu, bgu, wdn, bdn)
    rows = out.at[dest.reshape(-1)].get(mode="promise_in_bounds").reshape(TOP_K, n, D_MODEL)
    return _combine_ln(x1, rows, jnp.transpose(top_g), g, b)


def _rel_bucket(rel):
    half = N_BUCKETS // 2
    exact = half // 2
    ret = jnp.where(rel > 0, half, 0)
    n = jnp.abs(rel)
    large = exact + (jnp.log(jnp.maximum(n, 1).astype(F32) / exact)
                     / math.log(MAX_DISTANCE / exact) * (half - exact)).astype(jnp.int32)
    large = jnp.minimum(large, half - 1)
    return ret + jnp.where(n < exact, n, large)


def _near_bias_tiles(rel_table, tq, q_pos0):
    base = q_pos0 % KEY_TILE
    i = jnp.arange(tq, dtype=jnp.int32)[:, None] + base
    j = jnp.arange(KEY_TILE, dtype=jnp.int32)[None, :]
    tiles = [jnp.full((tq, KEY_TILE), -2 * MAX_DISTANCE, jnp.int32)]
    for d in (-KEY_TILE, 0, KEY_TILE):
        tiles.append(d + j - i)
    rel = jnp.stack(tiles)
    return jnp.transpose(rel_table[_rel_bucket(rel)], (3, 0, 1, 2)).astype(F32) * LOG2_E


def kernel(x_prompt, x_sample, cache_diff_k, cache_diff_v, state_mlstm_C, state_mlstm_n, state_mlstm_m, state_mlstm_conv, cache_sb_k, cache_sb_v, meta_tokens, rel_bias, w_in_ab, w_out_ab, conv_w_a, conv_b_a, w_aq_a, w_ak_a, b_if_a, mh_gain_a, lam_q1, lam_k1, lam_q2, lam_k2, subln_gain_b, w_in_c, w_out_c, ln_g, ln_b, w_router, b_router, w_gu, b_gu, w_down, b_down):
    bp, sp, _ = x_prompt.shape
    bs, ss, _ = x_sample.shape
    lp = N_META + sp
    past = cache_diff_k.shape[3]
    n_p = bp * lp
    n_s = bs * ss
    tm_p = lp // 3 if (lp % 3 == 0 and (lp // 3) % 16 == 0) else lp
    tq_p = KEY_TILE

    x0 = jnp.concatenate([jnp.broadcast_to(meta_tokens[None], (bp, N_META, D_MODEL)).astype(x_prompt.dtype),
                          x_prompt], axis=1)
    x_all = jnp.concatenate([x0.reshape(n_p, D_MODEL), x_sample.reshape(n_s, D_MODEL)], axis=0)

    j = 0
    lam_init = 0.8 - 0.6 * math.exp(-0.3 * 0)
    w_ab = w_in_ab[j]
    w_perm = jnp.concatenate([
        w_ab[:, 0:OFF_AIF],
        jnp.pad(w_ab[:, OFF_AIF:OFF_BQ], ((0, 0), (0, LANE - 2 * H_A))),
        w_ab[:, OFF_BQ:OFF_BQ + W_B] * (DH_B ** -0.5 * LOG2_E),
        w_ab[:, OFF_BQ + W_B:]], axis=1).astype(BF16)
    bg = jnp.pad(b_if_a[j], (0, LANE - 2 * H_A)).reshape(1, LANE)
    lam = (jnp.exp(jnp.sum(lam_q1[j] * lam_k1[j])) - jnp.exp(jnp.sum(lam_q2[j] * lam_k2[j]))).astype(F32) + lam_init
    lam_v = jnp.broadcast_to(lam.reshape(1, 1), (1, LANE))
    cw = conv_w_a[j]
    cb = conv_b_a[j].reshape(1, W_A)
    wq = w_aq_a[j].astype(BF16)
    wk = w_ak_a[j].astype(BF16)
    gain_a = mh_gain_a[j].reshape(1, W_A)
    gain_b = subln_gain_b[j].reshape(1, 2 * DH_B)

    def ab_group(row0, bsz, seq, tm, lead, state, past_kv, tq, q_pos0, chunked):
        a, g, q, k_new, v_new = _inproj_ab(x_all, row0, bsz, seq, tm, w_perm, bg)
        c0, n0, m0, buf = state
        mix_a, c1, n1, m1, cs = _mlstm(
            a, g, bsz, seq, lead, cw, cb, wq, wk, gain_a,
            c0, n0.reshape(bsz, H_A, 1, DH_A),
            jnp.broadcast_to(m0[:, :, None, None], (bsz, H_A, 1, LANE)), buf)
        if past_kv is None:
            k_all, v_all = k_new, v_new
        else:
            k_all = jnp.concatenate([past_kv[0], k_new], axis=2)
            v_all = jnp.concatenate([past_kv[1], v_new], axis=2)
        near = _near_bias_tiles(rel_bias, tq, q_pos0)
        mix_b = _diff_attention(q.reshape(bsz, seq, W_B), k_all, v_all, near, lam_v, gain_b,
                                tq=tq, q_pos0=q_pos0, chunked=chunked, lam_init=lam_init)
        outs = (k_new, v_new, c1, n1.reshape(bsz, H_A, DH_A), m1[:, :, 0, 0], cs)
        return mix_a, mix_b.reshape(bsz * seq, W_B), outs

    zero_state = (jnp.zeros((bp, H_A, DH_A, DH_A), F32), jnp.zeros((bp, H_A, DH_A), F32),
                  jnp.zeros((bp, H_A), F32), jnp.zeros((bp, CONV_W - 1, W_A), F32))
    mix_a_p, mix_b_p, ab_p = ab_group(0, bp, lp, tm_p, N_META, zero_state, None, tq_p, 0, True)
    s_state = (state_mlstm_C[j], state_mlstm_n[j], state_mlstm_m[j], state_mlstm_conv[j])
    mix_a_s, mix_b_s, ab_s = ab_group(n_p, bs, ss, ss, 0, s_state, (cache_diff_k[j], cache_diff_v[j]),
                                      ss, past, False)
    mix_a = jnp.concatenate([mix_a_p, mix_a_s], axis=0)
    mix_b = jnp.concatenate([mix_b_p, mix_b_s], axis=0)

    n_le = w_gu.shape[0] * N_EXPERTS
    wgu = _deinterleave_gu(w_gu.reshape(n_le, D_MODEL, 2 * D_EXPERT))
    bgu = jnp.concatenate([b_gu[..., 0::2], b_gu[..., 1::2]], axis=-1).reshape(n_le, 1, 2 * D_EXPERT)
    wdn = w_down.reshape(n_le, D_EXPERT, D_MODEL).astype(BF16)
    bdn = b_down.reshape(n_le, 1, D_MODEL)

    def token_stage(layer, mixes, w_out, x_in):
        wr_t = jnp.transpose(w_router[layer])
        br = b_router[layer].reshape(N_EXPERTS, 1)
        x1, x1b, top_i, top_g, rank, sizes = _outproj_ln_router(
            mixes, w_out.astype(BF16), x_in, ln_g[layer, 0].reshape(1, D_MODEL), ln_b[layer, 0].reshape(1, D_MODEL),
            wr_t, br)
        return _moe_layer(x1, x1b, top_i, top_g, rank, sizes, layer * N_EXPERTS, wgu, bgu, wdn, bdn,
                          ln_g[layer, 1].reshape(1, D_MODEL), ln_b[layer, 1].reshape(1, D_MODEL))

    x_all = token_stage(0, [mix_a, mix_b], w_out_ab[j], x_all)

    q_scale = jnp.where(jnp.arange(3 * D_MODEL) < D_MODEL, DH_C ** -0.5 * LOG2_E, 1.0).astype(F32)
    w_c = (w_in_c[j] * q_scale).astype(BF16)

    def c_group(row0, bsz, seq, tm, past_kv, tq, q_pos0):
        q, k_new, v_new = _inproj_c(x_all, row0, bsz, seq, tm, w_c)
        if past_kv is None:
            k_all, v_all = k_new, v_new
        else:
            k_all = jnp.concatenate([past_kv[0], k_new], axis=2)
            v_all = jnp.concatenate([past_kv[1], v_new], axis=2)
        o = _stick_breaking(q.reshape(bsz, seq, D_MODEL), k_all, v_all, tq=tq, q_pos0=q_pos0)
        return o.reshape(bsz * seq, D_MODEL), (k_new, v_new)

    mix_p, c_p = c_group(0, bp, lp, tm_p, None, tq_p, 0)
    mix_s, c_s = c_group(n_p, bs, ss, ss, (cache_sb_k[j], cache_sb_v[j]), ss, past)
    x_all = token_stage(1, [jnp.concatenate([mix_p, mix_s], axis=0)], w_out_c[j], x_all)

    y_prompt = x_all[:n_p].reshape(bp, lp, D_MODEL)[:, N_META:]
    y_sample = x_all[n_p:].reshape(bs, ss, D_MODEL)
    stack = lambda t: t[None]
    return (y_prompt, y_sample,
            stack(ab_p[0]), stack(ab_p[1]), stack(ab_p[2]), stack(ab_p[3]), stack(ab_p[4]), stack(ab_p[5]),
            stack(c_p[0]), stack(c_p[1]),
            stack(ab_s[0]), stack(ab_s[1]), stack(ab_s[2]), stack(ab_s[3]), stack(ab_s[4]), stack(ab_s[5]),
            stack(c_s[0]), stack(c_s[1]))
```

```python
import functools
import math

import jax
import jax.numpy as jnp
from jax import lax
from jax.experimental import pallas as pl
from jax.experimental.pallas import tpu as pltpu

F32 = jnp.float32
BF16 = jnp.bfloat16

D_MODEL = 1024
DEPTH = 2
CHUNK = 64
N_META = 16
H_A = 4
DH_A = 128
W_A = H_A * DH_A
CONV_W = 4
H_B = 4
DH_B = 64
W_B = H_B * 2 * DH_B
H_C = 16
DH_C = 64
N_BUCKETS = 32
MAX_DISTANCE = 128
N_EXPERTS = 32
TOP_K = 4
D_EXPERT = D_MODEL // 2
SWIGLU_LIMIT = 7.0
SWIGLU_ALPHA = 1.702
DN_ALPHA = (2 * DEPTH) ** 0.25
LN_EPS = 1e-5
OFF_AIF = 3 * W_A
OFF_BQ = OFF_AIF + 2 * H_A

LANE = 128
KEY_TILE = 128
LOG2_KEY_TILE = 7
LOG2_CHUNK = 6
VMEM_LIMIT = 56 * 1024 * 1024
MOE_BLOCK = 512
TOKEN_TILE = 512
SB_GROUP = 8
MOE_SPLIT = 2
NEG_BIG = -1e30
LOG2_E = 1.4426950408889634

COL_G = 3 * W_A
COL_Q = COL_G + LANE
COL_K = COL_Q + W_B
COL_V = COL_K + W_B
D_IN_AB_PAD = COL_V + W_B


def _dot(a, b):
    return jnp.dot(a, b, preferred_element_type=F32)


def _dot_nt(a, b, precision=None):
    return lax.dot_general(a, b, (((1,), (1,)), ((), ())), preferred_element_type=F32, precision=precision)


def _dot_tn(a, b):
    return lax.dot_general(a, b, (((0,), (0,)), ((), ())), preferred_element_type=F32)


def _log_sigmoid(x):
    return jnp.minimum(x, 0.0) - jnp.log(1.0 + jnp.exp(-jnp.abs(x)))


def _sigmoid(x):
    return 1.0 / (1.0 + jnp.exp(-x))


def _chunk_id(pos):
    return (pos + (CHUNK - N_META)) >> LOG2_CHUNK


def _params(sem):
    return pltpu.CompilerParams(dimension_semantics=sem, vmem_limit_bytes=VMEM_LIMIT)


def _inproj_ab_kernel(x_ref, w_ref, bg_ref, a_ref, g_ref, q_ref, k_ref, v_ref):
    xb = x_ref[...].astype(BF16)
    a_ref[...] = _dot(xb, w_ref[:, 0:COL_G])
    g_ref[...] = _dot(xb, w_ref[:, COL_G:COL_Q]) + bg_ref[...]
    q_ref[...] = _dot(xb, w_ref[:, COL_Q:COL_K]).astype(BF16)
    for h in range(H_B):
        k_ref[0, h] = _dot(xb, w_ref[:, COL_K + 2 * DH_B * h:COL_K + 2 * DH_B * (h + 1)])
        v_ref[0, h] = _dot(xb, w_ref[:, COL_V + 2 * DH_B * h:COL_V + 2 * DH_B * (h + 1)])


def _inproj_ab(x_all, row0, bsz, seq, tm, w, bg):
    nrt = seq // tm
    n = bsz * seq
    if row0 % tm:
        x_all, row0 = x_all[row0:row0 + n], 0
    off = row0 // tm
    return pl.pallas_call(
        _inproj_ab_kernel,
        grid=(bsz, nrt),
        in_specs=[
            pl.BlockSpec((tm, D_MODEL), lambda b, r: (off + b * nrt + r, 0)),
            pl.BlockSpec((D_MODEL, D_IN_AB_PAD), lambda b, r: (0, 0)),
            pl.BlockSpec((1, LANE), lambda b, r: (0, 0)),
        ],
        out_specs=[
            pl.BlockSpec((tm, COL_G), lambda b, r: (b * nrt + r, 0)),
            pl.BlockSpec((tm, LANE), lambda b, r: (b * nrt + r, 0)),
            pl.BlockSpec((tm, W_B), lambda b, r: (b * nrt + r, 0)),
            pl.BlockSpec((1, H_B, tm, 2 * DH_B), lambda b, r: (b, 0, r, 0)),
            pl.BlockSpec((1, H_B, tm, 2 * DH_B), lambda b, r: (b, 0, r, 0)),
        ],
        out_shape=[
            jax.ShapeDtypeStruct((n, COL_G), F32),
            jax.ShapeDtypeStruct((n, LANE), F32),
            jax.ShapeDtypeStruct((n, W_B), BF16),
            jax.ShapeDtypeStruct((bsz, H_B, seq, 2 * DH_B), F32),
            jax.ShapeDtypeStruct((bsz, H_B, seq, 2 * DH_B), F32),
        ],
        compiler_params=_params(("parallel", "parallel")),
        name="inproj_ab",
    )(x_all, w, bg)


def _mlstm_kernel(a_ref, g_ref, cw_ref, cb_ref, wq_ref, wk_ref, gain_ref, c0_ref, n0_ref, m0_ref, buf_ref,
                  out_ref, c1_ref, n1_ref, m1_ref, cs_ref, q_s, num_s, rs_s, u_s, nv_s, *, seq, lead):
    cs_ref[0] = a_ref[seq - (CONV_W - 1):seq, 0:W_A]
    c1_ref[...] = c0_ref[...]
    n1_ref[...] = n0_ref[...]
    m1_ref[...] = m0_ref[...]
    sel = (lax.broadcasted_iota(jnp.int32, (8, LANE), 0) == lax.broadcasted_iota(jnp.int32, (8, LANE), 1)).astype(F32)
    lane = lax.broadcasted_iota(jnp.int32, (1, LANE), 1)

    def local(c, r0, lc, first):
        if first:
            win = jnp.concatenate([jnp.zeros((5, W_A), F32), buf_ref[0], a_ref[0:lc, 0:W_A]], axis=0)
        else:
            win = a_ref[pl.ds(pl.multiple_of(r0 - 8, 8), lc + 8), 0:W_A]
        y = cb_ref[...]
        for j in range(CONV_W):
            y = y + win[5 + j:5 + j + lc, :] * cw_ref[j:j + 1, :]
        ca = y * _sigmoid(y)
        g = g_ref[pl.ds(r0, lc), :]
        g_rows = _dot_nt(sel, g, precision=lax.Precision.HIGHEST)
        ti = lax.broadcasted_iota(jnp.int32, (lc, lc), 0)
        si = lax.broadcasted_iota(jnp.int32, (lc, lc), 1)
        causal = si <= ti
        rs = jnp.zeros((lc, LANE), F32)
        for h in range(H_A):
            hs = slice(DH_A * h, DH_A * (h + 1))
            ig_c = g[:, h:h + 1]
            lf_c = _log_sigmoid(g[:, H_A + h:H_A + h + 1])
            ig_r = g_rows[h:h + 1, :]
            lf_r = _log_sigmoid(g_rows[H_A + h:H_A + h + 1, :])
            b_c = jnp.sum(jnp.where(causal, lf_r, 0.0), axis=1, keepdims=True)
            b_r = jnp.sum(jnp.where(ti <= si, lf_c, 0.0), axis=0, keepdims=True)
            dmat = jnp.where(causal, b_c - b_r + ig_r, -jnp.inf)
            m_loc = jnp.max(dmat, axis=1, keepdims=True)
            w = jnp.exp(dmat - m_loc)
            cab = ca[:, hs].astype(BF16)
            qb = _dot(cab, wq_ref[h]).astype(BF16)
            k = _dot(cab, wk_ref[h]) * (DH_A ** -0.5)
            vb = a_ref[pl.ds(r0, lc), W_A + DH_A * h:W_A + DH_A * (h + 1)].astype(BF16)
            s = _dot_nt(qb, k.astype(BF16)) * w
            q_s[pl.ds(r0, lc), hs] = qb
            num_s[pl.ds(r0, lc), hs] = _dot(s.astype(BF16), vb)
            den_loc = jnp.sum(s, axis=1, keepdims=True)
            rs = jnp.where(lane == h, den_loc, rs)
            rs = jnp.where(lane == H_A + h, m_loc, rs)
            rs = jnp.where(lane == 2 * H_A + h, b_c, rs)
            w_end = jnp.exp(b_c[lc - 1:lc, :] - b_c + ig_c - m_loc[lc - 1:lc, :])
            kw = k * w_end
            u_s[c, h] = _dot_tn(kw.astype(BF16), vb)
            nv_s[c, h] = jnp.sum(kw, axis=0, keepdims=True)
        rs_s[pl.ds(r0, lc), :] = rs

    def carry(c, r0, lc):
        rs = rs_s[pl.ds(r0, lc), :]
        for h in range(H_A):
            hs = slice(DH_A * h, DH_A * (h + 1))
            den_loc = jnp.broadcast_to(rs[:, h:h + 1], (lc, DH_A))
            m_loc = jnp.broadcast_to(rs[:, H_A + h:H_A + h + 1], (lc, DH_A))
            b_c = jnp.broadcast_to(rs[:, 2 * H_A + h:2 * H_A + h + 1], (lc, DH_A))
            m_prev = m1_ref[0, h]
            c_prev = c1_ref[0, h]
            n_prev = n1_ref[0, h]
            inter = b_c + m_prev
            m_t = jnp.maximum(inter, m_loc)
            gg = jnp.exp(inter - m_t)
            sc = jnp.exp(m_loc - m_t)
            qb = q_s[pl.ds(r0, lc), hs]
            num = sc * num_s[pl.ds(r0, lc), hs] + gg * _dot(qb, c_prev.astype(BF16))
            qn = jnp.broadcast_to(jnp.sum(qb.astype(F32) * n_prev, axis=1, keepdims=True), (lc, DH_A))
            den = sc * den_loc + gg * qn
            hh = num / jnp.maximum(jnp.abs(den), jnp.exp(-m_t))
            m_new = m_t[lc - 1:lc, :]
            decay = jnp.exp(inter[lc - 1:lc, :] - m_new)
            grow = sc[lc - 1:lc, :]
            c1_ref[0, h] = decay * c_prev + grow * u_s[c, h]
            n1_ref[0, h] = decay * n_prev + grow * nv_s[c, h]
            m1_ref[0, h] = m_new
            hn = hh * lax.rsqrt(jnp.mean(hh * hh, axis=1, keepdims=True) + LN_EPS) * gain_ref[:, hs]
            oa = a_ref[pl.ds(r0, lc), 2 * W_A + DH_A * h:2 * W_A + DH_A * (h + 1)]
            out_ref[pl.ds(r0, lc), hs] = (hn * _sigmoid(oa)).astype(BF16)

    first_len = lead if lead else CHUNK
    nrest = (seq - first_len) // CHUNK
    start = lambda i: pl.multiple_of(first_len + i * CHUNK, 16)
    local(0, 0, first_len, True)

    def local_body(i, c):
        local(2 * i + 1, start(2 * i), CHUNK, False)
        local(2 * i + 2, start(2 * i + 1), CHUNK, False)
        return c

    lax.fori_loop(0, nrest // 2, local_body, 0)
    if nrest % 2:
        local(nrest, start(nrest - 1), CHUNK, False)
    carry(0, 0, first_len)

    def carry_body(i, c):
        carry(i + 1, start(i), CHUNK)
        return c

    lax.fori_loop(0, nrest, carry_body, 0)


def _mlstm(a, g, bsz, seq, lead, cw, cb, wq, wk, gain, c0, n0, m0, buf):
    kern = functools.partial(_mlstm_kernel, seq=seq, lead=lead)
    assert seq >= CONV_W - 1
    nchunks = 1 + (seq - (lead if lead else CHUNK)) // CHUNK
    full = lambda *shape: pl.BlockSpec(shape, lambda b: (0,) * len(shape))
    per_b = lambda *shape: pl.BlockSpec((1,) + shape, lambda b: (b,) + (0,) * len(shape))
    return pl.pallas_call(
        kern,
        grid=(bsz,),
        in_specs=[
            pl.BlockSpec((seq, COL_G), lambda b: (b, 0)),
            pl.BlockSpec((seq, LANE), lambda b: (b, 0)),
            full(CONV_W, W_A), full(1, W_A), full(H_A, DH_A, DH_A), full(H_A, DH_A, DH_A), full(1, W_A),
            per_b(H_A, DH_A, DH_A), per_b(H_A, 1, DH_A), per_b(H_A, 1, LANE), per_b(CONV_W - 1, W_A),
        ],
        out_specs=[
            pl.BlockSpec((seq, W_A), lambda b: (b, 0)),
            per_b(H_A, DH_A, DH_A), per_b(H_A, 1, DH_A), per_b(H_A, 1, LANE), per_b(CONV_W - 1, W_A),
        ],
        out_shape=[
            jax.ShapeDtypeStruct((bsz * seq, W_A), BF16),
            jax.ShapeDtypeStruct((bsz, H_A, DH_A, DH_A), F32),
            jax.ShapeDtypeStruct((bsz, H_A, 1, DH_A), F32),
            jax.ShapeDtypeStruct((bsz, H_A, 1, LANE), F32),
            jax.ShapeDtypeStruct((bsz, CONV_W - 1, W_A), F32),
        ],
        scratch_shapes=[
            pltpu.VMEM((seq, W_A), BF16),
            pltpu.VMEM((seq, W_A), F32),
            pltpu.VMEM((seq, LANE), F32),
            pltpu.VMEM((nchunks, H_A, DH_A, DH_A), F32),
            pltpu.VMEM((nchunks, H_A, 1, DH_A), F32),
        ],
        compiler_params=_params(("parallel",)),
        name="mlstm",
    )(a, g, cw, cb, wq, wk, gain, c0, n0, m0, buf)


def _diff_kernel(lam_ref, q_ref, k_ref, v_ref, near_ref, gain_ref, o_ref, kb_ref, vb_ref, s_ref, mx_ref, l_ref,
                 acc_ref, *, tq, lq, lk, q_pos0, chunked, lam_init):
    qi = pl.program_id(2)
    nq = pl.num_programs(2)
    rows_k = kb_ref.shape[0]

    @pl.when(qi == 0)
    def _():
        kb_ref[0:lk, :] = k_ref[0, 0].astype(BF16)
        vb_ref[0:lk, :] = v_ref[0, 0].astype(BF16)
        kb_ref[lk:, :] = jnp.zeros((rows_k - lk, 2 * DH_B), BF16)
        vb_ref[lk:, :] = jnp.zeros((rows_k - lk, 2 * DH_B), BF16)

    q0 = q_pos0 + qi * tq
    qt = q0 >> LOG2_KEY_TILE
    lane = lax.broadcasted_iota(jnp.int32, (1, 2 * DH_B), 1)
    bias_far = near_ref[0, 0, 0:1, 0:1]

    def process(rows):
        r2 = 2 * rows
        qf = q_ref[0, 0:rows, :].astype(F32)
        qs = jnp.concatenate([jnp.where(lane < DH_B, qf, 0.0), jnp.where(lane >= DH_B, qf, 0.0)],
                             axis=0).astype(BF16)
        mx_ref[0:r2, :] = jnp.full((r2, KEY_TILE), NEG_BIG, F32)

        def step(j0, width, mode):
            ks = pl.multiple_of(j0 * KEY_TILE, KEY_TILE)
            kt = kb_ref[pl.ds(ks, width * KEY_TILE), :]
            s = _dot_nt(qs, kt)
            if mode == "far":
                s = s + bias_far
            elif mode == "prev":
                bias = near_ref[0, 1, 0:rows, :]
                s = s + jnp.concatenate([bias, bias], axis=0)
            else:
                bias = jnp.concatenate([near_ref[0, 2, 0:rows, :], near_ref[0, 3, 0:rows, :]], axis=1)
                kpos = ks + lax.broadcasted_iota(jnp.int32, (rows, 2 * KEY_TILE), 1)
                if chunked:
                    qpos = q0 + lax.broadcasted_iota(jnp.int32, (rows, 2 * KEY_TILE), 0)
                    mask = _chunk_id(kpos) <= _chunk_id(qpos)
                else:
                    mask = kpos < lk
                s = jnp.where(jnp.concatenate([mask, mask], axis=0),
                              s + jnp.concatenate([bias, bias], axis=0), NEG_BIG)
            mx = mx_ref[0:r2, :]
            for g in range(width):
                blk = s[:, g * KEY_TILE:(g + 1) * KEY_TILE]
                s_ref[j0 + g, 0:r2, :] = blk
                mx = jnp.maximum(mx, blk)
            mx_ref[0:r2, :] = mx

        step(qt, 2, "near")

        @pl.when(qt >= 1)
        def _():
            step(qt - 1, 1, "prev")

        def for_tiles(count, fn):
            def body(g, carry):
                fn(8 * g, 8)
                return carry

            lax.fori_loop(0, count >> 3, body, 0)
            base = (count >> 3) << 3
            for w, done in ((4, 0), (2, 4), (1, 6)):
                @pl.when((count & w) != 0)
                def _():
                    fn(base + (count & done), w)

        for_tiles(jnp.maximum(qt - 1, 0), lambda j0, w: step(j0, w, "far"))
        m = jnp.max(mx_ref[0:r2, :], axis=1, keepdims=True)
        l_ref[0:r2, :] = jnp.zeros((r2, KEY_TILE), F32)
        acc_ref[0:r2, :] = jnp.zeros((r2, 2 * DH_B), F32)

        def weights(j0, width):
            vt = vb_ref[pl.ds(pl.multiple_of(j0 * KEY_TILE, KEY_TILE), width * KEY_TILE), :]
            lsum = l_ref[0:r2, :]
            ps = []
            for t in range(width):
                p = jnp.exp2(s_ref[j0 + t, 0:r2, :] - m)
                lsum = lsum + p
                ps.append(p.astype(BF16))
            l_ref[0:r2, :] = lsum
            acc_ref[0:r2, :] += _dot(ps[0] if width == 1 else jnp.concatenate(ps, axis=1), vt)

        for_tiles(qt + 2, weights)
        o = acc_ref[0:r2, :] / jnp.sum(l_ref[0:r2, :], axis=1, keepdims=True)
        lam = lam_ref[:, 0:1]
        o = o[0:rows] - lam * o[rows:2 * rows]
        o = o * lax.rsqrt(jnp.mean(o * o, axis=1, keepdims=True) + LN_EPS) * gain_ref[...] * (1.0 - lam_init)
        o_ref[0, 0:rows, :] = o.astype(BF16)

    tail = lq % tq
    if tail == 0:
        process(tq)
    else:
        @pl.when(qi < nq - 1)
        def _():
            process(tq)

        @pl.when(qi == nq - 1)
        def _():
            process(tail)


def _diff_attention(q, k, v, near, lam, gain, *, tq, q_pos0, chunked, lam_init):
    bsz, lq, _ = q.shape
    lk = k.shape[2]
    nq = -(-lq // tq)
    nkt = -(-lk // KEY_TILE)
    kern = functools.partial(_diff_kernel, tq=tq, lq=lq, lk=lk, q_pos0=q_pos0, chunked=chunked, lam_init=lam_init)
    return pl.pallas_call(
        kern,
        grid=(bsz, H_B, nq),
        in_specs=[
            pl.BlockSpec((1, LANE), lambda b, h, i: (0, 0)),
            pl.BlockSpec((1, tq, 2 * DH_B), lambda b, h, i: (b, i, h)),
            pl.BlockSpec((1, 1, lk, 2 * DH_B), lambda b, h, i: (b, h, 0, 0)),
            pl.BlockSpec((1, 1, lk, 2 * DH_B), lambda b, h, i: (b, h, 0, 0)),
            pl.BlockSpec((1, 4, tq, KEY_TILE), lambda b, h, i: (h, 0, 0, 0)),
            pl.BlockSpec((1, 2 * DH_B), lambda b, h, i: (0, 0)),
        ],
        out_specs=pl.BlockSpec((1, tq, 2 * DH_B), lambda b, h, i: (b, i, h)),
        out_shape=jax.ShapeDtypeStruct((bsz, lq, W_B), BF16),
        scratch_shapes=[
            pltpu.VMEM(((nkt + 1) * KEY_TILE, 2 * DH_B), BF16),
            pltpu.VMEM(((nkt + 1) * KEY_TILE, 2 * DH_B), BF16),
            pltpu.VMEM((nkt + 1, 2 * tq, KEY_TILE), F32),
            pltpu.VMEM((2 * tq, KEY_TILE), F32),
            pltpu.VMEM((2 * tq, KEY_TILE), F32),
            pltpu.VMEM((2 * tq, 2 * DH_B), F32),
        ],
        compiler_params=_params(("parallel", "parallel", "arbitrary")),
        name="diff_attention",
    )(lam, q, k, v, near, gain)


def _inproj_c_kernel(x_ref, w_ref, q_ref, k_ref, v_ref):
    xb = x_ref[...].astype(BF16)
    q_ref[...] = _dot(xb, w_ref[:, 0:D_MODEL]).astype(BF16)
    yk = _dot(xb, w_ref[:, D_MODEL:2 * D_MODEL])
    for h in range(H_C):
        k_ref[0, h] = yk[:, DH_C * h:DH_C * (h + 1)]
    yv = _dot(xb, w_ref[:, 2 * D_MODEL:3 * D_MODEL])
    for h in range(H_C):
        v_ref[0, h] = yv[:, DH_C * h:DH_C * (h + 1)]


def _inproj_c(x_all, row0, bsz, seq, tm, w):
    nrt = seq // tm
    if row0 % tm:
        x_all, row0 = x_all[row0:row0 + bsz * seq], 0
    off = row0 // tm
    return pl.pallas_call(
        _inproj_c_kernel,
        grid=(bsz, nrt),
        in_specs=[
            pl.BlockSpec((tm, D_MODEL), lambda b, r: (off + b * nrt + r, 0)),
            pl.BlockSpec((D_MODEL, 3 * D_MODEL), lambda b, r: (0, 0)),
        ],
        out_specs=[
            pl.BlockSpec((tm, D_MODEL), lambda b, r: (b * nrt + r, 0)),
            pl.BlockSpec((1, H_C, tm, DH_C), lambda b, r: (b, 0, r, 0)),
            pl.BlockSpec((1, H_C, tm, DH_C), lambda b, r: (b, 0, r, 0)),
        ],
        out_shape=[
            jax.ShapeDtypeStruct((bsz * seq, D_MODEL), BF16),
            jax.ShapeDtypeStruct((bsz, H_C, seq, DH_C), F32),
            jax.ShapeDtypeStruct((bsz, H_C, seq, DH_C), F32),
        ],
        compiler_params=_params(("parallel", "parallel")),
        name="inproj_c",
    )(x_all, w)


def _sb_kernel(q_ref, k_ref, v_ref, o_ref, kb_ref, vb_ref, acc_ref, run_ref, u_ref, t_ref, hl_ref,
               *, tq, lq, lk, q_pos0):
    qi = pl.program_id(2)
    nq = pl.num_programs(2)
    rows_k = kb_ref.shape[0]

    @pl.when(qi == 0)
    def _():
        kb_ref[0:lk, :] = jnp.concatenate([k_ref[0, 0], k_ref[0, 1]], axis=1).astype(BF16)
        vb_ref[0:lk, :] = jnp.concatenate([v_ref[0, 0], v_ref[0, 1]], axis=1).astype(BF16)
        if rows_k > lk:
            kb_ref[lk:, :] = jnp.zeros((rows_k - lk, 2 * DH_C), BF16)
            vb_ref[lk:, :] = jnp.zeros((rows_k - lk, 2 * DH_C), BF16)

    q0 = q_pos0 + qi * tq
    jd = q0 >> LOG2_KEY_TILE
    jj = lax.broadcasted_iota(jnp.int32, (2 * KEY_TILE, 2 * KEY_TILE), 0) & (KEY_TILE - 1)
    ss = lax.broadcasted_iota(jnp.int32, (2 * KEY_TILE, 2 * KEY_TILE), 1)
    later = jnp.where((jj > ss) | (ss >= KEY_TILE), 1.0, 0.0).astype(BF16)
    lane = lax.broadcasted_iota(jnp.int32, (1, 2 * DH_C), 1)

    def process(rows):
        qf = q_ref[0, 0:rows, :].astype(F32)
        qh = [jnp.where(lane < DH_C, qf, 0.0).astype(BF16), jnp.where(lane >= DH_C, qf, 0.0).astype(BF16)]
        acc_ref[:, 0:rows, :] = jnp.zeros((2, rows, 2 * DH_C), F32)
        run_ref[:, 0:rows, :] = jnp.zeros((2, rows, KEY_TILE), F32)

        def stage_scores(j0, width, masked):
            ks = pl.multiple_of(j0 * KEY_TILE, KEY_TILE)
            kt = kb_ref[pl.ds(ks, width * KEY_TILE), :]
            for h in range(2):
                z = _dot_nt(qh[h], kt)
                sp = jnp.maximum(z, 0.0) + jnp.log(1.0 + jnp.exp2(-jnp.abs(z))) * LOG2_E
                u = z - sp
                if masked:
                    mask = (ks + lax.broadcasted_iota(jnp.int32, (rows, width * KEY_TILE), 1)) < (
                        q0 + lax.broadcasted_iota(jnp.int32, (rows, width * KEY_TILE), 0))
                    sp = jnp.where(mask, sp, 0.0)
                    u = jnp.where(mask, u, NEG_BIG)
                hi = sp.astype(BF16)
                lo = (sp - hi.astype(F32)).astype(BF16)
                for g in range(width):
                    cols = slice(g * KEY_TILE, (g + 1) * KEY_TILE)
                    u_ref[h, j0 + g, 0:rows, :] = u[:, cols]
                    hl_ref[h, j0 + g, 0:rows, :] = jnp.concatenate([hi[:, cols], lo[:, cols]], axis=1)

        def stage_weights(j0, width):
            ks = pl.multiple_of(j0 * KEY_TILE, KEY_TILE)
            vt = vb_ref[pl.ds(ks, width * KEY_TILE), :]
            for h in range(2):
                off = run_ref[h, 0:rows, :]
                parts = [None] * width
                for g in reversed(range(width)):
                    parts[g] = jnp.exp2(u_ref[h, j0 + g, 0:rows, :] - off).astype(BF16)
                    off = off + t_ref[h, j0 + g, 0:rows, :]
                a_all = parts[0] if width == 1 else jnp.concatenate(parts, axis=1)
                acc_ref[h, 0:rows, :] += _dot(a_all, vt)
                run_ref[h, 0:rows, :] = off

        def stage_sums(j0, width):
            for h in range(2):
                hl = hl_ref[h, pl.ds(j0, width), 0:rows, :].reshape(width * rows, 2 * KEY_TILE)
                cs = _dot(hl, later)
                for g in range(width):
                    blk = cs[g * rows:(g + 1) * rows]
                    u_ref[h, j0 + g, 0:rows, :] = u_ref[h, j0 + g, 0:rows, :] - blk[:, 0:KEY_TILE]
                    t_ref[h, j0 + g, 0:rows, :] = blk[:, KEY_TILE:2 * KEY_TILE]

        nfull = jd // SB_GROUP
        dgrp = nfull * SB_GROUP

        def scores_body(g, carry):
            stage_scores(SB_GROUP * g, SB_GROUP, False)
            return carry

        lax.fori_loop(0, nfull, scores_body, 0)

        def sums_body(g, carry):
            stage_sums(SB_GROUP * g, SB_GROUP)
            return carry

        lax.fori_loop(0, nfull, sums_body, 0)
        jlast = (q0 + rows - 1) >> LOG2_KEY_TILE
        for w in range(1, SB_GROUP + 1):
            @pl.when(jlast - dgrp == w - 1)
            def _():
                stage_scores(dgrp, w, True)
                stage_sums(dgrp, w)
                stage_weights(dgrp, w)

        def weights_body(g, carry):
            stage_weights(SB_GROUP * (nfull - 1 - g), SB_GROUP)
            return carry

        lax.fori_loop(0, nfull, weights_body, 0)

        o_ref[0, 0:rows, :] = jnp.where(lane < DH_C, acc_ref[0, 0:rows, :], acc_ref[1, 0:rows, :]).astype(BF16)

    tail = lq % tq
    if tail == 0:
        process(tq)
    else:
        @pl.when(qi < nq - 1)
        def _():
            process(tq)

        @pl.when(qi == nq - 1)
        def _():
            process(tail)


def _stick_breaking(q, k, v, *, tq, q_pos0):
    bsz, lq, _ = q.shape
    lk = k.shape[2]
    nq = -(-lq // tq)
    nkt = SB_GROUP * (-(-lk // (SB_GROUP * KEY_TILE)))
    kern = functools.partial(_sb_kernel, tq=tq, lq=lq, lk=lk, q_pos0=q_pos0)
    return pl.pallas_call(
        kern,
        grid=(bsz, H_C // 2, nq),
        in_specs=[
            pl.BlockSpec((1, tq, 2 * DH_C), lambda b, h, i: (b, i, h)),
            pl.BlockSpec((1, 2, lk, DH_C), lambda b, h, i: (b, h, 0, 0)),
            pl.BlockSpec((1, 2, lk, DH_C), lambda b, h, i: (b, h, 0, 0)),
        ],
        out_specs=pl.BlockSpec((1, tq, 2 * DH_C), lambda b, h, i: (b, i, h)),
        out_shape=jax.ShapeDtypeStruct((bsz, lq, D_MODEL), BF16),
        scratch_shapes=[
            pltpu.VMEM((nkt * KEY_TILE, 2 * DH_C), BF16),
            pltpu.VMEM((nkt * KEY_TILE, 2 * DH_C), BF16),
            pltpu.VMEM((2, tq, 2 * DH_C), F32),
            pltpu.VMEM((2, tq, KEY_TILE), F32),
            pltpu.VMEM((2, nkt, tq, KEY_TILE), F32),
            pltpu.VMEM((2, nkt, tq, KEY_TILE), F32),
            pltpu.VMEM((2, nkt, tq, 2 * KEY_TILE), BF16),
        ],
        compiler_params=_params(("parallel", "parallel", "arbitrary")),
        name="stick_breaking",
    )(q, k, v)


def _layer_norm(z, g, b):
    mu = jnp.mean(z, axis=1, keepdims=True)
    zc = z - mu
    var = jnp.mean(zc * zc, axis=1, keepdims=True)
    return zc * lax.rsqrt(var + LN_EPS) * g + b


def _outproj_kernel(*refs, n_in):
    mix_refs = refs[:n_in]
    (w_ref, x_ref, g_ref, b_ref, wr_ref, br_ref,
     x1_ref, x1b_ref, ti_ref, tg_ref, rk_ref, cnt_ref, tri_ref) = refs[n_in:]
    i = pl.program_id(0)
    tm = x_ref.shape[0]

    @pl.when(i == 0)
    def _():
        cnt_ref[...] = jnp.zeros_like(cnt_ref)
        tri_ref[...] = (lax.broadcasted_iota(jnp.int32, (tm, tm), 0)
                        < lax.broadcasted_iota(jnp.int32, (tm, tm), 1)).astype(BF16)

    y = None
    c0 = 0
    for r in mix_refs:
        wd = r.shape[1]
        part = _dot(r[...], w_ref[c0:c0 + wd, :])
        y = part if y is None else y + part
        c0 += wd
    x1 = _layer_norm(DN_ALPHA * x_ref[...] + y, g_ref[...], b_ref[...])
    x1_ref[...] = x1
    x1b_ref[...] = x1.astype(BF16)
    logits = _dot_nt(wr_ref[...], x1, precision=lax.Precision.HIGHEST) + br_ref[...]
    ei = lax.broadcasted_iota(jnp.int32, (N_EXPERTS, tm), 0)
    cur = logits
    vals, idxs = [], []
    for _k in range(TOP_K):
        mx = jnp.max(cur, axis=0, keepdims=True)
        ix = jnp.min(jnp.where(cur == mx, ei, N_EXPERTS), axis=0, keepdims=True)
        vals.append(mx)
        idxs.append(ix)
        cur = jnp.where(ei == ix, -jnp.inf, cur)
    es = [jnp.exp(vv - vals[0]) for vv in vals]
    tot = es[0] + es[1] + es[2] + es[3]
    ti_ref[...] = jnp.concatenate(idxs, axis=0)
    tg_ref[...] = jnp.concatenate([e / tot for e in es], axis=0)
    run = cnt_ref[...]
    ranks = []
    for kk in range(TOP_K):
        oh = (ei == idxs[kk])
        before = _dot(oh.astype(BF16), tri_ref[...])
        ranks.append(jnp.sum(jnp.where(oh, before + run, 0.0), axis=0, keepdims=True))
        run = run + jnp.sum(oh.astype(F32), axis=1, keepdims=True)
    rk_ref[...] = jnp.concatenate(ranks, axis=0).astype(jnp.int32)
    cnt_ref[...] = run


def _outproj_ln_router(mixes, w, x_all, g, b, wr_t, br, tile0, ntiles):
    tm = TOKEN_TILE
    n = ntiles * tm
    kern = functools.partial(_outproj_kernel, n_in=len(mixes))
    row_in = lambda wd: pl.BlockSpec((tm, wd), lambda i: (tile0 + i, 0))
    row = lambda wd: pl.BlockSpec((tm, wd), lambda i: (i, 0))
    full = lambda *shape: pl.BlockSpec(shape, lambda i: (0,) * len(shape))
    lanes = lambda rows: pl.BlockSpec((rows, tm), lambda i: (0, i))
    return pl.pallas_call(
        kern,
        grid=(ntiles,),
        in_specs=[row_in(m.shape[1]) for m in mixes] + [
            full(D_MODEL, D_MODEL), row_in(D_MODEL), full(1, D_MODEL), full(1, D_MODEL),
            full(N_EXPERTS, D_MODEL), full(N_EXPERTS, 1)],
        out_specs=[row(D_MODEL), row(D_MODEL), lanes(TOP_K), lanes(TOP_K), lanes(TOP_K), full(N_EXPERTS, 1)],
        out_shape=[
            jax.ShapeDtypeStruct((n, D_MODEL), F32),
            jax.ShapeDtypeStruct((n, D_MODEL), BF16),
            jax.ShapeDtypeStruct((TOP_K, n), jnp.int32),
            jax.ShapeDtypeStruct((TOP_K, n), F32),
            jax.ShapeDtypeStruct((TOP_K, n), jnp.int32),
            jax.ShapeDtypeStruct((N_EXPERTS, 1), F32),
        ],
        scratch_shapes=[pltpu.VMEM((tm, tm), BF16)],
        compiler_params=_params(("arbitrary",)),
        name="outproj_ln_router",
    )(*mixes, w, x_all, g, b, wr_t, br)


def _moe_kernel(be_ref, nu_ref, x_ref, wgu_ref, bgu_ref, wdn_ref, bdn_ref, o_ref):
    @pl.when(pl.program_id(0) < nu_ref[0])
    def _():
        h = _dot(x_ref[...], wgu_ref[0]) + bgu_ref[0]
        glu = jnp.minimum(h[:, 0:D_EXPERT], SWIGLU_LIMIT)
        lin = jnp.clip(h[:, D_EXPERT:2 * D_EXPERT], -SWIGLU_LIMIT, SWIGLU_LIMIT)
        act = glu * _sigmoid(SWIGLU_ALPHA * glu) * (lin + 1.0)
        o_ref[...] = (_dot(act.astype(BF16), wdn_ref[0]) + bdn_ref[0]).astype(BF16)


def _moe_experts(blk_e, n_used, xg, wgu, bgu, wdn, bdn):
    nb = blk_e.shape[0]
    grid_spec = pltpu.PrefetchScalarGridSpec(
        num_scalar_prefetch=2,
        grid=(nb,),
        in_specs=[
            pl.BlockSpec((MOE_BLOCK, D_MODEL), lambda i, be, nu: (i, 0)),
            pl.BlockSpec((1, D_MODEL, 2 * D_EXPERT), lambda i, be, nu: (be[i], 0, 0)),
            pl.BlockSpec((1, 1, 2 * D_EXPERT), lambda i, be, nu: (be[i], 0, 0)),
            pl.BlockSpec((1, D_EXPERT, D_MODEL), lambda i, be, nu: (be[i], 0, 0)),
            pl.BlockSpec((1, 1, D_MODEL), lambda i, be, nu: (be[i], 0, 0)),
        ],
        out_specs=pl.BlockSpec((MOE_BLOCK, D_MODEL), lambda i, be, nu: (i, 0)),
    )
    return pl.pallas_call(
        _moe_kernel,
        grid_spec=grid_spec,
        out_shape=jax.ShapeDtypeStruct((nb * MOE_BLOCK, D_MODEL), BF16),
        compiler_params=_params(("arbitrary",)),
        name="moe_experts",
    )(blk_e, n_used, xg, wgu, bgu, wdn, bdn)


def _deinterleave_kernel(w_ref, p_ref, o_ref):
    o_ref[0] = _dot(w_ref[0].astype(BF16), p_ref[...]).astype(BF16)


def _deinterleave_gu(w):
    n, d, f2 = w.shape
    col = jnp.arange(f2, dtype=jnp.int32)
    src = jnp.where(col < f2 // 2, 2 * col, 2 * (col - f2 // 2) + 1)
    perm = (col[:, None] == src[None, :]).astype(BF16)
    rows = 512
    return pl.pallas_call(
        _deinterleave_kernel,
        grid=(n, d // rows),
        in_specs=[
            pl.BlockSpec((1, rows, f2), lambda e, r: (e, r, 0)),
            pl.BlockSpec((f2, f2), lambda e, r: (0, 0)),
        ],
        out_specs=pl.BlockSpec((1, rows, f2), lambda e, r: (e, r, 0)),
        out_shape=jax.ShapeDtypeStruct((n, d, f2), BF16),
        compiler_params=_params(("parallel", "parallel")),
        name="deinterleave_gu",
    )(w, perm)


def _combine_ln_kernel(x_ref, r_ref, tg_ref, g_ref, b_ref, o_ref):
    tg = tg_ref[...]
    f = r_ref[0].astype(F32) * tg[:, 0:1]
    for kk in range(1, TOP_K):
        f = f + r_ref[kk].astype(F32) * tg[:, kk:kk + 1]
    o_ref[...] = _layer_norm(DN_ALPHA * x_ref[...] + f, g_ref[...], b_ref[...])


def _combine_ln(x1, rows, gates, g, b):
    n = x1.shape[0]
    tm = TOKEN_TILE
    row = pl.BlockSpec((tm, D_MODEL), lambda i: (i, 0))
    vec = pl.BlockSpec((1, D_MODEL), lambda i: (0, 0))
    return pl.pallas_call(
        _combine_ln_kernel,
        grid=(n // tm,),
        in_specs=[row, pl.BlockSpec((TOP_K, tm, D_MODEL), lambda i: (0, i, 0)),
                  pl.BlockSpec((tm, TOP_K), lambda i: (i, 0)), vec, vec],
        out_specs=row,
        out_shape=jax.ShapeDtypeStruct((n, D_MODEL), F32),
        compiler_params=_params(("parallel",)),
        name="combine_ln",
    )(x1, rows, gates, g, b)


def _moe_layer(x1, x1b, top_i, top_g, rank, sizes, e0, wgu, bgu, wdn, bdn, g, b):
    n = x1.shape[0]
    nb = -(-(n * TOP_K) // MOE_BLOCK) + N_EXPERTS
    sizes = sizes[:, 0].astype(jnp.int32)
    nblk = (sizes + MOE_BLOCK - 1) // MOE_BLOCK
    blk_end = jnp.cumsum(nblk)
    pad_starts = (blk_end - nblk) * MOE_BLOCK
    n_used = blk_end[-1]
    blk_ids = jnp.minimum(jnp.arange(nb, dtype=jnp.int32), n_used - 1)
    blk_e = jnp.sum((blk_ids[:, None] >= blk_end[None, :]).astype(jnp.int32), axis=1)
    blk_e = jnp.clip(blk_e, 0, N_EXPERTS - 1)
    experts = jnp.arange(N_EXPERTS, dtype=jnp.int32)
    dest = jnp.sum(jnp.where(top_i[:, :, None] == experts, pad_starts, 0), axis=2) + rank
    tok = jnp.broadcast_to(jnp.arange(n, dtype=jnp.int32)[None], (TOP_K, n))
    tok_of_row = jnp.zeros((nb * MOE_BLOCK,), jnp.int32).at[dest.reshape(-1)].set(
        tok.reshape(-1), unique_indices=True, indices_are_sorted=False)
    xg = x1b.at[tok_of_row].get(mode="promise_in_bounds")
    out = _moe_experts(blk_e + e0, n_used.reshape(1), xg, wgu, bgu, wdn, bdn)
    rows = out.at[dest.reshape(-1)].get(mode="promise_in_bounds").reshape(TOP_K, n, D_MODEL)
    return _combine_ln(x1, rows, jnp.transpose(top_g), g, b)


def _rel_bucket(rel):
    half = N_BUCKETS // 2
    exact = half // 2
    ret = jnp.where(rel > 0, half, 0)
    n = jnp.abs(rel)
    large = exact + (jnp.log(jnp.maximum(n, 1).astype(F32) / exact)
                     / math.log(MAX_DISTANCE / exact) * (half - exact)).astype(jnp.int32)
    large = jnp.minimum(large, half - 1)
    return ret + jnp.where(n < exact, n, large)


def _near_bias_tiles(rel_table, tq, q_pos0):
    base = q_pos0 % KEY_TILE
    i = jnp.arange(tq, dtype=jnp.int32)[:, None] + base
    j = jnp.arange(KEY_TILE, dtype=jnp.int32)[None, :]
    tiles = [jnp.full((tq, KEY_TILE), -2 * MAX_DISTANCE, jnp.int32)]
    for d in (-KEY_TILE, 0, KEY_TILE):
        tiles.append(d + j - i)
    rel = jnp.stack(tiles)
    return jnp.transpose(rel_table[_rel_bucket(rel)], (3, 0, 1, 2)).astype(F32) * LOG2_E


def kernel(x_prompt, x_sample, cache_diff_k, cache_diff_v, state_mlstm_C, state_mlstm_n, state_mlstm_m, state_mlstm_conv, cache_sb_k, cache_sb_v, meta_tokens, rel_bias, w_in_ab, w_out_ab, conv_w_a, conv_b_a, w_aq_a, w_ak_a, b_if_a, mh_gain_a, lam_q1, lam_k1, lam_q2, lam_k2, subln_gain_b, w_in_c, w_out_c, ln_g, ln_b, w_router, b_router, w_gu, b_gu, w_down, b_down):
    bp, sp, _ = x_prompt.shape
    bs, ss, _ = x_sample.shape
    lp = N_META + sp
    past = cache_diff_k.shape[3]
    n_p = bp * lp
    n_s = bs * ss
    tm_p = lp // 3 if (lp % 3 == 0 and (lp // 3) % 16 == 0) else lp
    tq_p = KEY_TILE

    x0 = jnp.concatenate([jnp.broadcast_to(meta_tokens[None], (bp, N_META, D_MODEL)).astype(x_prompt.dtype),
                          x_prompt], axis=1)
    x_all = jnp.concatenate([x0.reshape(n_p, D_MODEL), x_sample.reshape(n_s, D_MODEL)], axis=0)

    j = 0
    lam_init = 0.8 - 0.6 * math.exp(-0.3 * 0)
    w_ab = w_in_ab[j]
    w_perm = jnp.concatenate([
        w_ab[:, 0:OFF_AIF],
        jnp.pad(w_ab[:, OFF_AIF:OFF_BQ], ((0, 0), (0, LANE - 2 * H_A))),
        w_ab[:, OFF_BQ:OFF_BQ + W_B] * (DH_B ** -0.5 * LOG2_E),
        w_ab[:, OFF_BQ + W_B:]], axis=1).astype(BF16)
    bg = jnp.pad(b_if_a[j], (0, LANE - 2 * H_A)).reshape(1, LANE)
    lam = (jnp.exp(jnp.sum(lam_q1[j] * lam_k1[j])) - jnp.exp(jnp.sum(lam_q2[j] * lam_k2[j]))).astype(F32) + lam_init
    lam_v = jnp.broadcast_to(lam.reshape(1, 1), (1, LANE))
    cw = conv_w_a[j]
    cb = conv_b_a[j].reshape(1, W_A)
    wq = w_aq_a[j].astype(BF16)
    wk = w_ak_a[j].astype(BF16)
    gain_a = mh_gain_a[j].reshape(1, W_A)
    gain_b = subln_gain_b[j].reshape(1, 2 * DH_B)

    def ab_group(row0, bsz, seq, tm, lead, state, past_kv, tq, q_pos0, chunked):
        a, g, q, k_new, v_new = _inproj_ab(x_all, row0, bsz, seq, tm, w_perm, bg)
        c0, n0, m0, buf = state
        mix_a, c1, n1, m1, cs = _mlstm(
            a, g, bsz, seq, lead, cw, cb, wq, wk, gain_a,
            c0, n0.reshape(bsz, H_A, 1, DH_A),
            jnp.broadcast_to(m0[:, :, None, None], (bsz, H_A, 1, LANE)), buf)
        if past_kv is None:
            k_all, v_all = k_new, v_new
        else:
            k_all = jnp.concatenate([past_kv[0], k_new], axis=2)
            v_all = jnp.concatenate([past_kv[1], v_new], axis=2)
        near = _near_bias_tiles(rel_bias, tq, q_pos0)
        mix_b = _diff_attention(q.reshape(bsz, seq, W_B), k_all, v_all, near, lam_v, gain_b,
                                tq=tq, q_pos0=q_pos0, chunked=chunked, lam_init=lam_init)
        outs = (k_new, v_new, c1, n1.reshape(bsz, H_A, DH_A), m1[:, :, 0, 0], cs)
        return mix_a, mix_b.reshape(bsz * seq, W_B), outs

    zero_state = (jnp.zeros((bp, H_A, DH_A, DH_A), F32), jnp.zeros((bp, H_A, DH_A), F32),
                  jnp.zeros((bp, H_A), F32), jnp.zeros((bp, CONV_W - 1, W_A), F32))
    mix_a_p, mix_b_p, ab_p = ab_group(0, bp, lp, tm_p, N_META, zero_state, None, tq_p, 0, True)
    s_state = (state_mlstm_C[j], state_mlstm_n[j], state_mlstm_m[j], state_mlstm_conv[j])
    mix_a_s, mix_b_s, ab_s = ab_group(n_p, bs, ss, ss, 0, s_state, (cache_diff_k[j], cache_diff_v[j]),
                                      ss, past, False)
    mix_a = jnp.concatenate([mix_a_p, mix_a_s], axis=0)
    mix_b = jnp.concatenate([mix_b_p, mix_b_s], axis=0)

    n_le = w_gu.shape[0] * N_EXPERTS
    wgu = _deinterleave_gu(w_gu.reshape(n_le, D_MODEL, 2 * D_EXPERT))
    bgu = jnp.concatenate([b_gu[..., 0::2], b_gu[..., 1::2]], axis=-1).reshape(n_le, 1, 2 * D_EXPERT)
    wdn = w_down.reshape(n_le, D_EXPERT, D_MODEL).astype(BF16)
    bdn = b_down.reshape(n_le, 1, D_MODEL)

    def token_stage(layer, mixes, w_out, x_in):
        wr_t = jnp.transpose(w_router[layer])
        br = b_router[layer].reshape(N_EXPERTS, 1)
        w_out = w_out.astype(BF16)
        ntile = x_in.shape[0] // TOKEN_TILE
        bounds = [ntile * s // MOE_SPLIT for s in range(MOE_SPLIT + 1)]
        outs = []
        for t0, t1 in zip(bounds[:-1], bounds[1:]):
            x1, x1b, top_i, top_g, rank, sizes = _outproj_ln_router(
                mixes, w_out, x_in, ln_g[layer, 0].reshape(1, D_MODEL), ln_b[layer, 0].reshape(1, D_MODEL),
                wr_t, br, t0, t1 - t0)
            outs.append(_moe_layer(x1, x1b, top_i, top_g, rank, sizes, layer * N_EXPERTS, wgu, bgu, wdn, bdn,
                                   ln_g[layer, 1].reshape(1, D_MODEL), ln_b[layer, 1].reshape(1, D_MODEL)))
        return jnp.concatenate(outs, axis=0)

    x_all = token_stage(0, [mix_a, mix_b], w_out_ab[j], x_all)

    q_scale = jnp.where(jnp.arange(3 * D_MODEL) < D_MODEL, DH_C ** -0.5 * LOG2_E, 1.0).astype(F32)
    w_c = (w_in_c[j] * q_scale).astype(BF16)

    def c_group(row0, bsz, seq, tm, past_kv, tq, q_pos0):
        q, k_new, v_new = _inproj_c(x_all, row0, bsz, seq, tm, w_c)
        if past_kv is None:
            k_all, v_all = k_new, v_new
        else:
            k_all = jnp.concatenate([past_kv[0], k_new], axis=2)
            v_all = jnp.concatenate([past_kv[1], v_new], axis=2)
        o = _stick_breaking(q.reshape(bsz, seq, D_MODEL), k_all, v_all, tq=tq, q_pos0=q_pos0)
        return o.reshape(bsz * seq, D_MODEL), (k_new, v_new)

    mix_p, c_p = c_group(0, bp, lp, tm_p, None, 2 * KEY_TILE, 0)
    mix_s, c_s = c_group(n_p, bs, ss, ss, (cache_sb_k[j], cache_sb_v[j]), ss, past)
    x_all = token_stage(1, [jnp.concatenate([mix_p, mix_s], axis=0)], w_out_c[j], x_all)

    y_prompt = x_all[:n_p].reshape(bp, lp, D_MODEL)[:, N_META:]
    y_sample = x_all[n_p:].reshape(bs, ss, D_MODEL)
    stack = lambda t: t[None]
    return (y_prompt, y_sample,
            stack(ab_p[0]), stack(ab_p[1]), stack(ab_p[2]), stack(ab_p[3]), stack(ab_p[4]), stack(ab_p[5]),
            stack(c_p[0]), stack(c_p[1]),
            stack(ab_s[0]), stack(ab_s[1]), stack(ab_s[2]), stack(ab_s[3]), stack(ab_s[4]), stack(ab_s[5]),
            stack(c_s[0]), stack(c_s[1]))
```

```python
import functools
import math

import jax
import jax.numpy as jnp
from jax import lax
from jax.experimental import pallas as pl
from jax.experimental.pallas import tpu as pltpu

F32 = jnp.float32
BF16 = jnp.bfloat16

D_MODEL = 1024
DEPTH = 2
CHUNK = 64
N_META = 16
H_A = 4
DH_A = 128
W_A = H_A * DH_A
CONV_W = 4
H_B = 4
DH_B = 64
W_B = H_B * 2 * DH_B
H_C = 16
DH_C = 64
N_BUCKETS = 32
MAX_DISTANCE = 128
N_EXPERTS = 32
TOP_K = 4
D_EXPERT = D_MODEL // 2
SWIGLU_LIMIT = 7.0
SWIGLU_ALPHA = 1.702
DN_ALPHA = (2 * DEPTH) ** 0.25
LN_EPS = 1e-5
OFF_AIF = 3 * W_A
OFF_BQ = OFF_AIF + 2 * H_A

LANE = 128
KEY_TILE = 128
LOG2_KEY_TILE = 7
LOG2_CHUNK = 6
VMEM_LIMIT = 56 * 1024 * 1024
MOE_BLOCK = 512
TOKEN_TILE = 512
SB_GROUP = 8
MOE_SPLIT = 2
NEG_BIG = -1e30
LOG2_E = 1.4426950408889634

COL_G = 3 * W_A
COL_Q = COL_G + LANE
COL_K = COL_Q + W_B
COL_V = COL_K + W_B
D_IN_AB_PAD = COL_V + W_B


def _dot(a, b):
    return jnp.dot(a, b, preferred_element_type=F32)


def _dot_nt(a, b, precision=None):
    return lax.dot_general(a, b, (((1,), (1,)), ((), ())), preferred_element_type=F32, precision=precision)


def _dot_tn(a, b):
    return lax.dot_general(a, b, (((0,), (0,)), ((), ())), preferred_element_type=F32)


def _log_sigmoid(x):
    return jnp.minimum(x, 0.0) - jnp.log(1.0 + jnp.exp(-jnp.abs(x)))


def _sigmoid(x):
    return 1.0 / (1.0 + jnp.exp(-x))


def _chunk_id(pos):
    return (pos + (CHUNK - N_META)) >> LOG2_CHUNK


def _params(sem):
    return pltpu.CompilerParams(dimension_semantics=sem, vmem_limit_bytes=VMEM_LIMIT)


def _inproj_ab_kernel(x_ref, w_ref, bg_ref, a_ref, g_ref, q_ref, k_ref, v_ref):
    xb = x_ref[...].astype(BF16)
    a_ref[...] = _dot(xb, w_ref[:, 0:COL_G])
    g_ref[...] = _dot(xb, w_ref[:, COL_G:COL_Q]) + bg_ref[...]
    q_ref[...] = _dot(xb, w_ref[:, COL_Q:COL_K]).astype(BF16)
    for h in range(H_B):
        k_ref[0, h] = _dot(xb, w_ref[:, COL_K + 2 * DH_B * h:COL_K + 2 * DH_B * (h + 1)])
        v_ref[0, h] = _dot(xb, w_ref[:, COL_V + 2 * DH_B * h:COL_V + 2 * DH_B * (h + 1)])


def _inproj_ab(x_all, row0, bsz, seq, tm, w, bg):
    nrt = seq // tm
    n = bsz * seq
    if row0 % tm:
        x_all, row0 = x_all[row0:row0 + n], 0
    off = row0 // tm
    return pl.pallas_call(
        _inproj_ab_kernel,
        grid=(bsz, nrt),
        in_specs=[
            pl.BlockSpec((tm, D_MODEL), lambda b, r: (off + b * nrt + r, 0)),
            pl.BlockSpec((D_MODEL, D_IN_AB_PAD), lambda b, r: (0, 0)),
            pl.BlockSpec((1, LANE), lambda b, r: (0, 0)),
        ],
        out_specs=[
            pl.BlockSpec((tm, COL_G), lambda b, r: (b * nrt + r, 0)),
            pl.BlockSpec((tm, LANE), lambda b, r: (b * nrt + r, 0)),
            pl.BlockSpec((tm, W_B), lambda b, r: (b * nrt + r, 0)),
            pl.BlockSpec((1, H_B, tm, 2 * DH_B), lambda b, r: (b, 0, r, 0)),
            pl.BlockSpec((1, H_B, tm, 2 * DH_B), lambda b, r: (b, 0, r, 0)),
        ],
        out_shape=[
            jax.ShapeDtypeStruct((n, COL_G), F32),
            jax.ShapeDtypeStruct((n, LANE), F32),
            jax.ShapeDtypeStruct((n, W_B), BF16),
            jax.ShapeDtypeStruct((bsz, H_B, seq, 2 * DH_B), F32),
            jax.ShapeDtypeStruct((bsz, H_B, seq, 2 * DH_B), F32),
        ],
        compiler_params=_params(("parallel", "parallel")),
        name="inproj_ab",
    )(x_all, w, bg)


def _mlstm_kernel(a_ref, g_ref, cw_ref, cb_ref, wq_ref, wk_ref, gain_ref, c0_ref, n0_ref, m0_ref, buf_ref,
                  out_ref, c1_ref, n1_ref, m1_ref, cs_ref, q_s, num_s, rs_s, u_s, nv_s, *, seq, lead):
    cs_ref[0] = a_ref[seq - (CONV_W - 1):seq, 0:W_A]
    c1_ref[...] = c0_ref[...]
    n1_ref[...] = n0_ref[...]
    m1_ref[...] = m0_ref[...]
    sel = (lax.broadcasted_iota(jnp.int32, (8, LANE), 0) == lax.broadcasted_iota(jnp.int32, (8, LANE), 1)).astype(F32)
    lane = lax.broadcasted_iota(jnp.int32, (1, LANE), 1)

    def local(c, r0, lc, first):
        if first:
            win = jnp.concatenate([jnp.zeros((5, W_A), F32), buf_ref[0], a_ref[0:lc, 0:W_A]], axis=0)
        else:
            win = a_ref[pl.ds(pl.multiple_of(r0 - 8, 8), lc + 8), 0:W_A]
        y = cb_ref[...]
        for j in range(CONV_W):
            y = y + win[5 + j:5 + j + lc, :] * cw_ref[j:j + 1, :]
        ca = y * _sigmoid(y)
        g = g_ref[pl.ds(r0, lc), :]
        g_rows = _dot_nt(sel, g, precision=lax.Precision.HIGHEST)
        ti = lax.broadcasted_iota(jnp.int32, (lc, lc), 0)
        si = lax.broadcasted_iota(jnp.int32, (lc, lc), 1)
        causal = si <= ti
        rs = jnp.zeros((lc, LANE), F32)
        for h in range(H_A):
            hs = slice(DH_A * h, DH_A * (h + 1))
            ig_c = g[:, h:h + 1]
            lf_c = _log_sigmoid(g[:, H_A + h:H_A + h + 1])
            ig_r = g_rows[h:h + 1, :]
            lf_r = _log_sigmoid(g_rows[H_A + h:H_A + h + 1, :])
            b_c = jnp.sum(jnp.where(causal, lf_r, 0.0), axis=1, keepdims=True)
            b_r = jnp.sum(jnp.where(ti <= si, lf_c, 0.0), axis=0, keepdims=True)
            dmat = jnp.where(causal, b_c - b_r + ig_r, -jnp.inf)
            m_loc = jnp.max(dmat, axis=1, keepdims=True)
            w = jnp.exp(dmat - m_loc)
            cab = ca[:, hs].astype(BF16)
            qb = _dot(cab, wq_ref[h]).astype(BF16)
            k = _dot(cab, wk_ref[h]) * (DH_A ** -0.5)
            vb = a_ref[pl.ds(r0, lc), W_A + DH_A * h:W_A + DH_A * (h + 1)].astype(BF16)
            s = _dot_nt(qb, k.astype(BF16)) * w
            q_s[pl.ds(r0, lc), hs] = qb
            num_s[pl.ds(r0, lc), hs] = _dot(s.astype(BF16), vb)
            den_loc = jnp.sum(s, axis=1, keepdims=True)
            rs = jnp.where(lane == h, den_loc, rs)
            rs = jnp.where(lane == H_A + h, m_loc, rs)
            rs = jnp.where(lane == 2 * H_A + h, b_c, rs)
            w_end = jnp.exp(b_c[lc - 1:lc, :] - b_c + ig_c - m_loc[lc - 1:lc, :])
            kw = k * w_end
            u_s[c, h] = _dot_tn(kw.astype(BF16), vb)
            nv_s[c, h] = jnp.sum(kw, axis=0, keepdims=True)
        rs_s[pl.ds(r0, lc), :] = rs

    def carry(c, r0, lc):
        rs = rs_s[pl.ds(r0, lc), :]
        for h in range(H_A):
            hs = slice(DH_A * h, DH_A * (h + 1))
            den_loc = jnp.broadcast_to(rs[:, h:h + 1], (lc, DH_A))
            m_loc = jnp.broadcast_to(rs[:, H_A + h:H_A + h + 1], (lc, DH_A))
            b_c = jnp.broadcast_to(rs[:, 2 * H_A + h:2 * H_A + h + 1], (lc, DH_A))
            m_prev = m1_ref[0, h]
            c_prev = c1_ref[0, h]
            n_prev = n1_ref[0, h]
            inter = b_c + m_prev
            m_t = jnp.maximum(inter, m_loc)
            gg = jnp.exp(inter - m_t)
            sc = jnp.exp(m_loc - m_t)
            qb = q_s[pl.ds(r0, lc), hs]
            num = sc * num_s[pl.ds(r0, lc), hs] + gg * _dot(qb, c_prev.astype(BF16))
            qn = jnp.broadcast_to(jnp.sum(qb.astype(F32) * n_prev, axis=1, keepdims=True), (lc, DH_A))
            den = sc * den_loc + gg * qn
            hh = num / jnp.maximum(jnp.abs(den), jnp.exp(-m_t))
            m_new = m_t[lc - 1:lc, :]
            decay = jnp.exp(inter[lc - 1:lc, :] - m_new)
            grow = sc[lc - 1:lc, :]
            c1_ref[0, h] = decay * c_prev + grow * u_s[c, h]
            n1_ref[0, h] = decay * n_prev + grow * nv_s[c, h]
            m1_ref[0, h] = m_new
            hn = hh * lax.rsqrt(jnp.mean(hh * hh, axis=1, keepdims=True) + LN_EPS) * gain_ref[:, hs]
            oa = a_ref[pl.ds(r0, lc), 2 * W_A + DH_A * h:2 * W_A + DH_A * (h + 1)]
            out_ref[pl.ds(r0, lc), hs] = (hn * _sigmoid(oa)).astype(BF16)

    first_len = lead if lead else CHUNK
    nrest = (seq - first_len) // CHUNK
    start = lambda i: pl.multiple_of(first_len + i * CHUNK, 16)
    local(0, 0, first_len, True)

    def local_body(i, c):
        local(2 * i + 1, start(2 * i), CHUNK, False)
        local(2 * i + 2, start(2 * i + 1), CHUNK, False)
        return c

    lax.fori_loop(0, nrest // 2, local_body, 0)
    if nrest % 2:
        local(nrest, start(nrest - 1), CHUNK, False)
    carry(0, 0, first_len)

    def carry_body(i, c):
        carry(i + 1, start(i), CHUNK)
        return c

    lax.fori_loop(0, nrest, carry_body, 0)


def _mlstm(a, g, bsz, seq, lead, cw, cb, wq, wk, gain, c0, n0, m0, buf):
    kern = functools.partial(_mlstm_kernel, seq=seq, lead=lead)
    assert seq >= CONV_W - 1
    nchunks = 1 + (seq - (lead if lead else CHUNK)) // CHUNK
    full = lambda *shape: pl.BlockSpec(shape, lambda b: (0,) * len(shape))
    per_b = lambda *shape: pl.BlockSpec((1,) + shape, lambda b: (b,) + (0,) * len(shape))
    return pl.pallas_call(
        kern,
        grid=(bsz,),
        in_specs=[
            pl.BlockSpec((seq, COL_G), lambda b: (b, 0)),
            pl.BlockSpec((seq, LANE), lambda b: (b, 0)),
            full(CONV_W, W_A), full(1, W_A), full(H_A, DH_A, DH_A), full(H_A, DH_A, DH_A), full(1, W_A),
            per_b(H_A, DH_A, DH_A), per_b(H_A, 1, DH_A), per_b(H_A, 1, LANE), per_b(CONV_W - 1, W_A),
        ],
        out_specs=[
            pl.BlockSpec((seq, W_A), lambda b: (b, 0)),
            per_b(H_A, DH_A, DH_A), per_b(H_A, 1, DH_A), per_b(H_A, 1, LANE), per_b(CONV_W - 1, W_A),
        ],
        out_shape=[
            jax.ShapeDtypeStruct((bsz * seq, W_A), BF16),
            jax.ShapeDtypeStruct((bsz, H_A, DH_A, DH_A), F32),
            jax.ShapeDtypeStruct((bsz, H_A, 1, DH_A), F32),
            jax.ShapeDtypeStruct((bsz, H_A, 1, LANE), F32),
            jax.ShapeDtypeStruct((bsz, CONV_W - 1, W_A), F32),
        ],
        scratch_shapes=[
            pltpu.VMEM((seq, W_A), BF16),
            pltpu.VMEM((seq, W_A), F32),
            pltpu.VMEM((seq, LANE), F32),
            pltpu.VMEM((nchunks, H_A, DH_A, DH_A), F32),
            pltpu.VMEM((nchunks, H_A, 1, DH_A), F32),
        ],
        compiler_params=_params(("parallel",)),
        name="mlstm",
    )(a, g, cw, cb, wq, wk, gain, c0, n0, m0, buf)


def _diff_kernel(lam_ref, q_ref, k_ref, v_ref, near_ref, gain_ref, o_ref, kb_ref, vb_ref, s_ref, mx_ref, l_ref,
                 acc_ref, *, tq, lq, lk, q_pos0, chunked, lam_init):
    qi = pl.program_id(2)
    nq = pl.num_programs(2)
    rows_k = kb_ref.shape[0]

    @pl.when(qi == 0)
    def _():
        kb_ref[0:lk, :] = k_ref[0, 0].astype(BF16)
        vb_ref[0:lk, :] = v_ref[0, 0].astype(BF16)
        kb_ref[lk:, :] = jnp.zeros((rows_k - lk, 2 * DH_B), BF16)
        vb_ref[lk:, :] = jnp.zeros((rows_k - lk, 2 * DH_B), BF16)

    q0 = q_pos0 + qi * tq
    qt = q0 >> LOG2_KEY_TILE
    lane = lax.broadcasted_iota(jnp.int32, (1, 2 * DH_B), 1)
    bias_far = near_ref[0, 0, 0:1, 0:1]
    span = near_ref.shape[1] - 3

    def process(rows):
        r2 = 2 * rows
        qf = q_ref[0, 0:rows, :].astype(F32)
        qs = jnp.concatenate([jnp.where(lane < DH_B, qf, 0.0), jnp.where(lane >= DH_B, qf, 0.0)],
                             axis=0).astype(BF16)
        mx_ref[0:r2, :] = jnp.full((r2, KEY_TILE), NEG_BIG, F32)

        def step(j0, width, mode):
            ks = pl.multiple_of(j0 * KEY_TILE, KEY_TILE)
            kt = kb_ref[pl.ds(ks, width * KEY_TILE), :]
            s = _dot_nt(qs, kt)
            if mode == "far":
                s = s + bias_far
            else:
                first = span + 3 - width
                bias = jnp.concatenate([near_ref[0, first + g, 0:rows, :] for g in range(width)], axis=1)
                kpos = ks + lax.broadcasted_iota(jnp.int32, (rows, width * KEY_TILE), 1)
                if chunked:
                    qpos = q0 + lax.broadcasted_iota(jnp.int32, (rows, width * KEY_TILE), 0)
                    mask = _chunk_id(kpos) <= _chunk_id(qpos)
                else:
                    mask = kpos < lk
                s = jnp.where(jnp.concatenate([mask, mask], axis=0),
                              s + jnp.concatenate([bias, bias], axis=0), NEG_BIG)
            mx = mx_ref[0:r2, :]
            for g in range(width):
                blk = s[:, g * KEY_TILE:(g + 1) * KEY_TILE]
                s_ref[j0 + g, 0:r2, :] = blk
                mx = jnp.maximum(mx, blk)
            mx_ref[0:r2, :] = mx

        @pl.when(qt >= 1)
        def _():
            step(qt - 1, span + 2, "near")

        @pl.when(qt == 0)
        def _():
            step(0, span + 1, "near")

        def for_tiles(count, fn):
            def body(g, carry):
                fn(8 * g, 8)
                return carry

            lax.fori_loop(0, count >> 3, body, 0)
            base = (count >> 3) << 3
            for w, done in ((4, 0), (2, 4), (1, 6)):
                @pl.when((count & w) != 0)
                def _():
                    fn(base + (count & done), w)

        for_tiles(jnp.maximum(qt - 1, 0), lambda j0, w: step(j0, w, "far"))
        m = jnp.max(mx_ref[0:r2, :], axis=1, keepdims=True)
        l_ref[0:r2, :] = jnp.zeros((r2, KEY_TILE), F32)
        acc_ref[0:r2, :] = jnp.zeros((r2, 2 * DH_B), F32)

        def weights(j0, width):
            vt = vb_ref[pl.ds(pl.multiple_of(j0 * KEY_TILE, KEY_TILE), width * KEY_TILE), :]
            lsum = l_ref[0:r2, :]
            ps = []
            for t in range(width):
                p = jnp.exp2(s_ref[j0 + t, 0:r2, :] - m)
                lsum = lsum + p
                ps.append(p.astype(BF16))
            l_ref[0:r2, :] = lsum
            acc_ref[0:r2, :] += _dot(ps[0] if width == 1 else jnp.concatenate(ps, axis=1), vt)

        for_tiles(qt + span + 1, weights)
        o = acc_ref[0:r2, :] / jnp.sum(l_ref[0:r2, :], axis=1, keepdims=True)
        lam = lam_ref[:, 0:1]
        o = o[0:rows] - lam * o[rows:2 * rows]
        o = o * lax.rsqrt(jnp.mean(o * o, axis=1, keepdims=True) + LN_EPS) * gain_ref[...] * (1.0 - lam_init)
        o_ref[0, 0:rows, :] = o.astype(BF16)

    tail = lq % tq
    if tail == 0:
        process(tq)
    else:
        @pl.when(qi < nq - 1)
        def _():
            process(tq)

        @pl.when(qi == nq - 1)
        def _():
            process(tail)


def _diff_attention(q, k, v, near, lam, gain, *, tq, q_pos0, chunked, lam_init):
    bsz, lq, _ = q.shape
    lk = k.shape[2]
    nq = -(-lq // tq)
    nkt = -(-lk // KEY_TILE)
    span = near.shape[1] - 3
    kern = functools.partial(_diff_kernel, tq=tq, lq=lq, lk=lk, q_pos0=q_pos0, chunked=chunked, lam_init=lam_init)
    return pl.pallas_call(
        kern,
        grid=(bsz, H_B, nq),
        in_specs=[
            pl.BlockSpec((1, LANE), lambda b, h, i: (0, 0)),
            pl.BlockSpec((1, tq, 2 * DH_B), lambda b, h, i: (b, i, h)),
            pl.BlockSpec((1, 1, lk, 2 * DH_B), lambda b, h, i: (b, h, 0, 0)),
            pl.BlockSpec((1, 1, lk, 2 * DH_B), lambda b, h, i: (b, h, 0, 0)),
            pl.BlockSpec((1, span + 3, tq, KEY_TILE), lambda b, h, i: (h, 0, 0, 0)),
            pl.BlockSpec((1, 2 * DH_B), lambda b, h, i: (0, 0)),
        ],
        out_specs=pl.BlockSpec((1, tq, 2 * DH_B), lambda b, h, i: (b, i, h)),
        out_shape=jax.ShapeDtypeStruct((bsz, lq, W_B), BF16),
        scratch_shapes=[
            pltpu.VMEM(((nkt + span) * KEY_TILE, 2 * DH_B), BF16),
            pltpu.VMEM(((nkt + span) * KEY_TILE, 2 * DH_B), BF16),
            pltpu.VMEM((nkt + span, 2 * tq, KEY_TILE), F32),
            pltpu.VMEM((2 * tq, KEY_TILE), F32),
            pltpu.VMEM((2 * tq, KEY_TILE), F32),
            pltpu.VMEM((2 * tq, 2 * DH_B), F32),
        ],
        compiler_params=_params(("parallel", "parallel", "arbitrary")),
        name="diff_attention",
    )(lam, q, k, v, near, gain)


def _inproj_c_kernel(x_ref, w_ref, q_ref, k_ref, v_ref):
    xb = x_ref[...].astype(BF16)
    q_ref[...] = _dot(xb, w_ref[:, 0:D_MODEL]).astype(BF16)
    yk = _dot(xb, w_ref[:, D_MODEL:2 * D_MODEL])
    for h in range(H_C):
        k_ref[0, h] = yk[:, DH_C * h:DH_C * (h + 1)]
    yv = _dot(xb, w_ref[:, 2 * D_MODEL:3 * D_MODEL])
    for h in range(H_C):
        v_ref[0, h] = yv[:, DH_C * h:DH_C * (h + 1)]


def _inproj_c(x_all, row0, bsz, seq, tm, w):
    nrt = seq // tm
    if row0 % tm:
        x_all, row0 = x_all[row0:row0 + bsz * seq], 0
    off = row0 // tm
    return pl.pallas_call(
        _inproj_c_kernel,
        grid=(bsz, nrt),
        in_specs=[
            pl.BlockSpec((tm, D_MODEL), lambda b, r: (off + b * nrt + r, 0)),
            pl.BlockSpec((D_MODEL, 3 * D_MODEL), lambda b, r: (0, 0)),
        ],
        out_specs=[
            pl.BlockSpec((tm, D_MODEL), lambda b, r: (b * nrt + r, 0)),
            pl.BlockSpec((1, H_C, tm, DH_C), lambda b, r: (b, 0, r, 0)),
            pl.BlockSpec((1, H_C, tm, DH_C), lambda b, r: (b, 0, r, 0)),
        ],
        out_shape=[
            jax.ShapeDtypeStruct((bsz * seq, D_MODEL), BF16),
            jax.ShapeDtypeStruct((bsz, H_C, seq, DH_C), F32),
            jax.ShapeDtypeStruct((bsz, H_C, seq, DH_C), F32),
        ],
        compiler_params=_params(("parallel", "parallel")),
        name="inproj_c",
    )(x_all, w)


def _sb_kernel(q_ref, k_ref, v_ref, o_ref, kb_ref, vb_ref, acc_ref, run_ref, u_ref, t_ref, hl_ref,
               *, tq, lq, lk, q_pos0):
    qi = pl.program_id(2)
    nq = pl.num_programs(2)
    rows_k = kb_ref.shape[0]

    @pl.when(qi == 0)
    def _():
        kb_ref[0:lk, :] = jnp.concatenate([k_ref[0, 0], k_ref[0, 1]], axis=1).astype(BF16)
        vb_ref[0:lk, :] = jnp.concatenate([v_ref[0, 0], v_ref[0, 1]], axis=1).astype(BF16)
        if rows_k > lk:
            kb_ref[lk:, :] = jnp.zeros((rows_k - lk, 2 * DH_C), BF16)
            vb_ref[lk:, :] = jnp.zeros((rows_k - lk, 2 * DH_C), BF16)

    q0 = q_pos0 + qi * tq
    jd = q0 >> LOG2_KEY_TILE
    jj = lax.broadcasted_iota(jnp.int32, (2 * KEY_TILE, 2 * KEY_TILE), 0) & (KEY_TILE - 1)
    ss = lax.broadcasted_iota(jnp.int32, (2 * KEY_TILE, 2 * KEY_TILE), 1)
    later = jnp.where((jj > ss) | (ss >= KEY_TILE), 1.0, 0.0).astype(BF16)
    lane = lax.broadcasted_iota(jnp.int32, (1, 2 * DH_C), 1)

    def process(rows):
        qf = q_ref[0, 0:rows, :].astype(F32)
        qh = [jnp.where(lane < DH_C, qf, 0.0).astype(BF16), jnp.where(lane >= DH_C, qf, 0.0).astype(BF16)]
        acc_ref[:, 0:rows, :] = jnp.zeros((2, rows, 2 * DH_C), F32)
        run_ref[:, 0:rows, :] = jnp.zeros((2, rows, KEY_TILE), F32)

        def stage_scores(j0, width, masked):
            ks = pl.multiple_of(j0 * KEY_TILE, KEY_TILE)
            kt = kb_ref[pl.ds(ks, width * KEY_TILE), :]
            for h in range(2):
                z = _dot_nt(qh[h], kt)
                sp = jnp.maximum(z, 0.0) + jnp.log(1.0 + jnp.exp2(-jnp.abs(z))) * LOG2_E
                u = z - sp
                if masked:
                    mask = (ks + lax.broadcasted_iota(jnp.int32, (rows, width * KEY_TILE), 1)) < (
                        q0 + lax.broadcasted_iota(jnp.int32, (rows, width * KEY_TILE), 0))
                    sp = jnp.where(mask, sp, 0.0)
                    u = jnp.where(mask, u, NEG_BIG)
                hi = sp.astype(BF16)
                lo = (sp - hi.astype(F32)).astype(BF16)
                for g in range(width):
                    cols = slice(g * KEY_TILE, (g + 1) * KEY_TILE)
                    u_ref[h, j0 + g, 0:rows, :] = u[:, cols]
                    hl_ref[h, j0 + g, 0:rows, :] = jnp.concatenate([hi[:, cols], lo[:, cols]], axis=1)

        def stage_weights(j0, width):
            ks = pl.multiple_of(j0 * KEY_TILE, KEY_TILE)
            vt = vb_ref[pl.ds(ks, width * KEY_TILE), :]
            for h in range(2):
                off = run_ref[h, 0:rows, :]
                parts = [None] * width
                for g in reversed(range(width)):
                    parts[g] = jnp.exp2(u_ref[h, j0 + g, 0:rows, :] - off).astype(BF16)
                    off = off + t_ref[h, j0 + g, 0:rows, :]
                a_all = parts[0] if width == 1 else jnp.concatenate(parts, axis=1)
                acc_ref[h, 0:rows, :] += _dot(a_all, vt)
                run_ref[h, 0:rows, :] = off

        def stage_sums(j0, width):
            for h in range(2):
                hl = hl_ref[h, pl.ds(j0, width), 0:rows, :].reshape(width * rows, 2 * KEY_TILE)
                cs = _dot(hl, later)
                for g in range(width):
                    blk = cs[g * rows:(g + 1) * rows]
                    u_ref[h, j0 + g, 0:rows, :] = u_ref[h, j0 + g, 0:rows, :] - blk[:, 0:KEY_TILE]
                    t_ref[h, j0 + g, 0:rows, :] = blk[:, KEY_TILE:2 * KEY_TILE]

        nfull = jd // SB_GROUP
        dgrp = nfull * SB_GROUP

        def scores_body(g, carry):
            stage_scores(SB_GROUP * g, SB_GROUP, False)
            return carry

        lax.fori_loop(0, nfull, scores_body, 0)

        def sums_body(g, carry):
            stage_sums(SB_GROUP * g, SB_GROUP)
            return carry

        lax.fori_loop(0, nfull, sums_body, 0)
        jlast = (q0 + rows - 1) >> LOG2_KEY_TILE
        for w in range(1, SB_GROUP + 1):
            @pl.when(jlast - dgrp == w - 1)
            def _():
                stage_scores(dgrp, w, True)
                stage_sums(dgrp, w)
                stage_weights(dgrp, w)

        def weights_body(g, carry):
            stage_weights(SB_GROUP * (nfull - 1 - g), SB_GROUP)
            return carry

        lax.fori_loop(0, nfull, weights_body, 0)

        o_ref[0, 0:rows, :] = jnp.where(lane < DH_C, acc_ref[0, 0:rows, :], acc_ref[1, 0:rows, :]).astype(BF16)

    tail = lq % tq
    if tail == 0:
        process(tq)
    else:
        @pl.when(qi < nq - 1)
        def _():
            process(tq)

        @pl.when(qi == nq - 1)
        def _():
            process(tail)


def _stick_breaking(q, k, v, *, tq, q_pos0):
    bsz, lq, _ = q.shape
    lk = k.shape[2]
    nq = -(-lq // tq)
    nkt = SB_GROUP * (-(-lk // (SB_GROUP * KEY_TILE)))
    kern = functools.partial(_sb_kernel, tq=tq, lq=lq, lk=lk, q_pos0=q_pos0)
    return pl.pallas_call(
        kern,
        grid=(bsz, H_C // 2, nq),
        in_specs=[
            pl.BlockSpec((1, tq, 2 * DH_C), lambda b, h, i: (b, i, h)),
            pl.BlockSpec((1, 2, lk, DH_C), lambda b, h, i: (b, h, 0, 0)),
            pl.BlockSpec((1, 2, lk, DH_C), lambda b, h, i: (b, h, 0, 0)),
        ],
        out_specs=pl.BlockSpec((1, tq, 2 * DH_C), lambda b, h, i: (b, i, h)),
        out_shape=jax.ShapeDtypeStruct((bsz, lq, D_MODEL), BF16),
        scratch_shapes=[
            pltpu.VMEM((nkt * KEY_TILE, 2 * DH_C), BF16),
            pltpu.VMEM((nkt * KEY_TILE, 2 * DH_C), BF16),
            pltpu.VMEM((2, tq, 2 * DH_C), F32),
            pltpu.VMEM((2, tq, KEY_TILE), F32),
            pltpu.VMEM((2, nkt, tq, KEY_TILE), F32),
            pltpu.VMEM((2, nkt, tq, KEY_TILE), F32),
            pltpu.VMEM((2, nkt, tq, 2 * KEY_TILE), BF16),
        ],
        compiler_params=_params(("parallel", "parallel", "arbitrary")),
        name="stick_breaking",
    )(q, k, v)


def _layer_norm(z, g, b):
    mu = jnp.mean(z, axis=1, keepdims=True)
    zc = z - mu
    var = jnp.mean(zc * zc, axis=1, keepdims=True)
    return zc * lax.rsqrt(var + LN_EPS) * g + b


def _outproj_kernel(*refs, n_in):
    mix_refs = refs[:n_in]
    (w_ref, x_ref, g_ref, b_ref, wr_ref, br_ref,
     x1_ref, x1b_ref, ti_ref, tg_ref, rk_ref, cnt_ref, tri_ref) = refs[n_in:]
    i = pl.program_id(0)
    tm = x_ref.shape[0]

    @pl.when(i == 0)
    def _():
        cnt_ref[...] = jnp.zeros_like(cnt_ref)
        tri_ref[...] = (lax.broadcasted_iota(jnp.int32, (tm, tm), 0)
                        < lax.broadcasted_iota(jnp.int32, (tm, tm), 1)).astype(BF16)

    y = None
    c0 = 0
    for r in mix_refs:
        wd = r.shape[1]
        part = _dot(r[...], w_ref[c0:c0 + wd, :])
        y = part if y is None else y + part
        c0 += wd
    x1 = _layer_norm(DN_ALPHA * x_ref[...] + y, g_ref[...], b_ref[...])
    x1_ref[...] = x1
    x1b_ref[...] = x1.astype(BF16)
    logits = _dot_nt(wr_ref[...], x1, precision=lax.Precision.HIGHEST) + br_ref[...]
    ei = lax.broadcasted_iota(jnp.int32, (N_EXPERTS, tm), 0)
    cur = logits
    vals, idxs = [], []
    for _k in range(TOP_K):
        mx = jnp.max(cur, axis=0, keepdims=True)
        ix = jnp.min(jnp.where(cur == mx, ei, N_EXPERTS), axis=0, keepdims=True)
        vals.append(mx)
        idxs.append(ix)
        cur = jnp.where(ei == ix, -jnp.inf, cur)
    es = [jnp.exp(vv - vals[0]) for vv in vals]
    tot = es[0] + es[1] + es[2] + es[3]
    ti_ref[...] = jnp.concatenate(idxs, axis=0)
    tg_ref[...] = jnp.concatenate([e / tot for e in es], axis=0)
    run = cnt_ref[...]
    ranks = []
    for kk in range(TOP_K):
        oh = (ei == idxs[kk])
        before = _dot(oh.astype(BF16), tri_ref[...])
        ranks.append(jnp.sum(jnp.where(oh, before + run, 0.0), axis=0, keepdims=True))
        run = run + jnp.sum(oh.astype(F32), axis=1, keepdims=True)
    rk_ref[...] = jnp.concatenate(ranks, axis=0).astype(jnp.int32)
    cnt_ref[...] = run


def _outproj_ln_router(mixes, w, x_all, g, b, wr_t, br, tile0, ntiles):
    tm = TOKEN_TILE
    n = ntiles * tm
    kern = functools.partial(_outproj_kernel, n_in=len(mixes))
    row_in = lambda wd: pl.BlockSpec((tm, wd), lambda i: (tile0 + i, 0))
    row = lambda wd: pl.BlockSpec((tm, wd), lambda i: (i, 0))
    full = lambda *shape: pl.BlockSpec(shape, lambda i: (0,) * len(shape))
    lanes = lambda rows: pl.BlockSpec((rows, tm), lambda i: (0, i))
    return pl.pallas_call(
        kern,
        grid=(ntiles,),
        in_specs=[row_in(m.shape[1]) for m in mixes] + [
            full(D_MODEL, D_MODEL), row_in(D_MODEL), full(1, D_MODEL), full(1, D_MODEL),
            full(N_EXPERTS, D_MODEL), full(N_EXPERTS, 1)],
        out_specs=[row(D_MODEL), row(D_MODEL), lanes(TOP_K), lanes(TOP_K), lanes(TOP_K), full(N_EXPERTS, 1)],
        out_shape=[
            jax.ShapeDtypeStruct((n, D_MODEL), F32),
            jax.ShapeDtypeStruct((n, D_MODEL), BF16),
            jax.ShapeDtypeStruct((TOP_K, n), jnp.int32),
            jax.ShapeDtypeStruct((TOP_K, n), F32),
            jax.ShapeDtypeStruct((TOP_K, n), jnp.int32),
            jax.ShapeDtypeStruct((N_EXPERTS, 1), F32),
        ],
        scratch_shapes=[pltpu.VMEM((tm, tm), BF16)],
        compiler_params=_params(("arbitrary",)),
        name="outproj_ln_router",
    )(*mixes, w, x_all, g, b, wr_t, br)


def _moe_kernel(be_ref, nu_ref, x_ref, wgu_ref, bgu_ref, wdn_ref, bdn_ref, o_ref):
    @pl.when(pl.program_id(0) < nu_ref[0])
    def _():
        h = _dot(x_ref[...], wgu_ref[0]) + bgu_ref[0]
        glu = jnp.minimum(h[:, 0:D_EXPERT], SWIGLU_LIMIT)
        lin = jnp.clip(h[:, D_EXPERT:2 * D_EXPERT], -SWIGLU_LIMIT, SWIGLU_LIMIT)
        act = glu * _sigmoid(SWIGLU_ALPHA * glu) * (lin + 1.0)
        o_ref[...] = (_dot(act.astype(BF16), wdn_ref[0]) + bdn_ref[0]).astype(BF16)


def _moe_experts(blk_e, n_used, xg, wgu, bgu, wdn, bdn):
    nb = blk_e.shape[0]
    grid_spec = pltpu.PrefetchScalarGridSpec(
        num_scalar_prefetch=2,
        grid=(nb,),
        in_specs=[
            pl.BlockSpec((MOE_BLOCK, D_MODEL), lambda i, be, nu: (i, 0)),
            pl.BlockSpec((1, D_MODEL, 2 * D_EXPERT), lambda i, be, nu: (be[i], 0, 0)),
            pl.BlockSpec((1, 1, 2 * D_EXPERT), lambda i, be, nu: (be[i], 0, 0)),
            pl.BlockSpec((1, D_EXPERT, D_MODEL), lambda i, be, nu: (be[i], 0, 0)),
            pl.BlockSpec((1, 1, D_MODEL), lambda i, be, nu: (be[i], 0, 0)),
        ],
        out_specs=pl.BlockSpec((MOE_BLOCK, D_MODEL), lambda i, be, nu: (i, 0)),
    )
    return pl.pallas_call(
        _moe_kernel,
        grid_spec=grid_spec,
        out_shape=jax.ShapeDtypeStruct((nb * MOE_BLOCK, D_MODEL), BF16),
        compiler_params=_params(("arbitrary",)),
        name="moe_experts",
    )(blk_e, n_used, xg, wgu, bgu, wdn, bdn)


def _deinterleave_kernel(w_ref, p_ref, o_ref):
    o_ref[0] = _dot(w_ref[0].astype(BF16), p_ref[...]).astype(BF16)


def _deinterleave_gu(w):
    n, d, f2 = w.shape
    col = jnp.arange(f2, dtype=jnp.int32)
    src = jnp.where(col < f2 // 2, 2 * col, 2 * (col - f2 // 2) + 1)
    perm = (col[:, None] == src[None, :]).astype(BF16)
    rows = 512
    return pl.pallas_call(
        _deinterleave_kernel,
        grid=(n, d // rows),
        in_specs=[
            pl.BlockSpec((1, rows, f2), lambda e, r: (e, r, 0)),
            pl.BlockSpec((f2, f2), lambda e, r: (0, 0)),
        ],
        out_specs=pl.BlockSpec((1, rows, f2), lambda e, r: (e, r, 0)),
        out_shape=jax.ShapeDtypeStruct((n, d, f2), BF16),
        compiler_params=_params(("parallel", "parallel")),
        name="deinterleave_gu",
    )(w, perm)


def _combine_ln_kernel(x_ref, r_ref, tg_ref, g_ref, b_ref, o_ref):
    tg = tg_ref[...]
    f = r_ref[0].astype(F32) * tg[:, 0:1]
    for kk in range(1, TOP_K):
        f = f + r_ref[kk].astype(F32) * tg[:, kk:kk + 1]
    o_ref[...] = _layer_norm(DN_ALPHA * x_ref[...] + f, g_ref[...], b_ref[...])


def _combine_ln(x1, rows, gates, g, b):
    n = x1.shape[0]
    tm = TOKEN_TILE
    row = pl.BlockSpec((tm, D_MODEL), lambda i: (i, 0))
    vec = pl.BlockSpec((1, D_MODEL), lambda i: (0, 0))
    return pl.pallas_call(
        _combine_ln_kernel,
        grid=(n // tm,),
        in_specs=[row, pl.BlockSpec((TOP_K, tm, D_MODEL), lambda i: (0, i, 0)),
                  pl.BlockSpec((tm, TOP_K), lambda i: (i, 0)), vec, vec],
        out_specs=row,
        out_shape=jax.ShapeDtypeStruct((n, D_MODEL), F32),
        compiler_params=_params(("parallel",)),
        name="combine_ln",
    )(x1, rows, gates, g, b)


def _moe_layer(x1, x1b, top_i, top_g, rank, sizes, e0, wgu, bgu, wdn, bdn, g, b):
    n = x1.shape[0]
    nb = -(-(n * TOP_K) // MOE_BLOCK) + N_EXPERTS
    sizes = sizes[:, 0].astype(jnp.int32)
    nblk = (sizes + MOE_BLOCK - 1) // MOE_BLOCK
    blk_end = jnp.cumsum(nblk)
    pad_starts = (blk_end - nblk) * MOE_BLOCK
    n_used = blk_end[-1]
    blk_ids = jnp.minimum(jnp.arange(nb, dtype=jnp.int32), n_used - 1)
    blk_e = jnp.sum((blk_ids[:, None] >= blk_end[None, :]).astype(jnp.int32), axis=1)
    blk_e = jnp.clip(blk_e, 0, N_EXPERTS - 1)
    experts = jnp.arange(N_EXPERTS, dtype=jnp.int32)
    dest = jnp.sum(jnp.where(top_i[:, :, None] == experts, pad_starts, 0), axis=2) + rank
    tok = jnp.broadcast_to(jnp.arange(n, dtype=jnp.int32)[None], (TOP_K, n))
    tok_of_row = jnp.zeros((nb * MOE_BLOCK,), jnp.int32).at[dest.reshape(-1)].set(
        tok.reshape(-1), unique_indices=True, indices_are_sorted=False)
    xg = x1b.at[tok_of_row].get(mode="promise_in_bounds")
    out = _moe_experts(blk_e + e0, n_used.reshape(1), xg, wgu, bgu, wdn, bdn)
    rows = out.at[dest.reshape(-1)].get(mode="promise_in_bounds").reshape(TOP_K, n, D_MODEL)
    return _combine_ln(x1, rows, jnp.transpose(top_g), g, b)


def _rel_bucket(rel):
    half = N_BUCKETS // 2
    exact = half // 2
    ret = jnp.where(rel > 0, half, 0)
    n = jnp.abs(rel)
    large = exact + (jnp.log(jnp.maximum(n, 1).astype(F32) / exact)
                     / math.log(MAX_DISTANCE / exact) * (half - exact)).astype(jnp.int32)
    large = jnp.minimum(large, half - 1)
    return ret + jnp.where(n < exact, n, large)


def _near_bias_tiles(rel_table, tq, q_pos0):
    base = q_pos0 % KEY_TILE
    span = max(1, tq // KEY_TILE)
    i = jnp.arange(tq, dtype=jnp.int32)[:, None] + base
    j = jnp.arange(KEY_TILE, dtype=jnp.int32)[None, :]
    tiles = [jnp.full((tq, KEY_TILE), -2 * MAX_DISTANCE, jnp.int32)]
    for d in range(-1, span + 1):
        tiles.append(d * KEY_TILE + j - i)
    rel = jnp.stack(tiles)
    return jnp.transpose(rel_table[_rel_bucket(rel)], (3, 0, 1, 2)).astype(F32) * LOG2_E


def kernel(x_prompt, x_sample, cache_diff_k, cache_diff_v, state_mlstm_C, state_mlstm_n, state_mlstm_m, state_mlstm_conv, cache_sb_k, cache_sb_v, meta_tokens, rel_bias, w_in_ab, w_out_ab, conv_w_a, conv_b_a, w_aq_a, w_ak_a, b_if_a, mh_gain_a, lam_q1, lam_k1, lam_q2, lam_k2, subln_gain_b, w_in_c, w_out_c, ln_g, ln_b, w_router, b_router, w_gu, b_gu, w_down, b_down):
    bp, sp, _ = x_prompt.shape
    bs, ss, _ = x_sample.shape
    lp = N_META + sp
    past = cache_diff_k.shape[3]
    n_p = bp * lp
    n_s = bs * ss
    tm_p = lp // 3 if (lp % 3 == 0 and (lp // 3) % 16 == 0) else lp
    tq_p = 2 * KEY_TILE

    x0 = jnp.concatenate([jnp.broadcast_to(meta_tokens[None], (bp, N_META, D_MODEL)).astype(x_prompt.dtype),
                          x_prompt], axis=1)
    x_all = jnp.concatenate([x0.reshape(n_p, D_MODEL), x_sample.reshape(n_s, D_MODEL)], axis=0)

    j = 0
    lam_init = 0.8 - 0.6 * math.exp(-0.3 * 0)
    w_ab = w_in_ab[j]
    w_perm = jnp.concatenate([
        w_ab[:, 0:OFF_AIF],
        jnp.pad(w_ab[:, OFF_AIF:OFF_BQ], ((0, 0), (0, LANE - 2 * H_A))),
        w_ab[:, OFF_BQ:OFF_BQ + W_B] * (DH_B ** -0.5 * LOG2_E),
        w_ab[:, OFF_BQ + W_B:]], axis=1).astype(BF16)
    bg = jnp.pad(b_if_a[j], (0, LANE - 2 * H_A)).reshape(1, LANE)
    lam = (jnp.exp(jnp.sum(lam_q1[j] * lam_k1[j])) - jnp.exp(jnp.sum(lam_q2[j] * lam_k2[j]))).astype(F32) + lam_init
    lam_v = jnp.broadcast_to(lam.reshape(1, 1), (1, LANE))
    cw = conv_w_a[j]
    cb = conv_b_a[j].reshape(1, W_A)
    wq = w_aq_a[j].astype(BF16)
    wk = w_ak_a[j].astype(BF16)
    gain_a = mh_gain_a[j].reshape(1, W_A)
    gain_b = subln_gain_b[j].reshape(1, 2 * DH_B)

    def ab_group(row0, bsz, seq, tm, lead, state, past_kv, tq, q_pos0, chunked):
        a, g, q, k_new, v_new = _inproj_ab(x_all, row0, bsz, seq, tm, w_perm, bg)
        c0, n0, m0, buf = state
        mix_a, c1, n1, m1, cs = _mlstm(
            a, g, bsz, seq, lead, cw, cb, wq, wk, gain_a,
            c0, n0.reshape(bsz, H_A, 1, DH_A),
            jnp.broadcast_to(m0[:, :, None, None], (bsz, H_A, 1, LANE)), buf)
        if past_kv is None:
            k_all, v_all = k_new, v_new
        else:
            k_all = jnp.concatenate([past_kv[0], k_new], axis=2)
            v_all = jnp.concatenate([past_kv[1], v_new], axis=2)
        near = _near_bias_tiles(rel_bias, tq, q_pos0)
        mix_b = _diff_attention(q.reshape(bsz, seq, W_B), k_all, v_all, near, lam_v, gain_b,
                                tq=tq, q_pos0=q_pos0, chunked=chunked, lam_init=lam_init)
        outs = (k_new, v_new, c1, n1.reshape(bsz, H_A, DH_A), m1[:, :, 0, 0], cs)
        return mix_a, mix_b.reshape(bsz * seq, W_B), outs

    zero_state = (jnp.zeros((bp, H_A, DH_A, DH_A), F32), jnp.zeros((bp, H_A, DH_A), F32),
                  jnp.zeros((bp, H_A), F32), jnp.zeros((bp, CONV_W - 1, W_A), F32))
    mix_a_p, mix_b_p, ab_p = ab_group(0, bp, lp, tm_p, N_META, zero_state, None, tq_p, 0, True)
    s_state = (state_mlstm_C[j], state_mlstm_n[j], state_mlstm_m[j], state_mlstm_conv[j])
    mix_a_s, mix_b_s, ab_s = ab_group(n_p, bs, ss, ss, 0, s_state, (cache_diff_k[j], cache_diff_v[j]),
                                      ss, past, False)
    mix_a = jnp.concatenate([mix_a_p, mix_a_s], axis=0)
    mix_b = jnp.concatenate([mix_b_p, mix_b_s], axis=0)

    n_le = w_gu.shape[0] * N_EXPERTS
    wgu = _deinterleave_gu(w_gu.reshape(n_le, D_MODEL, 2 * D_EXPERT))
    bgu = jnp.concatenate([b_gu[..., 0::2], b_gu[..., 1::2]], axis=-1).reshape(n_le, 1, 2 * D_EXPERT)
    wdn = w_down.reshape(n_le, D_EXPERT, D_MODEL).astype(BF16)
    bdn = b_down.reshape(n_le, 1, D_MODEL)

    def token_stage(layer, mixes, w_out, x_in):
        wr_t = jnp.transpose(w_router[layer])
        br = b_router[layer].reshape(N_EXPERTS, 1)
        w_out = w_out.astype(BF16)
        ntile = x_in.shape[0] // TOKEN_TILE
        bounds = [ntile * s // MOE_SPLIT for s in range(MOE_SPLIT + 1)]
        outs = []
        for t0, t1 in zip(bounds[:-1], bounds[1:]):
            x1, x1b, top_i, top_g, rank, sizes = _outproj_ln_router(
                mixes, w_out, x_in, ln_g[layer, 0].reshape(1, D_MODEL), ln_b[layer, 0].reshape(1, D_MODEL),
                wr_t, br, t0, t1 - t0)
            outs.append(_moe_layer(x1, x1b, top_i, top_g, rank, sizes, layer * N_EXPERTS, wgu, bgu, wdn, bdn,
                                   ln_g[layer, 1].reshape(1, D_MODEL), ln_b[layer, 1].reshape(1, D_MODEL)))
        return jnp.concatenate(outs, axis=0)

    x_all = token_stage(0, [mix_a, mix_b], w_out_ab[j], x_all)

    q_scale = jnp.where(jnp.arange(3 * D_MODEL) < D_MODEL, DH_C ** -0.5 * LOG2_E, 1.0).astype(F32)
    w_c = (w_in_c[j] * q_scale).astype(BF16)

    def c_group(row0, bsz, seq, tm, past_kv, tq, q_pos0):
        q, k_new, v_new = _inproj_c(x_all, row0, bsz, seq, tm, w_c)
        if past_kv is None:
            k_all, v_all = k_new, v_new
        else:
            k_all = jnp.concatenate([past_kv[0], k_new], axis=2)
            v_all = jnp.concatenate([past_kv[1], v_new], axis=2)
        o = _stick_breaking(q.reshape(bsz, seq, D_MODEL), k_all, v_all, tq=tq, q_pos0=q_pos0)
        return o.reshape(bsz * seq, D_MODEL), (k_new, v_new)

    mix_p, c_p = c_group(0, bp, lp, tm_p, None, 2 * KEY_TILE, 0)
    mix_s, c_s = c_group(n_p, bs, ss, ss, (cache_sb_k[j], cache_sb_v[j]), ss, past)
    x_all = token_stage(1, [jnp.concatenate([mix_p, mix_s], axis=0)], w_out_c[j], x_all)

    y_prompt = x_all[:n_p].reshape(bp, lp, D_MODEL)[:, N_META:]
    y_sample = x_all[n_p:].reshape(bs, ss, D_MODEL)
    stack = lambda t: t[None]
    return (y_prompt, y_sample,
            stack(ab_p[0]), stack(ab_p[1]), stack(ab_p[2]), stack(ab_p[3]), stack(ab_p[4]), stack(ab_p[5]),
            stack(c_p[0]), stack(c_p[1]),
            stack(ab_s[0]), stack(ab_s[1]), stack(ab_s[2]), stack(ab_s[3]), stack(ab_s[4]), stack(ab_s[5]),
            stack(c_s[0]), stack(c_s[1]))
```

```python
import functools
import math

import jax
import jax.numpy as jnp
from jax import lax
from jax.experimental import pallas as pl
from jax.experimental.pallas import tpu as pltpu

F32 = jnp.float32
BF16 = jnp.bfloat16

D_MODEL = 1024
DEPTH = 2
CHUNK = 64
N_META = 16
H_A = 4
DH_A = 128
W_A = H_A * DH_A
CONV_W = 4
H_B = 4
DH_B = 64
W_B = H_B * 2 * DH_B
H_C = 16
DH_C = 64
N_BUCKETS = 32
MAX_DISTANCE = 128
N_EXPERTS = 32
TOP_K = 4
D_EXPERT = D_MODEL // 2
SWIGLU_LIMIT = 7.0
SWIGLU_ALPHA = 1.702
DN_ALPHA = (2 * DEPTH) ** 0.25
LN_EPS = 1e-5
OFF_AIF = 3 * W_A
OFF_BQ = OFF_AIF + 2 * H_A

LANE = 128
KEY_TILE = 128
LOG2_KEY_TILE = 7
LOG2_CHUNK = 6
VMEM_LIMIT = 56 * 1024 * 1024
MOE_BLOCK = 512
TOKEN_TILE = 512
SB_GROUP = 8
MOE_SPLIT = 2
NEG_BIG = -1e30
LOG2_E = 1.4426950408889634

COL_G = 3 * W_A
COL_Q = COL_G + LANE
COL_K = COL_Q + W_B
COL_V = COL_K + W_B
D_IN_AB_PAD = COL_V + W_B


def _dot(a, b):
    return jnp.dot(a, b, preferred_element_type=F32)


def _dot_nt(a, b, precision=None):
    return lax.dot_general(a, b, (((1,), (1,)), ((), ())), preferred_element_type=F32, precision=precision)


def _dot_tn(a, b):
    return lax.dot_general(a, b, (((0,), (0,)), ((), ())), preferred_element_type=F32)


def _log_sigmoid(x):
    return jnp.minimum(x, 0.0) - jnp.log(1.0 + jnp.exp(-jnp.abs(x)))


def _sigmoid(x):
    return 1.0 / (1.0 + jnp.exp(-x))


def _chunk_id(pos):
    return (pos + (CHUNK - N_META)) >> LOG2_CHUNK


def _params(sem):
    return pltpu.CompilerParams(dimension_semantics=sem, vmem_limit_bytes=VMEM_LIMIT)


def _inproj_ab_kernel(x_ref, w_ref, bg_ref, a_ref, g_ref, q_ref, k_ref, v_ref):
    xb = x_ref[...].astype(BF16)
    a_ref[...] = _dot(xb, w_ref[:, 0:COL_G])
    g_ref[...] = _dot(xb, w_ref[:, COL_G:COL_Q]) + bg_ref[...]
    q_ref[...] = _dot(xb, w_ref[:, COL_Q:COL_K]).astype(BF16)
    for h in range(H_B):
        k_ref[0, h] = _dot(xb, w_ref[:, COL_K + 2 * DH_B * h:COL_K + 2 * DH_B * (h + 1)])
        v_ref[0, h] = _dot(xb, w_ref[:, COL_V + 2 * DH_B * h:COL_V + 2 * DH_B * (h + 1)])


def _inproj_ab(x_all, row0, bsz, seq, tm, w, bg):
    nrt = seq // tm
    n = bsz * seq
    if row0 % tm:
        x_all, row0 = x_all[row0:row0 + n], 0
    off = row0 // tm
    return pl.pallas_call(
        _inproj_ab_kernel,
        grid=(bsz, nrt),
        in_specs=[
            pl.BlockSpec((tm, D_MODEL), lambda b, r: (off + b * nrt + r, 0)),
            pl.BlockSpec((D_MODEL, D_IN_AB_PAD), lambda b, r: (0, 0)),
            pl.BlockSpec((1, LANE), lambda b, r: (0, 0)),
        ],
        out_specs=[
            pl.BlockSpec((tm, COL_G), lambda b, r: (b * nrt + r, 0)),
            pl.BlockSpec((tm, LANE), lambda b, r: (b * nrt + r, 0)),
            pl.BlockSpec((tm, W_B), lambda b, r: (b * nrt + r, 0)),
            pl.BlockSpec((1, H_B, tm, 2 * DH_B), lambda b, r: (b, 0, r, 0)),
            pl.BlockSpec((1, H_B, tm, 2 * DH_B), lambda b, r: (b, 0, r, 0)),
        ],
        out_shape=[
            jax.ShapeDtypeStruct((n, COL_G), F32),
            jax.ShapeDtypeStruct((n, LANE), F32),
            jax.ShapeDtypeStruct((n, W_B), BF16),
            jax.ShapeDtypeStruct((bsz, H_B, seq, 2 * DH_B), F32),
            jax.ShapeDtypeStruct((bsz, H_B, seq, 2 * DH_B), F32),
        ],
        compiler_params=_params(("parallel", "parallel")),
        name="inproj_ab",
    )(x_all, w, bg)


def _mlstm_kernel(a_ref, g_ref, cw_ref, cb_ref, wq_ref, wk_ref, gain_ref, c0_ref, n0_ref, m0_ref, buf_ref,
                  out_ref, c1_ref, n1_ref, m1_ref, cs_ref, q_s, num_s, rs_s, u_s, nv_s, *, seq, lead):
    cs_ref[0] = a_ref[seq - (CONV_W - 1):seq, 0:W_A]
    c1_ref[...] = c0_ref[...]
    n1_ref[...] = n0_ref[...]
    m1_ref[...] = m0_ref[...]
    sel = (lax.broadcasted_iota(jnp.int32, (8, LANE), 0) == lax.broadcasted_iota(jnp.int32, (8, LANE), 1)).astype(F32)
    lane = lax.broadcasted_iota(jnp.int32, (1, LANE), 1)

    def local(c, r0, lc, first):
        if first:
            win = jnp.concatenate([jnp.zeros((5, W_A), F32), buf_ref[0], a_ref[0:lc, 0:W_A]], axis=0)
        else:
            win = a_ref[pl.ds(pl.multiple_of(r0 - 8, 8), lc + 8), 0:W_A]
        y = cb_ref[...]
        for j in range(CONV_W):
            y = y + win[5 + j:5 + j + lc, :] * cw_ref[j:j + 1, :]
        ca = y * _sigmoid(y)
        g = g_ref[pl.ds(r0, lc), :]
        g_rows = _dot_nt(sel, g, precision=lax.Precision.HIGHEST)
        ti = lax.broadcasted_iota(jnp.int32, (lc, lc), 0)
        si = lax.broadcasted_iota(jnp.int32, (lc, lc), 1)
        causal = si <= ti
        rs = jnp.zeros((lc, LANE), F32)
        for h in range(H_A):
            hs = slice(DH_A * h, DH_A * (h + 1))
            ig_c = g[:, h:h + 1]
            lf_c = _log_sigmoid(g[:, H_A + h:H_A + h + 1])
            ig_r = g_rows[h:h + 1, :]
            lf_r = _log_sigmoid(g_rows[H_A + h:H_A + h + 1, :])
            b_c = jnp.sum(jnp.where(causal, lf_r, 0.0), axis=1, keepdims=True)
            b_r = jnp.sum(jnp.where(ti <= si, lf_c, 0.0), axis=0, keepdims=True)
            dmat = jnp.where(causal, b_c - b_r + ig_r, -jnp.inf)
            m_loc = jnp.max(dmat, axis=1, keepdims=True)
            w = jnp.exp(dmat - m_loc)
            cab = ca[:, hs].astype(BF16)
            qb = _dot(cab, wq_ref[h]).astype(BF16)
            k = _dot(cab, wk_ref[h]) * (DH_A ** -0.5)
            vb = a_ref[pl.ds(r0, lc), W_A + DH_A * h:W_A + DH_A * (h + 1)].astype(BF16)
            s = _dot_nt(qb, k.astype(BF16)) * w
            q_s[pl.ds(r0, lc), hs] = qb
            num_s[pl.ds(r0, lc), hs] = _dot(s.astype(BF16), vb)
            den_loc = jnp.sum(s, axis=1, keepdims=True)
            rs = jnp.where(lane == h, den_loc, rs)
            rs = jnp.where(lane == H_A + h, m_loc, rs)
            rs = jnp.where(lane == 2 * H_A + h, b_c, rs)
            w_end = jnp.exp(b_c[lc - 1:lc, :] - b_c + ig_c - m_loc[lc - 1:lc, :])
            kw = k * w_end
            u_s[c, h] = _dot_tn(kw.astype(BF16), vb)
            nv_s[c, h] = jnp.sum(kw, axis=0, keepdims=True)
        rs_s[pl.ds(r0, lc), :] = rs

    def carry(c, r0, lc):
        rs = rs_s[pl.ds(r0, lc), :]
        for h in range(H_A):
            hs = slice(DH_A * h, DH_A * (h + 1))
            den_loc = jnp.broadcast_to(rs[:, h:h + 1], (lc, DH_A))
            m_loc = jnp.broadcast_to(rs[:, H_A + h:H_A + h + 1], (lc, DH_A))
            b_c = jnp.broadcast_to(rs[:, 2 * H_A + h:2 * H_A + h + 1], (lc, DH_A))
            m_prev = m1_ref[0, h]
            c_prev = c1_ref[0, h]
            n_prev = n1_ref[0, h]
            inter = b_c + m_prev
            m_t = jnp.maximum(inter, m_loc)
            gg = jnp.exp(inter - m_t)
            sc = jnp.exp(m_loc - m_t)
            qb = q_s[pl.ds(r0, lc), hs]
            num = sc * num_s[pl.ds(r0, lc), hs] + gg * _dot(qb, c_prev.astype(BF16))
            qn = jnp.broadcast_to(jnp.sum(qb.astype(F32) * n_prev, axis=1, keepdims=True), (lc, DH_A))
            den = sc * den_loc + gg * qn
            hh = num / jnp.maximum(jnp.abs(den), jnp.exp(-m_t))
            m_new = m_t[lc - 1:lc, :]
            decay = jnp.exp(inter[lc - 1:lc, :] - m_new)
            grow = sc[lc - 1:lc, :]
            c1_ref[0, h] = decay * c_prev + grow * u_s[c, h]
            n1_ref[0, h] = decay * n_prev + grow * nv_s[c, h]
            m1_ref[0, h] = m_new
            hn = hh * lax.rsqrt(jnp.mean(hh * hh, axis=1, keepdims=True) + LN_EPS) * gain_ref[:, hs]
            oa = a_ref[pl.ds(r0, lc), 2 * W_A + DH_A * h:2 * W_A + DH_A * (h + 1)]
            out_ref[pl.ds(r0, lc), hs] = (hn * _sigmoid(oa)).astype(BF16)

    first_len = lead if lead else CHUNK
    nrest = (seq - first_len) // CHUNK
    start = lambda i: pl.multiple_of(first_len + i * CHUNK, 16)
    local(0, 0, first_len, True)

    def local_body(i, c):
        local(2 * i + 1, start(2 * i), CHUNK, False)
        local(2 * i + 2, start(2 * i + 1), CHUNK, False)
        return c

    lax.fori_loop(0, nrest // 2, local_body, 0)
    if nrest % 2:
        local(nrest, start(nrest - 1), CHUNK, False)
    carry(0, 0, first_len)

    def carry_body(i, c):
        carry(i + 1, start(i), CHUNK)
        return c

    lax.fori_loop(0, nrest, carry_body, 0)


def _mlstm(a, g, bsz, seq, lead, cw, cb, wq, wk, gain, c0, n0, m0, buf):
    kern = functools.partial(_mlstm_kernel, seq=seq, lead=lead)
    assert seq >= CONV_W - 1
    nchunks = 1 + (seq - (lead if lead else CHUNK)) // CHUNK
    full = lambda *shape: pl.BlockSpec(shape, lambda b: (0,) * len(shape))
    per_b = lambda *shape: pl.BlockSpec((1,) + shape, lambda b: (b,) + (0,) * len(shape))
    return pl.pallas_call(
        kern,
        grid=(bsz,),
        in_specs=[
            pl.BlockSpec((seq, COL_G), lambda b: (b, 0)),
            pl.BlockSpec((seq, LANE), lambda b: (b, 0)),
            full(CONV_W, W_A), full(1, W_A), full(H_A, DH_A, DH_A), full(H_A, DH_A, DH_A), full(1, W_A),
            per_b(H_A, DH_A, DH_A), per_b(H_A, 1, DH_A), per_b(H_A, 1, LANE), per_b(CONV_W - 1, W_A),
        ],
        out_specs=[
            pl.BlockSpec((seq, W_A), lambda b: (b, 0)),
            per_b(H_A, DH_A, DH_A), per_b(H_A, 1, DH_A), per_b(H_A, 1, LANE), per_b(CONV_W - 1, W_A),
        ],
        out_shape=[
            jax.ShapeDtypeStruct((bsz * seq, W_A), BF16),
            jax.ShapeDtypeStruct((bsz, H_A, DH_A, DH_A), F32),
            jax.ShapeDtypeStruct((bsz, H_A, 1, DH_A), F32),
            jax.ShapeDtypeStruct((bsz, H_A, 1, LANE), F32),
            jax.ShapeDtypeStruct((bsz, CONV_W - 1, W_A), F32),
        ],
        scratch_shapes=[
            pltpu.VMEM((seq, W_A), BF16),
            pltpu.VMEM((seq, W_A), F32),
            pltpu.VMEM((seq, LANE), F32),
            pltpu.VMEM((nchunks, H_A, DH_A, DH_A), F32),
            pltpu.VMEM((nchunks, H_A, 1, DH_A), F32),
        ],
        compiler_params=_params(("parallel",)),
        name="mlstm",
    )(a, g, cw, cb, wq, wk, gain, c0, n0, m0, buf)


def _diff_kernel(lam_ref, q_ref, k_ref, v_ref, near_ref, gain_ref, o_ref, kb_ref, vb_ref, s_ref, mx_ref, l_ref,
                 acc_ref, *, tq, lq, lk, q_pos0, chunked, lam_init):
    qi = pl.program_id(2)
    nq = pl.num_programs(2)
    rows_k = kb_ref.shape[0]

    @pl.when(qi == 0)
    def _():
        kb_ref[0:lk, :] = k_ref[0, 0].astype(BF16)
        vb_ref[0:lk, :] = v_ref[0, 0].astype(BF16)
        kb_ref[lk:, :] = jnp.zeros((rows_k - lk, 2 * DH_B), BF16)
        vb_ref[lk:, :] = jnp.zeros((rows_k - lk, 2 * DH_B), BF16)

    q0 = q_pos0 + qi * tq
    qt = q0 >> LOG2_KEY_TILE
    lane = lax.broadcasted_iota(jnp.int32, (1, 2 * DH_B), 1)
    bias_far = near_ref[0, 0, 0:1, 0:1]
    span = near_ref.shape[1] - 3

    def process(rows):
        r2 = 2 * rows
        qf = q_ref[0, 0:rows, :].astype(F32)
        qs = jnp.concatenate([jnp.where(lane < DH_B, qf, 0.0), jnp.where(lane >= DH_B, qf, 0.0)],
                             axis=0).astype(BF16)
        mx_ref[0:r2, :] = jnp.full((r2, KEY_TILE), NEG_BIG, F32)

        def step(j0, width, mode):
            ks = pl.multiple_of(j0 * KEY_TILE, KEY_TILE)
            kt = kb_ref[pl.ds(ks, width * KEY_TILE), :]
            s = _dot_nt(qs, kt)
            if mode == "far":
                s = s + bias_far
            else:
                first = span + 3 - width
                bias = jnp.concatenate([near_ref[0, first + g, 0:rows, :] for g in range(width)], axis=1)
                kpos = ks + lax.broadcasted_iota(jnp.int32, (rows, width * KEY_TILE), 1)
                if chunked:
                    qpos = q0 + lax.broadcasted_iota(jnp.int32, (rows, width * KEY_TILE), 0)
                    mask = _chunk_id(kpos) <= _chunk_id(qpos)
                else:
                    mask = kpos < lk
                s = jnp.where(jnp.concatenate([mask, mask], axis=0),
                              s + jnp.concatenate([bias, bias], axis=0), NEG_BIG)
            mx = mx_ref[0:r2, :]
            for g in range(width):
                blk = s[:, g * KEY_TILE:(g + 1) * KEY_TILE]
                s_ref[j0 + g, 0:r2, :] = blk
                mx = jnp.maximum(mx, blk)
            mx_ref[0:r2, :] = mx

        @pl.when(qt >= 1)
        def _():
            step(qt - 1, span + 2, "near")

        @pl.when(qt == 0)
        def _():
            step(0, span + 1, "near")

        def for_tiles(count, fn):
            def body(g, carry):
                fn(8 * g, 8)
                return carry

            lax.fori_loop(0, count >> 3, body, 0)
            base = (count >> 3) << 3
            for w, done in ((4, 0), (2, 4), (1, 6)):
                @pl.when((count & w) != 0)
                def _():
                    fn(base + (count & done), w)

        for_tiles(jnp.maximum(qt - 1, 0), lambda j0, w: step(j0, w, "far"))
        m = jnp.max(mx_ref[0:r2, :], axis=1, keepdims=True)
        l_ref[0:r2, :] = jnp.zeros((r2, KEY_TILE), F32)
        acc_ref[0:r2, :] = jnp.zeros((r2, 2 * DH_B), F32)

        def weights(j0, width):
            vt = vb_ref[pl.ds(pl.multiple_of(j0 * KEY_TILE, KEY_TILE), width * KEY_TILE), :]
            lsum = l_ref[0:r2, :]
            ps = []
            for t in range(width):
                p = jnp.exp2(s_ref[j0 + t, 0:r2, :] - m)
                lsum = lsum + p
                ps.append(p.astype(BF16))
            l_ref[0:r2, :] = lsum
            acc_ref[0:r2, :] += _dot(ps[0] if width == 1 else jnp.concatenate(ps, axis=1), vt)

        for_tiles(qt + span + 1, weights)
        o = acc_ref[0:r2, :] / jnp.sum(l_ref[0:r2, :], axis=1, keepdims=True)
        lam = lam_ref[:, 0:1]
        o = o[0:rows] - lam * o[rows:2 * rows]
        o = o * lax.rsqrt(jnp.mean(o * o, axis=1, keepdims=True) + LN_EPS) * gain_ref[...] * (1.0 - lam_init)
        o_ref[0, 0:rows, :] = o.astype(BF16)

    tail = lq % tq
    if tail == 0:
        process(tq)
    else:
        @pl.when(qi < nq - 1)
        def _():
            process(tq)

        @pl.when(qi == nq - 1)
        def _():
            process(tail)


def _diff_attention(q, k, v, near, lam, gain, *, tq, q_pos0, chunked, lam_init):
    bsz, lq, _ = q.shape
    lk = k.shape[2]
    nq = -(-lq // tq)
    nkt = -(-lk // KEY_TILE)
    span = near.shape[1] - 3
    kern = functools.partial(_diff_kernel, tq=tq, lq=lq, lk=lk, q_pos0=q_pos0, chunked=chunked, lam_init=lam_init)
    return pl.pallas_call(
        kern,
        grid=(bsz, H_B, nq),
        in_specs=[
            pl.BlockSpec((1, LANE), lambda b, h, i: (0, 0)),
            pl.BlockSpec((1, tq, 2 * DH_B), lambda b, h, i: (b, i, h)),
            pl.BlockSpec((1, 1, lk, 2 * DH_B), lambda b, h, i: (b, h, 0, 0)),
            pl.BlockSpec((1, 1, lk, 2 * DH_B), lambda b, h, i: (b, h, 0, 0)),
            pl.BlockSpec((1, span + 3, tq, KEY_TILE), lambda b, h, i: (h, 0, 0, 0)),
            pl.BlockSpec((1, 2 * DH_B), lambda b, h, i: (0, 0)),
        ],
        out_specs=pl.BlockSpec((1, tq, 2 * DH_B), lambda b, h, i: (b, i, h)),
        out_shape=jax.ShapeDtypeStruct((bsz, lq, W_B), BF16),
        scratch_shapes=[
            pltpu.VMEM(((nkt + span) * KEY_TILE, 2 * DH_B), BF16),
            pltpu.VMEM(((nkt + span) * KEY_TILE, 2 * DH_B), BF16),
            pltpu.VMEM((nkt + span, 2 * tq, KEY_TILE), F32),
            pltpu.VMEM((2 * tq, KEY_TILE), F32),
            pltpu.VMEM((2 * tq, KEY_TILE), F32),
            pltpu.VMEM((2 * tq, 2 * DH_B), F32),
        ],
        compiler_params=_params(("parallel", "parallel", "arbitrary")),
        name="diff_attention",
    )(lam, q, k, v, near, gain)


def _inproj_c_kernel(x_ref, w_ref, q_ref, k_ref, v_ref):
    xb = x_ref[...].astype(BF16)
    q_ref[...] = _dot(xb, w_ref[:, 0:D_MODEL]).astype(BF16)
    yk = _dot(xb, w_ref[:, D_MODEL:2 * D_MODEL])
    for h in range(H_C):
        k_ref[0, h] = yk[:, DH_C * h:DH_C * (h + 1)]
    yv = _dot(xb, w_ref[:, 2 * D_MODEL:3 * D_MODEL])
    for h in range(H_C):
        v_ref[0, h] = yv[:, DH_C * h:DH_C * (h + 1)]


def _inproj_c(x_all, row0, bsz, seq, tm, w):
    nrt = seq // tm
    if row0 % tm:
        x_all, row0 = x_all[row0:row0 + bsz * seq], 0
    off = row0 // tm
    return pl.pallas_call(
        _inproj_c_kernel,
        grid=(bsz, nrt),
        in_specs=[
            pl.BlockSpec((tm, D_MODEL), lambda b, r: (off + b * nrt + r, 0)),
            pl.BlockSpec((D_MODEL, 3 * D_MODEL), lambda b, r: (0, 0)),
        ],
        out_specs=[
            pl.BlockSpec((tm, D_MODEL), lambda b, r: (b * nrt + r, 0)),
            pl.BlockSpec((1, H_C, tm, DH_C), lambda b, r: (b, 0, r, 0)),
            pl.BlockSpec((1, H_C, tm, DH_C), lambda b, r: (b, 0, r, 0)),
        ],
        out_shape=[
            jax.ShapeDtypeStruct((bsz * seq, D_MODEL), BF16),
            jax.ShapeDtypeStruct((bsz, H_C, seq, DH_C), F32),
            jax.ShapeDtypeStruct((bsz, H_C, seq, DH_C), F32),
        ],
        compiler_params=_params(("parallel", "parallel")),
        name="inproj_c",
    )(x_all, w)


def _sb_kernel(q_ref, k_ref, v_ref, o_ref, kb_ref, vb_ref, acc_ref, run_ref, u_ref, t_ref, hl_ref,
               *, tq, lq, lk, q_pos0):
    qi = pl.program_id(2)
    nq = pl.num_programs(2)
    rows_k = kb_ref.shape[0]

    @pl.when(qi == 0)
    def _():
        kb_ref[0:lk, :] = jnp.concatenate([k_ref[0, 0], k_ref[0, 1]], axis=1).astype(BF16)
        vb_ref[0:lk, :] = jnp.concatenate([v_ref[0, 0], v_ref[0, 1]], axis=1).astype(BF16)
        if rows_k > lk:
            kb_ref[lk:, :] = jnp.zeros((rows_k - lk, 2 * DH_C), BF16)
            vb_ref[lk:, :] = jnp.zeros((rows_k - lk, 2 * DH_C), BF16)

    q0 = q_pos0 + qi * tq
    jd = q0 >> LOG2_KEY_TILE
    jj = lax.broadcasted_iota(jnp.int32, (2 * KEY_TILE, 2 * KEY_TILE), 0) & (KEY_TILE - 1)
    ss = lax.broadcasted_iota(jnp.int32, (2 * KEY_TILE, 2 * KEY_TILE), 1)
    later = jnp.where((jj > ss) | (ss >= KEY_TILE), 1.0, 0.0).astype(BF16)
    lane = lax.broadcasted_iota(jnp.int32, (1, 2 * DH_C), 1)

    def process(rows):
        qf = q_ref[0, 0:rows, :].astype(F32)
        qh = [jnp.where(lane < DH_C, qf, 0.0).astype(BF16), jnp.where(lane >= DH_C, qf, 0.0).astype(BF16)]
        acc_ref[:, 0:rows, :] = jnp.zeros((2, rows, 2 * DH_C), F32)
        run_ref[:, 0:rows, :] = jnp.zeros((2, rows, KEY_TILE), F32)

        def stage_scores(j0, width, masked):
            ks = pl.multiple_of(j0 * KEY_TILE, KEY_TILE)
            kt = kb_ref[pl.ds(ks, width * KEY_TILE), :]
            for h in range(2):
                z = _dot_nt(qh[h], kt)
                sp = jnp.maximum(z, 0.0) + jnp.log(1.0 + jnp.exp2(-jnp.abs(z))) * LOG2_E
                u = z - sp
                if masked:
                    mask = (ks + lax.broadcasted_iota(jnp.int32, (rows, width * KEY_TILE), 1)) < (
                        q0 + lax.broadcasted_iota(jnp.int32, (rows, width * KEY_TILE), 0))
                    sp = jnp.where(mask, sp, 0.0)
                    u = jnp.where(mask, u, NEG_BIG)
                hi = sp.astype(BF16)
                lo = (sp - hi.astype(F32)).astype(BF16)
                for g in range(width):
                    cols = slice(g * KEY_TILE, (g + 1) * KEY_TILE)
                    u_ref[h, j0 + g, 0:rows, :] = u[:, cols]
                    hl_ref[h, j0 + g, 0:rows, :] = jnp.concatenate([hi[:, cols], lo[:, cols]], axis=1)

        def stage_weights(j0, width):
            ks = pl.multiple_of(j0 * KEY_TILE, KEY_TILE)
            vt = vb_ref[pl.ds(ks, width * KEY_TILE), :]
            for h in range(2):
                off = run_ref[h, 0:rows, :]
                parts = [None] * width
                for g in reversed(range(width)):
                    parts[g] = jnp.exp2(u_ref[h, j0 + g, 0:rows, :] - off).astype(BF16)
                    off = off + t_ref[h, j0 + g, 0:rows, :]
                a_all = parts[0] if width == 1 else jnp.concatenate(parts, axis=1)
                acc_ref[h, 0:rows, :] += _dot(a_all, vt)
                run_ref[h, 0:rows, :] = off

        def stage_sums(j0, width):
            for h in range(2):
                hl = hl_ref[h, pl.ds(j0, width), 0:rows, :].reshape(width * rows, 2 * KEY_TILE)
                cs = _dot(hl, later)
                for g in range(width):
                    blk = cs[g * rows:(g + 1) * rows]
                    u_ref[h, j0 + g, 0:rows, :] = u_ref[h, j0 + g, 0:rows, :] - blk[:, 0:KEY_TILE]
                    t_ref[h, j0 + g, 0:rows, :] = blk[:, KEY_TILE:2 * KEY_TILE]

        nfull = jd // SB_GROUP
        dgrp = nfull * SB_GROUP

        def scores_body(g, carry):
            stage_scores(SB_GROUP * g, SB_GROUP, False)
            return carry

        lax.fori_loop(0, nfull, scores_body, 0)

        def sums_body(g, carry):
            stage_sums(SB_GROUP * g, SB_GROUP)
            return carry

        lax.fori_loop(0, nfull, sums_body, 0)
        jlast = (q0 + rows - 1) >> LOG2_KEY_TILE
        for w in range(1, SB_GROUP + 1):
            @pl.when(jlast - dgrp == w - 1)
            def _():
                stage_scores(dgrp, w, True)
                stage_sums(dgrp, w)
                stage_weights(dgrp, w)

        def weights_body(g, carry):
            stage_weights(SB_GROUP * (nfull - 1 - g), SB_GROUP)
            return carry

        lax.fori_loop(0, nfull, weights_body, 0)

        o_ref[0, 0:rows, :] = jnp.where(lane < DH_C, acc_ref[0, 0:rows, :], acc_ref[1, 0:rows, :]).astype(BF16)

    tail = lq % tq
    if tail == 0:
        process(tq)
    else:
        @pl.when(qi < nq - 1)
        def _():
            process(tq)

        @pl.when(qi == nq - 1)
        def _():
            process(tail)


def _stick_breaking(q, k, v, *, tq, q_pos0):
    bsz, lq, _ = q.shape
    lk = k.shape[2]
    nq = -(-lq // tq)
    nkt = SB_GROUP * (-(-lk // (SB_GROUP * KEY_TILE)))
    kern = functools.partial(_sb_kernel, tq=tq, lq=lq, lk=lk, q_pos0=q_pos0)
    return pl.pallas_call(
        kern,
        grid=(bsz, H_C // 2, nq),
        in_specs=[
            pl.BlockSpec((1, tq, 2 * DH_C), lambda b, h, i: (b, i, h)),
            pl.BlockSpec((1, 2, lk, DH_C), lambda b, h, i: (b, h, 0, 0)),
            pl.BlockSpec((1, 2, lk, DH_C), lambda b, h, i: (b, h, 0, 0)),
        ],
        out_specs=pl.BlockSpec((1, tq, 2 * DH_C), lambda b, h, i: (b, i, h)),
        out_shape=jax.ShapeDtypeStruct((bsz, lq, D_MODEL), BF16),
        scratch_shapes=[
            pltpu.VMEM((nkt * KEY_TILE, 2 * DH_C), BF16),
            pltpu.VMEM((nkt * KEY_TILE, 2 * DH_C), BF16),
            pltpu.VMEM((2, tq, 2 * DH_C), F32),
            pltpu.VMEM((2, tq, KEY_TILE), F32),
            pltpu.VMEM((2, nkt, tq, KEY_TILE), F32),
            pltpu.VMEM((2, nkt, tq, KEY_TILE), F32),
            pltpu.VMEM((2, nkt, tq, 2 * KEY_TILE), BF16),
        ],
        compiler_params=_params(("parallel", "parallel", "arbitrary")),
        name="stick_breaking",
    )(q, k, v)


def _layer_norm(z, g, b):
    mu = jnp.mean(z, axis=1, keepdims=True)
    zc = z - mu
    var = jnp.mean(zc * zc, axis=1, keepdims=True)
    return zc * lax.rsqrt(var + LN_EPS) * g + b


def _outproj_kernel(*refs, n_in):
    mix_refs = refs[:n_in]
    (w_ref, x_ref, g_ref, b_ref, wr_ref, br_ref,
     x1_ref, x1b_ref, ti_ref, tg_ref, rk_ref, cnt_ref, tri_ref) = refs[n_in:]
    i = pl.program_id(0)
    tm = x_ref.shape[0]

    @pl.when(i == 0)
    def _():
        cnt_ref[...] = jnp.zeros_like(cnt_ref)
        tri_ref[...] = (lax.broadcasted_iota(jnp.int32, (tm, tm), 0)
                        < lax.broadcasted_iota(jnp.int32, (tm, tm), 1)).astype(BF16)

    y = None
    c0 = 0
    for r in mix_refs:
        wd = r.shape[1]
        part = _dot(r[...], w_ref[c0:c0 + wd, :])
        y = part if y is None else y + part
        c0 += wd
    x1 = _layer_norm(DN_ALPHA * x_ref[...] + y, g_ref[...], b_ref[...])
    x1_ref[...] = x1
    x1b_ref[...] = x1.astype(BF16)
    logits = _dot_nt(wr_ref[...], x1, precision=lax.Precision.HIGHEST) + br_ref[...]
    ei = lax.broadcasted_iota(jnp.int32, (N_EXPERTS, tm), 0)
    cur = logits
    vals, idxs = [], []
    for _k in range(TOP_K):
        mx = jnp.max(cur, axis=0, keepdims=True)
        ix = jnp.min(jnp.where(cur == mx, ei, N_EXPERTS), axis=0, keepdims=True)
        vals.append(mx)
        idxs.append(ix)
        cur = jnp.where(ei == ix, -jnp.inf, cur)
    es = [jnp.exp(vv - vals[0]) for vv in vals]
    tot = es[0] + es[1] + es[2] + es[3]
    ti_ref[...] = jnp.concatenate(idxs, axis=0)
    tg_ref[...] = jnp.concatenate([e / tot for e in es], axis=0)
    run = cnt_ref[...]
    ranks = []
    for kk in range(TOP_K):
        oh = (ei == idxs[kk])
        before = _dot(oh.astype(BF16), tri_ref[...])
        ranks.append(jnp.sum(jnp.where(oh, before + run, 0.0), axis=0, keepdims=True))
        run = run + jnp.sum(oh.astype(F32), axis=1, keepdims=True)
    rk_ref[...] = jnp.concatenate(ranks, axis=0).astype(jnp.int32)
    cnt_ref[...] = run


def _outproj_ln_router(mixes, w, x_all, g, b, wr_t, br, tile0, ntiles):
    tm = TOKEN_TILE
    n = ntiles * tm
    kern = functools.partial(_outproj_kernel, n_in=len(mixes))
    row_in = lambda wd: pl.BlockSpec((tm, wd), lambda i: (tile0 + i, 0))
    row = lambda wd: pl.BlockSpec((tm, wd), lambda i: (i, 0))
    full = lambda *shape: pl.BlockSpec(shape, lambda i: (0,) * len(shape))
    lanes = lambda rows: pl.BlockSpec((rows, tm), lambda i: (0, i))
    return pl.pallas_call(
        kern,
        grid=(ntiles,),
        in_specs=[row_in(m.shape[1]) for m in mixes] + [
            full(D_MODEL, D_MODEL), row_in(D_MODEL), full(1, D_MODEL), full(1, D_MODEL),
            full(N_EXPERTS, D_MODEL), full(N_EXPERTS, 1)],
        out_specs=[row(D_MODEL), row(D_MODEL), lanes(TOP_K), lanes(TOP_K), lanes(TOP_K), full(N_EXPERTS, 1)],
        out_shape=[
            jax.ShapeDtypeStruct((n, D_MODEL), F32),
            jax.ShapeDtypeStruct((n, D_MODEL), BF16),
            jax.ShapeDtypeStruct((TOP_K, n), jnp.int32),
            jax.ShapeDtypeStruct((TOP_K, n), F32),
            jax.ShapeDtypeStruct((TOP_K, n), jnp.int32),
            jax.ShapeDtypeStruct((N_EXPERTS, 1), F32),
        ],
        scratch_shapes=[pltpu.VMEM((tm, tm), BF16)],
        compiler_params=_params(("arbitrary",)),
        name="outproj_ln_router",
    )(*mixes, w, x_all, g, b, wr_t, br)


def _moe_kernel(be_ref, nu_ref, x_ref, wgu_ref, bgu_ref, wdn_ref, bdn_ref, o_ref):
    @pl.when(pl.program_id(0) < nu_ref[0])
    def _():
        h = _dot(x_ref[...], wgu_ref[0]) + bgu_ref[0]
        glu = jnp.minimum(h[:, 0:D_EXPERT], SWIGLU_LIMIT)
        lin = jnp.clip(h[:, D_EXPERT:2 * D_EXPERT], -SWIGLU_LIMIT, SWIGLU_LIMIT)
        act = glu * _sigmoid(SWIGLU_ALPHA * glu) * (lin + 1.0)
        o_ref[...] = (_dot(act.astype(BF16), wdn_ref[0]) + bdn_ref[0]).astype(BF16)


def _moe_experts(blk_e, n_used, xg, wgu, bgu, wdn, bdn):
    nb = blk_e.shape[0]
    grid_spec = pltpu.PrefetchScalarGridSpec(
        num_scalar_prefetch=2,
        grid=(nb,),
        in_specs=[
            pl.BlockSpec((MOE_BLOCK, D_MODEL), lambda i, be, nu: (i, 0)),
            pl.BlockSpec((1, D_MODEL, 2 * D_EXPERT), lambda i, be, nu: (be[i], 0, 0)),
            pl.BlockSpec((1, 1, 2 * D_EXPERT), lambda i, be, nu: (be[i], 0, 0)),
            pl.BlockSpec((1, D_EXPERT, D_MODEL), lambda i, be, nu: (be[i], 0, 0)),
            pl.BlockSpec((1, 1, D_MODEL), lambda i, be, nu: (be[i], 0, 0)),
        ],
        out_specs=pl.BlockSpec((MOE_BLOCK, D_MODEL), lambda i, be, nu: (i, 0)),
    )
    return pl.pallas_call(
        _moe_kernel,
        grid_spec=grid_spec,
        out_shape=jax.ShapeDtypeStruct((nb * MOE_BLOCK, D_MODEL), BF16),
        compiler_params=_params(("arbitrary",)),
        name="moe_experts",
    )(blk_e, n_used, xg, wgu, bgu, wdn, bdn)


def _deinterleave_kernel(w_ref, p_ref, o_ref):
    o_ref[0] = _dot(w_ref[0].astype(BF16), p_ref[...]).astype(BF16)


def _deinterleave_gu(w):
    n, d, f2 = w.shape
    col = jnp.arange(f2, dtype=jnp.int32)
    src = jnp.where(col < f2 // 2, 2 * col, 2 * (col - f2 // 2) + 1)
    perm = (col[:, None] == src[None, :]).astype(BF16)
    rows = 512
    return pl.pallas_call(
        _deinterleave_kernel,
        grid=(n, d // rows),
        in_specs=[
            pl.BlockSpec((1, rows, f2), lambda e, r: (e, r, 0)),
            pl.BlockSpec((f2, f2), lambda e, r: (0, 0)),
        ],
        out_specs=pl.BlockSpec((1, rows, f2), lambda e, r: (e, r, 0)),
        out_shape=jax.ShapeDtypeStruct((n, d, f2), BF16),
        compiler_params=_params(("parallel", "parallel")),
        name="deinterleave_gu",
    )(w, perm)


def _combine_ln_kernel(x_ref, r_ref, tg_ref, g_ref, b_ref, o_ref):
    tg = tg_ref[...]
    f = r_ref[0].astype(F32) * tg[:, 0:1]
    for kk in range(1, TOP_K):
        f = f + r_ref[kk].astype(F32) * tg[:, kk:kk + 1]
    o_ref[...] = _layer_norm(DN_ALPHA * x_ref[...] + f, g_ref[...], b_ref[...])


def _combine_ln_into_kernel(x_ref, r_ref, tg_ref, g_ref, b_ref, prev_ref, o_ref):
    del prev_ref
    _combine_ln_kernel(x_ref, r_ref, tg_ref, g_ref, b_ref, o_ref)


def _combine_ln(x1, rows, gates, g, b, into, tile0, n_total):
    n = x1.shape[0]
    tm = TOKEN_TILE
    row = pl.BlockSpec((tm, D_MODEL), lambda i: (i, 0))
    vec = pl.BlockSpec((1, D_MODEL), lambda i: (0, 0))
    in_specs = [row, pl.BlockSpec((TOP_K, tm, D_MODEL), lambda i: (0, i, 0)),
                pl.BlockSpec((tm, TOP_K), lambda i: (i, 0)), vec, vec]
    args = [x1, rows, gates, g, b]
    if into is not None:
        in_specs.append(pl.BlockSpec(memory_space=pl.ANY))
        args.append(into)
    return pl.pallas_call(
        _combine_ln_kernel if into is None else _combine_ln_into_kernel,
        grid=(n // tm,),
        in_specs=in_specs,
        out_specs=pl.BlockSpec((tm, D_MODEL), lambda i: (tile0 + i, 0)),
        out_shape=jax.ShapeDtypeStruct((n_total, D_MODEL), F32),
        input_output_aliases={} if into is None else {len(args) - 1: 0},
        compiler_params=_params(("parallel",)),
        name="combine_ln",
    )(*args)


def _moe_layer(x1, x1b, top_i, top_g, rank, sizes, e0, wgu, bgu, wdn, bdn, g, b, into, tile0, n_total):
    n = x1.shape[0]
    nb = -(-(n * TOP_K) // MOE_BLOCK) + N_EXPERTS
    sizes = sizes[:, 0].astype(jnp.int32)
    nblk = (sizes + MOE_BLOCK - 1) // MOE_BLOCK
    blk_end = jnp.cumsum(nblk)
    pad_starts = (blk_end - nblk) * MOE_BLOCK
    n_used = blk_end[-1]
    blk_ids = jnp.minimum(jnp.arange(nb, dtype=jnp.int32), n_used - 1)
    blk_e = jnp.sum((blk_ids[:, None] >= blk_end[None, :]).astype(jnp.int32), axis=1)
    blk_e = jnp.clip(blk_e, 0, N_EXPERTS - 1)
    experts = jnp.arange(N_EXPERTS, dtype=jnp.int32)
    dest = jnp.sum(jnp.where(top_i[:, :, None] == experts, pad_starts, 0), axis=2) + rank
    tok = jnp.broadcast_to(jnp.arange(n, dtype=jnp.int32)[None], (TOP_K, n))
    tok_of_row = jnp.zeros((nb * MOE_BLOCK,), jnp.int32).at[dest.reshape(-1)].set(
        tok.reshape(-1), unique_indices=True, indices_are_sorted=False)
    xg = x1b.at[tok_of_row].get(mode="promise_in_bounds")
    out = _moe_experts(blk_e + e0, n_used.reshape(1), xg, wgu, bgu, wdn, bdn)
    rows = out.at[dest.reshape(-1)].get(mode="promise_in_bounds").reshape(TOP_K, n, D_MODEL)
    return _combine_ln(x1, rows, jnp.transpose(top_g), g, b, into, tile0, n_total)


def _rel_bucket(rel):
    half = N_BUCKETS // 2
    exact = half // 2
    ret = jnp.where(rel > 0, half, 0)
    n = jnp.abs(rel)
    large = exact + (jnp.log(jnp.maximum(n, 1).astype(F32) / exact)
                     / math.log(MAX_DISTANCE / exact) * (half - exact)).astype(jnp.int32)
    large = jnp.minimum(large, half - 1)
    return ret + jnp.where(n < exact, n, large)


def _near_bias_tiles(rel_table, tq, q_pos0):
    base = q_pos0 % KEY_TILE
    span = max(1, tq // KEY_TILE)
    i = jnp.arange(tq, dtype=jnp.int32)[:, None] + base
    j = jnp.arange(KEY_TILE, dtype=jnp.int32)[None, :]
    tiles = [jnp.full((tq, KEY_TILE), -2 * MAX_DISTANCE, jnp.int32)]
    for d in range(-1, span + 1):
        tiles.append(d * KEY_TILE + j - i)
    rel = jnp.stack(tiles)
    bucket = _rel_bucket(rel)
    out = jnp.zeros((H_B,) + rel.shape, F32)
    for bk in range(N_BUCKETS):
        out = jnp.where(bucket[None] == bk, rel_table[bk].astype(F32)[:, None, None, None], out)
    return out * LOG2_E


def kernel(x_prompt, x_sample, cache_diff_k, cache_diff_v, state_mlstm_C, state_mlstm_n, state_mlstm_m, state_mlstm_conv, cache_sb_k, cache_sb_v, meta_tokens, rel_bias, w_in_ab, w_out_ab, conv_w_a, conv_b_a, w_aq_a, w_ak_a, b_if_a, mh_gain_a, lam_q1, lam_k1, lam_q2, lam_k2, subln_gain_b, w_in_c, w_out_c, ln_g, ln_b, w_router, b_router, w_gu, b_gu, w_down, b_down):
    bp, sp, _ = x_prompt.shape
    bs, ss, _ = x_sample.shape
    lp = N_META + sp
    past = cache_diff_k.shape[3]
    n_p = bp * lp
    n_s = bs * ss
    tm_p = lp // 3 if (lp % 3 == 0 and (lp // 3) % 16 == 0) else lp
    tq_p = 2 * KEY_TILE

    parts = []
    for bi in range(bp):
        parts += [meta_tokens.astype(x_prompt.dtype), x_prompt[bi]]
    x_all = jnp.concatenate(parts + [x_sample.reshape(n_s, D_MODEL)], axis=0)

    j = 0
    lam_init = 0.8 - 0.6 * math.exp(-0.3 * 0)
    w_ab = w_in_ab[j]
    w_perm = jnp.concatenate([
        w_ab[:, 0:OFF_AIF],
        jnp.pad(w_ab[:, OFF_AIF:OFF_BQ], ((0, 0), (0, LANE - 2 * H_A))),
        w_ab[:, OFF_BQ:OFF_BQ + W_B] * (DH_B ** -0.5 * LOG2_E),
        w_ab[:, OFF_BQ + W_B:]], axis=1).astype(BF16)
    bg = jnp.pad(b_if_a[j], (0, LANE - 2 * H_A)).reshape(1, LANE)
    lam = (jnp.exp(jnp.sum(lam_q1[j] * lam_k1[j])) - jnp.exp(jnp.sum(lam_q2[j] * lam_k2[j]))).astype(F32) + lam_init
    lam_v = jnp.broadcast_to(lam.reshape(1, 1), (1, LANE))
    cw = conv_w_a[j]
    cb = conv_b_a[j].reshape(1, W_A)
    wq = w_aq_a[j].astype(BF16)
    wk = w_ak_a[j].astype(BF16)
    gain_a = mh_gain_a[j].reshape(1, W_A)
    gain_b = subln_gain_b[j].reshape(1, 2 * DH_B)

    def ab_group(row0, bsz, seq, tm, lead, state, past_kv, tq, q_pos0, chunked):
        a, g, q, k_new, v_new = _inproj_ab(x_all, row0, bsz, seq, tm, w_perm, bg)
        c0, n0, m0, buf = state
        mix_a, c1, n1, m1, cs = _mlstm(
            a, g, bsz, seq, lead, cw, cb, wq, wk, gain_a,
            c0, n0.reshape(bsz, H_A, 1, DH_A),
            jnp.broadcast_to(m0[:, :, None, None], (bsz, H_A, 1, LANE)), buf)
        if past_kv is None:
            k_all, v_all = k_new, v_new
        else:
            k_all = jnp.concatenate([past_kv[0], k_new], axis=2)
            v_all = jnp.concatenate([past_kv[1], v_new], axis=2)
        near = _near_bias_tiles(rel_bias, tq, q_pos0)
        mix_b = _diff_attention(q.reshape(bsz, seq, W_B), k_all, v_all, near, lam_v, gain_b,
                                tq=tq, q_pos0=q_pos0, chunked=chunked, lam_init=lam_init)
        outs = (k_new, v_new, c1, n1.reshape(bsz, H_A, DH_A), m1[:, :, 0, 0], cs)
        return mix_a, mix_b.reshape(bsz * seq, W_B), outs

    zero_state = (jnp.zeros((bp, H_A, DH_A, DH_A), F32), jnp.zeros((bp, H_A, DH_A), F32),
                  jnp.zeros((bp, H_A), F32), jnp.zeros((bp, CONV_W - 1, W_A), F32))
    mix_a_p, mix_b_p, ab_p = ab_group(0, bp, lp, tm_p, N_META, zero_state, None, tq_p, 0, True)
    s_state = (state_mlstm_C[j], state_mlstm_n[j], state_mlstm_m[j], state_mlstm_conv[j])
    mix_a_s, mix_b_s, ab_s = ab_group(n_p, bs, ss, ss, 0, s_state, (cache_diff_k[j], cache_diff_v[j]),
                                      ss, past, False)
    mix_a = jnp.concatenate([mix_a_p, mix_a_s], axis=0)
    mix_b = jnp.concatenate([mix_b_p, mix_b_s], axis=0)

    n_le = w_gu.shape[0] * N_EXPERTS
    wgu = _deinterleave_gu(w_gu.reshape(n_le, D_MODEL, 2 * D_EXPERT))
    bgu = jnp.concatenate([b_gu[..., 0::2], b_gu[..., 1::2]], axis=-1).reshape(n_le, 1, 2 * D_EXPERT)
    wdn = w_down.reshape(n_le, D_EXPERT, D_MODEL).astype(BF16)
    bdn = b_down.reshape(n_le, 1, D_MODEL)

    def token_stage(layer, mixes, w_out, x_in):
        wr_t = jnp.transpose(w_router[layer])
        br = b_router[layer].reshape(N_EXPERTS, 1)
        w_out = w_out.astype(BF16)
        ntile = x_in.shape[0] // TOKEN_TILE
        bounds = [ntile * s // MOE_SPLIT for s in range(MOE_SPLIT + 1)]
        out = None
        for t0, t1 in zip(bounds[:-1], bounds[1:]):
            x1, x1b, top_i, top_g, rank, sizes = _outproj_ln_router(
                mixes, w_out, x_in, ln_g[layer, 0].reshape(1, D_MODEL), ln_b[layer, 0].reshape(1, D_MODEL),
                wr_t, br, t0, t1 - t0)
            out = _moe_layer(x1, x1b, top_i, top_g, rank, sizes, layer * N_EXPERTS, wgu, bgu, wdn, bdn,
                             ln_g[layer, 1].reshape(1, D_MODEL), ln_b[layer, 1].reshape(1, D_MODEL),
                             out, t0, x_in.shape[0])
        return out

    x_all = token_stage(0, [mix_a, mix_b], w_out_ab[j], x_all)

    q_scale = jnp.where(jnp.arange(3 * D_MODEL) < D_MODEL, DH_C ** -0.5 * LOG2_E, 1.0).astype(F32)
    w_c = (w_in_c[j] * q_scale).astype(BF16)

    def c_group(row0, bsz, seq, tm, past_kv, tq, q_pos0):
        q, k_new, v_new = _inproj_c(x_all, row0, bsz, seq, tm, w_c)
        if past_kv is None:
            k_all, v_all = k_new, v_new
        else:
            k_all = jnp.concatenate([past_kv[0], k_new], axis=2)
            v_all = jnp.concatenate([past_kv[1], v_new], axis=2)
        o = _stick_breaking(q.reshape(bsz, seq, D_MODEL), k_all, v_all, tq=tq, q_pos0=q_pos0)
        return o.reshape(bsz * seq, D_MODEL), (k_new, v_new)

    mix_p, c_p = c_group(0, bp, lp, tm_p, None, 2 * KEY_TILE, 0)
    mix_s, c_s = c_group(n_p, bs, ss, ss, (cache_sb_k[j], cache_sb_v[j]), ss, past)
    x_all = token_stage(1, [jnp.concatenate([mix_p, mix_s], axis=0)], w_out_c[j], x_all)

    y_prompt = x_all[:n_p].reshape(bp, lp, D_MODEL)[:, N_META:]
    y_sample = x_all[n_p:].reshape(bs, ss, D_MODEL)
    stack = lambda t: t[None]
    return (y_prompt, y_sample,
            stack(ab_p[0]), stack(ab_p[1]), stack(ab_p[2]), stack(ab_p[3]), stack(ab_p[4]), stack(ab_p[5]),
            stack(c_p[0]), stack(c_p[1]),
            stack(ab_s[0]), stack(ab_s[1]), stack(ab_s[2]), stack(ab_s[3]), stack(ab_s[4]), stack(ab_s[5]),
            stack(c_s[0]), stack(c_s[1]))
```

```python
import functools
import math

import jax
import jax.numpy as jnp
from jax import lax
from jax.experimental import pallas as pl
from jax.experimental.pallas import tpu as pltpu

F32 = jnp.float32
BF16 = jnp.bfloat16

D_MODEL = 1024
DEPTH = 2
CHUNK = 64
N_META = 16
H_A = 4
DH_A = 128
W_A = H_A * DH_A
CONV_W = 4
H_B = 4
DH_B = 64
W_B = H_B * 2 * DH_B
H_C = 16
DH_C = 64
N_BUCKETS = 32
MAX_DISTANCE = 128
N_EXPERTS = 32
TOP_K = 4
D_EXPERT = D_MODEL // 2
SWIGLU_LIMIT = 7.0
SWIGLU_ALPHA = 1.702
DN_ALPHA = (2 * DEPTH) ** 0.25
LN_EPS = 1e-5
OFF_AIF = 3 * W_A
OFF_BQ = OFF_AIF + 2 * H_A

LANE = 128
KEY_TILE = 128
LOG2_KEY_TILE = 7
LOG2_CHUNK = 6
VMEM_LIMIT = 56 * 1024 * 1024
MOE_BLOCK = 512
TOKEN_TILE = 512
SB_GROUP = 8
MOE_SPLIT = 2
NEG_BIG = -1e30
LOG2_E = 1.4426950408889634

COL_G = 3 * W_A
COL_Q = COL_G + LANE
COL_K = COL_Q + W_B
COL_V = COL_K + W_B
D_IN_AB_PAD = COL_V + W_B


def _dot(a, b):
    return jnp.dot(a, b, preferred_element_type=F32)


def _dot_nt(a, b, precision=None):
    return lax.dot_general(a, b, (((1,), (1,)), ((), ())), preferred_element_type=F32, precision=precision)


def _dot_tn(a, b):
    return lax.dot_general(a, b, (((0,), (0,)), ((), ())), preferred_element_type=F32)


def _log_sigmoid(x):
    return jnp.minimum(x, 0.0) - jnp.log(1.0 + jnp.exp(-jnp.abs(x)))


def _sigmoid(x):
    return 1.0 / (1.0 + jnp.exp(-x))


def _chunk_id(pos):
    return (pos + (CHUNK - N_META)) >> LOG2_CHUNK


def _params(sem):
    return pltpu.CompilerParams(dimension_semantics=sem, vmem_limit_bytes=VMEM_LIMIT)


def _inproj_ab_kernel(x_ref, w_ref, bg_ref, a_ref, g_ref, q_ref, k_ref, v_ref):
    xb = x_ref[...].astype(BF16)
    a_ref[...] = _dot(xb, w_ref[:, 0:COL_G])
    g_ref[...] = _dot(xb, w_ref[:, COL_G:COL_Q]) + bg_ref[...]
    q_ref[...] = _dot(xb, w_ref[:, COL_Q:COL_K]).astype(BF16)
    for h in range(H_B):
        k_ref[0, h] = _dot(xb, w_ref[:, COL_K + 2 * DH_B * h:COL_K + 2 * DH_B * (h + 1)])
        v_ref[0, h] = _dot(xb, w_ref[:, COL_V + 2 * DH_B * h:COL_V + 2 * DH_B * (h + 1)])


def _inproj_ab(x_all, row0, bsz, seq, tm, w, bg):
    nrt = seq // tm
    n = bsz * seq
    if row0 % tm:
        x_all, row0 = x_all[row0:row0 + n], 0
    off = row0 // tm
    return pl.pallas_call(
        _inproj_ab_kernel,
        grid=(bsz, nrt),
        in_specs=[
            pl.BlockSpec((tm, D_MODEL), lambda b, r: (off + b * nrt + r, 0)),
            pl.BlockSpec((D_MODEL, D_IN_AB_PAD), lambda b, r: (0, 0)),
            pl.BlockSpec((1, LANE), lambda b, r: (0, 0)),
        ],
        out_specs=[
            pl.BlockSpec((tm, COL_G), lambda b, r: (b * nrt + r, 0)),
            pl.BlockSpec((tm, LANE), lambda b, r: (b * nrt + r, 0)),
            pl.BlockSpec((tm, W_B), lambda b, r: (b * nrt + r, 0)),
            pl.BlockSpec((1, H_B, tm, 2 * DH_B), lambda b, r: (b, 0, r, 0)),
            pl.BlockSpec((1, H_B, tm, 2 * DH_B), lambda b, r: (b, 0, r, 0)),
        ],
        out_shape=[
            jax.ShapeDtypeStruct((n, COL_G), F32),
            jax.ShapeDtypeStruct((n, LANE), F32),
            jax.ShapeDtypeStruct((n, W_B), BF16),
            jax.ShapeDtypeStruct((bsz, H_B, seq, 2 * DH_B), F32),
            jax.ShapeDtypeStruct((bsz, H_B, seq, 2 * DH_B), F32),
        ],
        compiler_params=_params(("parallel", "parallel")),
        name="inproj_ab",
    )(x_all, w, bg)


def _mlstm_kernel(a_ref, g_ref, cw_ref, cb_ref, wq_ref, wk_ref, gain_ref, c0_ref, n0_ref, m0_ref, buf_ref,
                  out_ref, c1_ref, n1_ref, m1_ref, cs_ref, q_s, num_s, rs_s, u_s, nv_s, *, seq, lead):
    cs_ref[0] = a_ref[seq - (CONV_W - 1):seq, 0:W_A]
    c1_ref[...] = c0_ref[...]
    n1_ref[...] = n0_ref[...]
    m1_ref[...] = m0_ref[...]
    sel = (lax.broadcasted_iota(jnp.int32, (8, LANE), 0) == lax.broadcasted_iota(jnp.int32, (8, LANE), 1)).astype(F32)
    lane = lax.broadcasted_iota(jnp.int32, (1, LANE), 1)

    def local(c, r0, lc, first):
        if first:
            win = jnp.concatenate([jnp.zeros((5, W_A), F32), buf_ref[0], a_ref[0:lc, 0:W_A]], axis=0)
        else:
            win = a_ref[pl.ds(pl.multiple_of(r0 - 8, 8), lc + 8), 0:W_A]
        y = cb_ref[...]
        for j in range(CONV_W):
            y = y + win[5 + j:5 + j + lc, :] * cw_ref[j:j + 1, :]
        ca = y * _sigmoid(y)
        g = g_ref[pl.ds(r0, lc), :]
        g_rows = _dot_nt(sel, g, precision=lax.Precision.HIGHEST)
        ti = lax.broadcasted_iota(jnp.int32, (lc, lc), 0)
        si = lax.broadcasted_iota(jnp.int32, (lc, lc), 1)
        causal = si <= ti
        rs = jnp.zeros((lc, LANE), F32)
        for h in range(H_A):
            hs = slice(DH_A * h, DH_A * (h + 1))
            ig_c = g[:, h:h + 1]
            lf_c = _log_sigmoid(g[:, H_A + h:H_A + h + 1])
            ig_r = g_rows[h:h + 1, :]
            lf_r = _log_sigmoid(g_rows[H_A + h:H_A + h + 1, :])
            b_c = jnp.sum(jnp.where(causal, lf_r, 0.0), axis=1, keepdims=True)
            b_r = jnp.sum(jnp.where(ti <= si, lf_c, 0.0), axis=0, keepdims=True)
            dmat = jnp.where(causal, b_c - b_r + ig_r, -jnp.inf)
            m_loc = jnp.max(dmat, axis=1, keepdims=True)
            w = jnp.exp(dmat - m_loc)
            cab = ca[:, hs].astype(BF16)
            qb = _dot(cab, wq_ref[h]).astype(BF16)
            k = _dot(cab, wk_ref[h]) * (DH_A ** -0.5)
            vb = a_ref[pl.ds(r0, lc), W_A + DH_A * h:W_A + DH_A * (h + 1)].astype(BF16)
            s = _dot_nt(qb, k.astype(BF16)) * w
            q_s[pl.ds(r0, lc), hs] = qb
            num_s[pl.ds(r0, lc), hs] = _dot(s.astype(BF16), vb)
            den_loc = jnp.sum(s, axis=1, keepdims=True)
            rs = jnp.where(lane == h, den_loc, rs)
            rs = jnp.where(lane == H_A + h, m_loc, rs)
            rs = jnp.where(lane == 2 * H_A + h, b_c, rs)
            w_end = jnp.exp(b_c[lc - 1:lc, :] - b_c + ig_c - m_loc[lc - 1:lc, :])
            kw = k * w_end
            u_s[c, h] = _dot_tn(kw.astype(BF16), vb)
            nv_s[c, h] = jnp.sum(kw, axis=0, keepdims=True)
        rs_s[pl.ds(r0, lc), :] = rs

    def carry(c, r0, lc):
        rs = rs_s[pl.ds(r0, lc), :]
        for h in range(H_A):
            hs = slice(DH_A * h, DH_A * (h + 1))
            den_loc = jnp.broadcast_to(rs[:, h:h + 1], (lc, DH_A))
            m_loc = jnp.broadcast_to(rs[:, H_A + h:H_A + h + 1], (lc, DH_A))
            b_c = jnp.broadcast_to(rs[:, 2 * H_A + h:2 * H_A + h + 1], (lc, DH_A))
            m_prev = m1_ref[0, h]
            c_prev = c1_ref[0, h]
            n_prev = n1_ref[0, h]
            inter = b_c + m_prev
            m_t = jnp.maximum(inter, m_loc)
            gg = jnp.exp(inter - m_t)
            sc = jnp.exp(m_loc - m_t)
            qb = q_s[pl.ds(r0, lc), hs]
            num = sc * num_s[pl.ds(r0, lc), hs] + gg * _dot(qb, c_prev.astype(BF16))
            qn = jnp.broadcast_to(jnp.sum(qb.astype(F32) * n_prev, axis=1, keepdims=True), (lc, DH_A))
            den = sc * den_loc + gg * qn
            hh = num / jnp.maximum(jnp.abs(den), jnp.exp(-m_t))
            m_new = m_t[lc - 1:lc, :]
            decay = jnp.exp(inter[lc - 1:lc, :] - m_new)
            grow = sc[lc - 1:lc, :]
            c1_ref[0, h] = decay * c_prev + grow * u_s[c, h]
            n1_ref[0, h] = decay * n_prev + grow * nv_s[c, h]
            m1_ref[0, h] = m_new
            hn = hh * lax.rsqrt(jnp.mean(hh * hh, axis=1, keepdims=True) + LN_EPS) * gain_ref[:, hs]
            oa = a_ref[pl.ds(r0, lc), 2 * W_A + DH_A * h:2 * W_A + DH_A * (h + 1)]
            out_ref[pl.ds(r0, lc), hs] = (hn * _sigmoid(oa)).astype(BF16)

    first_len = lead if lead else CHUNK
    nrest = (seq - first_len) // CHUNK
    start = lambda i: pl.multiple_of(first_len + i * CHUNK, 16)
    local(0, 0, first_len, True)

    def local_body(i, c):
        local(2 * i + 1, start(2 * i), CHUNK, False)
        local(2 * i + 2, start(2 * i + 1), CHUNK, False)
        return c

    lax.fori_loop(0, nrest // 2, local_body, 0)
    if nrest % 2:
        local(nrest, start(nrest - 1), CHUNK, False)
    carry(0, 0, first_len)

    def carry_body(i, c):
        carry(i + 1, start(i), CHUNK)
        return c

    lax.fori_loop(0, nrest, carry_body, 0)


def _mlstm(a, g, bsz, seq, lead, cw, cb, wq, wk, gain, c0, n0, m0, buf):
    kern = functools.partial(_mlstm_kernel, seq=seq, lead=lead)
    assert seq >= CONV_W - 1
    nchunks = 1 + (seq - (lead if lead else CHUNK)) // CHUNK
    full = lambda *shape: pl.BlockSpec(shape, lambda b: (0,) * len(shape))
    per_b = lambda *shape: pl.BlockSpec((1,) + shape, lambda b: (b,) + (0,) * len(shape))
    return pl.pallas_call(
        kern,
        grid=(bsz,),
        in_specs=[
            pl.BlockSpec((seq, COL_G), lambda b: (b, 0)),
            pl.BlockSpec((seq, LANE), lambda b: (b, 0)),
            full(CONV_W, W_A), full(1, W_A), full(H_A, DH_A, DH_A), full(H_A, DH_A, DH_A), full(1, W_A),
            per_b(H_A, DH_A, DH_A), per_b(H_A, 1, DH_A), per_b(H_A, 1, LANE), per_b(CONV_W - 1, W_A),
        ],
        out_specs=[
            pl.BlockSpec((seq, W_A), lambda b: (b, 0)),
            per_b(H_A, DH_A, DH_A), per_b(H_A, 1, DH_A), per_b(H_A, 1, LANE), per_b(CONV_W - 1, W_A),
        ],
        out_shape=[
            jax.ShapeDtypeStruct((bsz * seq, W_A), BF16),
            jax.ShapeDtypeStruct((bsz, H_A, DH_A, DH_A), F32),
            jax.ShapeDtypeStruct((bsz, H_A, 1, DH_A), F32),
            jax.ShapeDtypeStruct((bsz, H_A, 1, LANE), F32),
            jax.ShapeDtypeStruct((bsz, CONV_W - 1, W_A), F32),
        ],
        scratch_shapes=[
            pltpu.VMEM((seq, W_A), BF16),
            pltpu.VMEM((seq, W_A), F32),
            pltpu.VMEM((seq, LANE), F32),
            pltpu.VMEM((nchunks, H_A, DH_A, DH_A), F32),
            pltpu.VMEM((nchunks, H_A, 1, DH_A), F32),
        ],
        compiler_params=_params(("parallel",)),
        name="mlstm",
    )(a, g, cw, cb, wq, wk, gain, c0, n0, m0, buf)


def _diff_kernel(lam_ref, q_ref, k_ref, v_ref, near_ref, gain_ref, o_ref, kb_ref, vb_ref, s_ref, mx_ref, l_ref,
                 acc_ref, *, tq, lq, lk, q_pos0, chunked, lam_init):
    qi = pl.program_id(2)
    nq = pl.num_programs(2)
    rows_k = kb_ref.shape[0]

    @pl.when(qi == 0)
    def _():
        kb_ref[0:lk, :] = k_ref[0, 0].astype(BF16)
        vb_ref[0:lk, :] = v_ref[0, 0].astype(BF16)
        kb_ref[lk:, :] = jnp.zeros((rows_k - lk, 2 * DH_B), BF16)
        vb_ref[lk:, :] = jnp.zeros((rows_k - lk, 2 * DH_B), BF16)

    q0 = q_pos0 + qi * tq
    qt = q0 >> LOG2_KEY_TILE
    lane = lax.broadcasted_iota(jnp.int32, (1, 2 * DH_B), 1)
    bias_far = near_ref[0, 0, 0:1, 0:1]
    span = near_ref.shape[1] - 3

    def process(rows):
        r2 = 2 * rows
        qf = q_ref[0, 0:rows, :].astype(F32)
        qs = jnp.concatenate([jnp.where(lane < DH_B, qf, 0.0), jnp.where(lane >= DH_B, qf, 0.0)],
                             axis=0).astype(BF16)
        mx_ref[0:r2, :] = jnp.full((r2, KEY_TILE), NEG_BIG, F32)

        def step(j0, width, mode):
            ks = pl.multiple_of(j0 * KEY_TILE, KEY_TILE)
            kt = kb_ref[pl.ds(ks, width * KEY_TILE), :]
            s = _dot_nt(qs, kt)
            if mode == "far":
                s = s + bias_far
            else:
                first = span + 3 - width
                bias = jnp.concatenate([near_ref[0, first + g, 0:rows, :] for g in range(width)], axis=1)
                kpos = ks + lax.broadcasted_iota(jnp.int32, (rows, width * KEY_TILE), 1)
                if chunked:
                    qpos = q0 + lax.broadcasted_iota(jnp.int32, (rows, width * KEY_TILE), 0)
                    mask = _chunk_id(kpos) <= _chunk_id(qpos)
                else:
                    mask = kpos < lk
                s = jnp.where(jnp.concatenate([mask, mask], axis=0),
                              s + jnp.concatenate([bias, bias], axis=0), NEG_BIG)
            mx = mx_ref[0:r2, :]
            for g in range(width):
                blk = s[:, g * KEY_TILE:(g + 1) * KEY_TILE]
                s_ref[j0 + g, 0:r2, :] = blk
                mx = jnp.maximum(mx, blk)
            mx_ref[0:r2, :] = mx

        @pl.when(qt >= 1)
        def _():
            step(qt - 1, span + 2, "near")

        @pl.when(qt == 0)
        def _():
            step(0, span + 1, "near")

        def for_tiles(count, fn):
            def body(g, carry):
                fn(8 * g, 8)
                return carry

            lax.fori_loop(0, count >> 3, body, 0)
            base = (count >> 3) << 3
            for w in range(1, 8):
                @pl.when((count & 7) == w)
                def _():
                    fn(base, w)

        for_tiles(jnp.maximum(qt - 1, 0), lambda j0, w: step(j0, w, "far"))
        m = jnp.max(mx_ref[0:r2, :], axis=1, keepdims=True)
        l_ref[0:r2, :] = jnp.zeros((r2, KEY_TILE), F32)
        acc_ref[0:r2, :] = jnp.zeros((r2, 2 * DH_B), F32)

        def weights(j0, width):
            vt = vb_ref[pl.ds(pl.multiple_of(j0 * KEY_TILE, KEY_TILE), width * KEY_TILE), :]
            lsum = l_ref[0:r2, :]
            ps = []
            for t in range(width):
                p = jnp.exp2(s_ref[j0 + t, 0:r2, :] - m)
                lsum = lsum + p
                ps.append(p.astype(BF16))
            l_ref[0:r2, :] = lsum
            acc_ref[0:r2, :] += _dot(ps[0] if width == 1 else jnp.concatenate(ps, axis=1), vt)

        for_tiles(qt + span + 1, weights)
        o = acc_ref[0:r2, :] / jnp.sum(l_ref[0:r2, :], axis=1, keepdims=True)
        lam = lam_ref[:, 0:1]
        o = o[0:rows] - lam * o[rows:2 * rows]
        o = o * lax.rsqrt(jnp.mean(o * o, axis=1, keepdims=True) + LN_EPS) * gain_ref[...] * (1.0 - lam_init)
        o_ref[0, 0:rows, :] = o.astype(BF16)

    tail = lq % tq
    if tail == 0:
        process(tq)
    else:
        @pl.when(qi < nq - 1)
        def _():
            process(tq)

        @pl.when(qi == nq - 1)
        def _():
            process(tail)


def _diff_attention(q, k, v, near, lam, gain, *, tq, q_pos0, chunked, lam_init):
    bsz, lq, _ = q.shape
    lk = k.shape[2]
    nq = -(-lq // tq)
    nkt = -(-lk // KEY_TILE)
    span = near.shape[1] - 3
    kern = functools.partial(_diff_kernel, tq=tq, lq=lq, lk=lk, q_pos0=q_pos0, chunked=chunked, lam_init=lam_init)
    return pl.pallas_call(
        kern,
        grid=(bsz, H_B, nq),
        in_specs=[
            pl.BlockSpec((1, LANE), lambda b, h, i: (0, 0)),
            pl.BlockSpec((1, tq, 2 * DH_B), lambda b, h, i: (b, i, h)),
            pl.BlockSpec((1, 1, lk, 2 * DH_B), lambda b, h, i: (b, h, 0, 0)),
            pl.BlockSpec((1, 1, lk, 2 * DH_B), lambda b, h, i: (b, h, 0, 0)),
            pl.BlockSpec((1, span + 3, tq, KEY_TILE), lambda b, h, i: (h, 0, 0, 0)),
            pl.BlockSpec((1, 2 * DH_B), lambda b, h, i: (0, 0)),
        ],
        out_specs=pl.BlockSpec((1, tq, 2 * DH_B), lambda b, h, i: (b, i, h)),
        out_shape=jax.ShapeDtypeStruct((bsz, lq, W_B), BF16),
        scratch_shapes=[
            pltpu.VMEM(((nkt + span) * KEY_TILE, 2 * DH_B), BF16),
            pltpu.VMEM(((nkt + span) * KEY_TILE, 2 * DH_B), BF16),
            pltpu.VMEM((nkt + span, 2 * tq, KEY_TILE), F32),
            pltpu.VMEM((2 * tq, KEY_TILE), F32),
            pltpu.VMEM((2 * tq, KEY_TILE), F32),
            pltpu.VMEM((2 * tq, 2 * DH_B), F32),
        ],
        compiler_params=_params(("parallel", "parallel", "arbitrary")),
        name="diff_attention",
    )(lam, q, k, v, near, gain)


def _inproj_c_kernel(x_ref, w_ref, q_ref, k_ref, v_ref):
    xb = x_ref[...].astype(BF16)
    q_ref[...] = _dot(xb, w_ref[:, 0:D_MODEL]).astype(BF16)
    yk = _dot(xb, w_ref[:, D_MODEL:2 * D_MODEL])
    for h in range(H_C):
        k_ref[0, h] = yk[:, DH_C * h:DH_C * (h + 1)]
    yv = _dot(xb, w_ref[:, 2 * D_MODEL:3 * D_MODEL])
    for h in range(H_C):
        v_ref[0, h] = yv[:, DH_C * h:DH_C * (h + 1)]


def _inproj_c(x_all, row0, bsz, seq, tm, w):
    nrt = seq // tm
    if row0 % tm:
        x_all, row0 = x_all[row0:row0 + bsz * seq], 0
    off = row0 // tm
    return pl.pallas_call(
        _inproj_c_kernel,
        grid=(bsz, nrt),
        in_specs=[
            pl.BlockSpec((tm, D_MODEL), lambda b, r: (off + b * nrt + r, 0)),
            pl.BlockSpec((D_MODEL, 3 * D_MODEL), lambda b, r: (0, 0)),
        ],
        out_specs=[
            pl.BlockSpec((tm, D_MODEL), lambda b, r: (b * nrt + r, 0)),
            pl.BlockSpec((1, H_C, tm, DH_C), lambda b, r: (b, 0, r, 0)),
            pl.BlockSpec((1, H_C, tm, DH_C), lambda b, r: (b, 0, r, 0)),
        ],
        out_shape=[
            jax.ShapeDtypeStruct((bsz * seq, D_MODEL), BF16),
            jax.ShapeDtypeStruct((bsz, H_C, seq, DH_C), F32),
            jax.ShapeDtypeStruct((bsz, H_C, seq, DH_C), F32),
        ],
        compiler_params=_params(("parallel", "parallel")),
        name="inproj_c",
    )(x_all, w)


def _sb_kernel(q_ref, k_ref, v_ref, o_ref, kb_ref, vb_ref, acc_ref, run_ref, u_ref, t_ref, hl_ref,
               *, tq, lq, lk, q_pos0):
    qi = pl.program_id(2)
    nq = pl.num_programs(2)
    rows_k = kb_ref.shape[0]

    @pl.when(qi == 0)
    def _():
        kb_ref[0:lk, :] = jnp.concatenate([k_ref[0, 0], k_ref[0, 1]], axis=1).astype(BF16)
        vb_ref[0:lk, :] = jnp.concatenate([v_ref[0, 0], v_ref[0, 1]], axis=1).astype(BF16)
        if rows_k > lk:
            kb_ref[lk:, :] = jnp.zeros((rows_k - lk, 2 * DH_C), BF16)
            vb_ref[lk:, :] = jnp.zeros((rows_k - lk, 2 * DH_C), BF16)

    q0 = q_pos0 + qi * tq
    jd = q0 >> LOG2_KEY_TILE
    jj = lax.broadcasted_iota(jnp.int32, (2 * KEY_TILE, 2 * KEY_TILE), 0) & (KEY_TILE - 1)
    ss = lax.broadcasted_iota(jnp.int32, (2 * KEY_TILE, 2 * KEY_TILE), 1)
    later = jnp.where((jj > ss) | (ss >= KEY_TILE), 1.0, 0.0).astype(BF16)
    lane = lax.broadcasted_iota(jnp.int32, (1, 2 * DH_C), 1)

    def process(rows):
        qf = q_ref[0, 0:rows, :].astype(F32)
        qh = [jnp.where(lane < DH_C, qf, 0.0).astype(BF16), jnp.where(lane >= DH_C, qf, 0.0).astype(BF16)]
        acc_ref[:, 0:rows, :] = jnp.zeros((2, rows, 2 * DH_C), F32)
        run_ref[:, 0:rows, :] = jnp.zeros((2, rows, KEY_TILE), F32)

        def stage_scores(j0, width, masked):
            ks = pl.multiple_of(j0 * KEY_TILE, KEY_TILE)
            kt = kb_ref[pl.ds(ks, width * KEY_TILE), :]
            for h in range(2):
                z = _dot_nt(qh[h], kt)
                sp = jnp.maximum(z, 0.0) + jnp.log(1.0 + jnp.exp2(-jnp.abs(z))) * LOG2_E
                u = z - sp
                if masked:
                    mask = (ks + lax.broadcasted_iota(jnp.int32, (rows, width * KEY_TILE), 1)) < (
                        q0 + lax.broadcasted_iota(jnp.int32, (rows, width * KEY_TILE), 0))
                    sp = jnp.where(mask, sp, 0.0)
                    u = jnp.where(mask, u, NEG_BIG)
                hi = sp.astype(BF16)
                lo = (sp - hi.astype(F32)).astype(BF16)
                for g in range(width):
                    cols = slice(g * KEY_TILE, (g + 1) * KEY_TILE)
                    u_ref[h, j0 + g, 0:rows, :] = u[:, cols]
                    hl_ref[h, j0 + g, 0:rows, :] = jnp.concatenate([hi[:, cols], lo[:, cols]], axis=1)

        def stage_weights(j0, width):
            ks = pl.multiple_of(j0 * KEY_TILE, KEY_TILE)
            vt = vb_ref[pl.ds(ks, width * KEY_TILE), :]
            for h in range(2):
                off = run_ref[h, 0:rows, :]
                parts = [None] * width
                for g in reversed(range(width)):
                    parts[g] = jnp.exp2(u_ref[h, j0 + g, 0:rows, :] - off).astype(BF16)
                    off = off + t_ref[h, j0 + g, 0:rows, :]
                a_all = parts[0] if width == 1 else jnp.concatenate(parts, axis=1)
                acc_ref[h, 0:rows, :] += _dot(a_all, vt)
                run_ref[h, 0:rows, :] = off

        def stage_sums(j0, width):
            for h in range(2):
                hl = hl_ref[h, pl.ds(j0, width), 0:rows, :].reshape(width * rows, 2 * KEY_TILE)
                cs = _dot(hl, later)
                for g in range(width):
                    blk = cs[g * rows:(g + 1) * rows]
                    u_ref[h, j0 + g, 0:rows, :] = u_ref[h, j0 + g, 0:rows, :] - blk[:, 0:KEY_TILE]
                    t_ref[h, j0 + g, 0:rows, :] = blk[:, KEY_TILE:2 * KEY_TILE]

        nfull = jd // SB_GROUP
        dgrp = nfull * SB_GROUP

        def scores_body(g, carry):
            stage_scores(SB_GROUP * g, SB_GROUP, False)
            return carry

        lax.fori_loop(0, nfull, scores_body, 0)

        def sums_body(g, carry):
            stage_sums(SB_GROUP * g, SB_GROUP)
            return carry

        lax.fori_loop(0, nfull, sums_body, 0)
        jlast = (q0 + rows - 1) >> LOG2_KEY_TILE
        for w in range(1, SB_GROUP + 1):
            @pl.when(jlast - dgrp == w - 1)
            def _():
                stage_scores(dgrp, w, True)
                stage_sums(dgrp, w)
                stage_weights(dgrp, w)

        def weights_body(g, carry):
            stage_weights(SB_GROUP * (nfull - 1 - g), SB_GROUP)
            return carry

        lax.fori_loop(0, nfull, weights_body, 0)

        o_ref[0, 0:rows, :] = jnp.where(lane < DH_C, acc_ref[0, 0:rows, :], acc_ref[1, 0:rows, :]).astype(BF16)

    tail = lq % tq
    if tail == 0:
        process(tq)
    else:
        @pl.when(qi < nq - 1)
        def _():
            process(tq)

        @pl.when(qi == nq - 1)
        def _():
            process(tail)


def _stick_breaking(q, k, v, *, tq, q_pos0):
    bsz, lq, _ = q.shape
    lk = k.shape[2]
    nq = -(-lq // tq)
    nkt = SB_GROUP * (-(-lk // (SB_GROUP * KEY_TILE)))
    kern = functools.partial(_sb_kernel, tq=tq, lq=lq, lk=lk, q_pos0=q_pos0)
    return pl.pallas_call(
        kern,
        grid=(bsz, H_C // 2, nq),
        in_specs=[
            pl.BlockSpec((1, tq, 2 * DH_C), lambda b, h, i: (b, i, h)),
            pl.BlockSpec((1, 2, lk, DH_C), lambda b, h, i: (b, h, 0, 0)),
            pl.BlockSpec((1, 2, lk, DH_C), lambda b, h, i: (b, h, 0, 0)),
        ],
        out_specs=pl.BlockSpec((1, tq, 2 * DH_C), lambda b, h, i: (b, i, h)),
        out_shape=jax.ShapeDtypeStruct((bsz, lq, D_MODEL), BF16),
        scratch_shapes=[
            pltpu.VMEM((nkt * KEY_TILE, 2 * DH_C), BF16),
            pltpu.VMEM((nkt * KEY_TILE, 2 * DH_C), BF16),
            pltpu.VMEM((2, tq, 2 * DH_C), F32),
            pltpu.VMEM((2, tq, KEY_TILE), F32),
            pltpu.VMEM((2, nkt, tq, KEY_TILE), F32),
            pltpu.VMEM((2, nkt, tq, KEY_TILE), F32),
            pltpu.VMEM((2, nkt, tq, 2 * KEY_TILE), BF16),
        ],
        compiler_params=_params(("parallel", "parallel", "arbitrary")),
        name="stick_breaking",
    )(q, k, v)


def _layer_norm(z, g, b):
    mu = jnp.mean(z, axis=1, keepdims=True)
    zc = z - mu
    var = jnp.mean(zc * zc, axis=1, keepdims=True)
    return zc * lax.rsqrt(var + LN_EPS) * g + b


def _outproj_kernel(*refs, n_in):
    mix_refs = refs[:n_in]
    (w_ref, x_ref, g_ref, b_ref, wr_ref, br_ref,
     x1_ref, x1b_ref, ti_ref, tg_ref, rk_ref, cnt_ref, tri_ref) = refs[n_in:]
    i = pl.program_id(0)
    tm = x_ref.shape[0]

    @pl.when(i == 0)
    def _():
        cnt_ref[...] = jnp.zeros_like(cnt_ref)
        tri_ref[...] = (lax.broadcasted_iota(jnp.int32, (tm, tm), 0)
                        < lax.broadcasted_iota(jnp.int32, (tm, tm), 1)).astype(BF16)

    y = None
    c0 = 0
    for r in mix_refs:
        wd = r.shape[1]
        part = _dot(r[...], w_ref[c0:c0 + wd, :])
        y = part if y is None else y + part
        c0 += wd
    x1 = _layer_norm(DN_ALPHA * x_ref[...] + y, g_ref[...], b_ref[...])
    x1_ref[...] = x1
    x1h = x1.astype(BF16)
    x1b_ref[...] = x1h
    x1l = (x1 - x1h.astype(F32)).astype(BF16)
    both = _dot_nt(wr_ref[...], x1h)
    logits = (both[0:N_EXPERTS] + both[N_EXPERTS:2 * N_EXPERTS]
              + _dot_nt(wr_ref[0:N_EXPERTS, :], x1l) + br_ref[...])
    ei = lax.broadcasted_iota(jnp.int32, (N_EXPERTS, tm), 0)
    cur = logits
    vals, idxs = [], []
    for _k in range(TOP_K):
        mx = jnp.max(cur, axis=0, keepdims=True)
        ix = jnp.min(jnp.where(cur == mx, ei, N_EXPERTS), axis=0, keepdims=True)
        vals.append(mx)
        idxs.append(ix)
        cur = jnp.where(ei == ix, -jnp.inf, cur)
    es = [jnp.exp(vv - vals[0]) for vv in vals]
    tot = es[0] + es[1] + es[2] + es[3]
    ti_ref[...] = jnp.concatenate(idxs, axis=0)
    tg_ref[...] = jnp.concatenate([e / tot for e in es], axis=0)
    run = cnt_ref[...]
    ranks = []
    for kk in range(TOP_K):
        oh = (ei == idxs[kk])
        before = _dot(oh.astype(BF16), tri_ref[...])
        ranks.append(jnp.sum(jnp.where(oh, before + run, 0.0), axis=0, keepdims=True))
        run = run + jnp.sum(oh.astype(F32), axis=1, keepdims=True)
    rk_ref[...] = jnp.concatenate(ranks, axis=0).astype(jnp.int32)
    cnt_ref[...] = run


def _outproj_ln_router(mixes, w, x_all, g, b, wr_t, br, tile0, ntiles):
    tm = TOKEN_TILE
    n = ntiles * tm
    kern = functools.partial(_outproj_kernel, n_in=len(mixes))
    row_in = lambda wd: pl.BlockSpec((tm, wd), lambda i: (tile0 + i, 0))
    row = lambda wd: pl.BlockSpec((tm, wd), lambda i: (i, 0))
    full = lambda *shape: pl.BlockSpec(shape, lambda i: (0,) * len(shape))
    lanes = lambda rows: pl.BlockSpec((rows, tm), lambda i: (0, i))
    return pl.pallas_call(
        kern,
        grid=(ntiles,),
        in_specs=[row_in(m.shape[1]) for m in mixes] + [
            full(D_MODEL, D_MODEL), row_in(D_MODEL), full(1, D_MODEL), full(1, D_MODEL),
            full(2 * N_EXPERTS, D_MODEL), full(N_EXPERTS, 1)],
        out_specs=[row(D_MODEL), row(D_MODEL), lanes(TOP_K), lanes(TOP_K), lanes(TOP_K), full(N_EXPERTS, 1)],
        out_shape=[
            jax.ShapeDtypeStruct((n, D_MODEL), F32),
            jax.ShapeDtypeStruct((n, D_MODEL), BF16),
            jax.ShapeDtypeStruct((TOP_K, n), jnp.int32),
            jax.ShapeDtypeStruct((TOP_K, n), F32),
            jax.ShapeDtypeStruct((TOP_K, n), jnp.int32),
            jax.ShapeDtypeStruct((N_EXPERTS, 1), F32),
        ],
        scratch_shapes=[pltpu.VMEM((tm, tm), BF16)],
        compiler_params=_params(("arbitrary",)),
        name="outproj_ln_router",
    )(*mixes, w, x_all, g, b, wr_t, br)


def _moe_kernel(be_ref, nu_ref, x_ref, wgu_ref, bgu_ref, wdn_ref, bdn_ref, o_ref):
    @pl.when(pl.program_id(0) < nu_ref[0])
    def _():
        h = _dot(x_ref[...], wgu_ref[0]) + bgu_ref[0]
        glu = jnp.minimum(h[:, 0:D_EXPERT], SWIGLU_LIMIT)
        lin = jnp.clip(h[:, D_EXPERT:2 * D_EXPERT], -SWIGLU_LIMIT, SWIGLU_LIMIT)
        act = glu * _sigmoid(SWIGLU_ALPHA * glu) * (lin + 1.0)
        o_ref[...] = (_dot(act.astype(BF16), wdn_ref[0]) + bdn_ref[0]).astype(BF16)


def _moe_experts(blk_e, n_used, xg, wgu, bgu, wdn, bdn):
    nb = blk_e.shape[0]
    grid_spec = pltpu.PrefetchScalarGridSpec(
        num_scalar_prefetch=2,
        grid=(nb,),
        in_specs=[
            pl.BlockSpec((MOE_BLOCK, D_MODEL), lambda i, be, nu: (i, 0)),
            pl.BlockSpec((1, D_MODEL, 2 * D_EXPERT), lambda i, be, nu: (be[i], 0, 0)),
            pl.BlockSpec((1, 1, 2 * D_EXPERT), lambda i, be, nu: (be[i], 0, 0)),
            pl.BlockSpec((1, D_EXPERT, D_MODEL), lambda i, be, nu: (be[i], 0, 0)),
            pl.BlockSpec((1, 1, D_MODEL), lambda i, be, nu: (be[i], 0, 0)),
        ],
        out_specs=pl.BlockSpec((MOE_BLOCK, D_MODEL), lambda i, be, nu: (i, 0)),
    )
    return pl.pallas_call(
        _moe_kernel,
        grid_spec=grid_spec,
        out_shape=jax.ShapeDtypeStruct((nb * MOE_BLOCK, D_MODEL), BF16),
        compiler_params=_params(("arbitrary",)),
        name="moe_experts",
    )(blk_e, n_used, xg, wgu, bgu, wdn, bdn)


def _deinterleave_kernel(w_ref, p_ref, o_ref):
    o_ref[0] = _dot(w_ref[0].astype(BF16), p_ref[...]).astype(BF16)


def _deinterleave_gu(w):
    n, d, f2 = w.shape
    col = jnp.arange(f2, dtype=jnp.int32)
    src = jnp.where(col < f2 // 2, 2 * col, 2 * (col - f2 // 2) + 1)
    perm = (col[:, None] == src[None, :]).astype(BF16)
    rows = 512
    return pl.pallas_call(
        _deinterleave_kernel,
        grid=(n, d // rows),
        in_specs=[
            pl.BlockSpec((1, rows, f2), lambda e, r: (e, r, 0)),
            pl.BlockSpec((f2, f2), lambda e, r: (0, 0)),
        ],
        out_specs=pl.BlockSpec((1, rows, f2), lambda e, r: (e, r, 0)),
        out_shape=jax.ShapeDtypeStruct((n, d, f2), BF16),
        compiler_params=_params(("parallel", "parallel")),
        name="deinterleave_gu",
    )(w, perm)


def _combine_ln_kernel(x_ref, r_ref, tg_ref, g_ref, b_ref, o_ref):
    tg = tg_ref[...]
    f = r_ref[0].astype(F32) * tg[:, 0:1]
    for kk in range(1, TOP_K):
        f = f + r_ref[kk].astype(F32) * tg[:, kk:kk + 1]
    o_ref[...] = _layer_norm(DN_ALPHA * x_ref[...] + f, g_ref[...], b_ref[...])


def _combine_ln_into_kernel(x_ref, r_ref, tg_ref, g_ref, b_ref, prev_ref, o_ref):
    del prev_ref
    _combine_ln_kernel(x_ref, r_ref, tg_ref, g_ref, b_ref, o_ref)


def _combine_ln(x1, rows, gates, g, b, into, tile0, n_total):
    n = x1.shape[0]
    tm = TOKEN_TILE
    row = pl.BlockSpec((tm, D_MODEL), lambda i: (i, 0))
    vec = pl.BlockSpec((1, D_MODEL), lambda i: (0, 0))
    in_specs = [row, pl.BlockSpec((TOP_K, tm, D_MODEL), lambda i: (0, i, 0)),
                pl.BlockSpec((tm, TOP_K), lambda i: (i, 0)), vec, vec]
    args = [x1, rows, gates, g, b]
    if into is not None:
        in_specs.append(pl.BlockSpec(memory_space=pl.ANY))
        args.append(into)
    return pl.pallas_call(
        _combine_ln_kernel if into is None else _combine_ln_into_kernel,
        grid=(n // tm,),
        in_specs=in_specs,
        out_specs=pl.BlockSpec((tm, D_MODEL), lambda i: (tile0 + i, 0)),
        out_shape=jax.ShapeDtypeStruct((n_total, D_MODEL), F32),
        input_output_aliases={} if into is None else {len(args) - 1: 0},
        compiler_params=_params(("parallel",)),
        name="combine_ln",
    )(*args)


def _moe_layer(x1, x1b, top_i, top_g, rank, sizes, e0, wgu, bgu, wdn, bdn, g, b, into, tile0, n_total):
    n = x1.shape[0]
    nb = -(-(n * TOP_K) // MOE_BLOCK) + N_EXPERTS
    sizes = sizes[:, 0].astype(jnp.int32)
    nblk = (sizes + MOE_BLOCK - 1) // MOE_BLOCK
    blk_end = jnp.cumsum(nblk)
    pad_starts = (blk_end - nblk) * MOE_BLOCK
    n_used = blk_end[-1]
    blk_ids = jnp.minimum(jnp.arange(nb, dtype=jnp.int32), n_used - 1)
    blk_e = jnp.sum((blk_ids[:, None] >= blk_end[None, :]).astype(jnp.int32), axis=1)
    blk_e = jnp.clip(blk_e, 0, N_EXPERTS - 1)
    experts = jnp.arange(N_EXPERTS, dtype=jnp.int32)
    dest = jnp.sum(jnp.where(top_i[:, :, None] == experts, pad_starts, 0), axis=2) + rank
    tok = jnp.broadcast_to(jnp.arange(n, dtype=jnp.int32)[None], (TOP_K, n))
    tok_of_row = jnp.zeros((nb * MOE_BLOCK,), jnp.int32).at[dest.reshape(-1)].set(
        tok.reshape(-1), unique_indices=True, indices_are_sorted=False)
    xg = x1b.at[tok_of_row].get(mode="promise_in_bounds")
    out = _moe_experts(blk_e + e0, n_used.reshape(1), xg, wgu, bgu, wdn, bdn)
    rows = out.at[dest.reshape(-1)].get(mode="promise_in_bounds").reshape(TOP_K, n, D_MODEL)
    return _combine_ln(x1, rows, jnp.transpose(top_g), g, b, into, tile0, n_total)


def _rel_bucket(rel):
    half = N_BUCKETS // 2
    exact = half // 2
    ret = jnp.where(rel > 0, half, 0)
    n = jnp.abs(rel)
    large = exact + (jnp.log(jnp.maximum(n, 1).astype(F32) / exact)
                     / math.log(MAX_DISTANCE / exact) * (half - exact)).astype(jnp.int32)
    large = jnp.minimum(large, half - 1)
    return ret + jnp.where(n < exact, n, large)


def _near_bias_tiles(rel_table, tq, q_pos0):
    base = q_pos0 % KEY_TILE
    span = max(1, tq // KEY_TILE)
    i = jnp.arange(tq, dtype=jnp.int32)[:, None] + base
    j = jnp.arange(KEY_TILE, dtype=jnp.int32)[None, :]
    tiles = [jnp.full((tq, KEY_TILE), -2 * MAX_DISTANCE, jnp.int32)]
    for d in range(-1, span + 1):
        tiles.append(d * KEY_TILE + j - i)
    rel = jnp.stack(tiles)
    bucket = _rel_bucket(rel)
    out = jnp.zeros((H_B,) + rel.shape, F32)
    for bk in range(N_BUCKETS):
        out = jnp.where(bucket[None] == bk, rel_table[bk].astype(F32)[:, None, None, None], out)
    return out * LOG2_E


def kernel(x_prompt, x_sample, cache_diff_k, cache_diff_v, state_mlstm_C, state_mlstm_n, state_mlstm_m, state_mlstm_conv, cache_sb_k, cache_sb_v, meta_tokens, rel_bias, w_in_ab, w_out_ab, conv_w_a, conv_b_a, w_aq_a, w_ak_a, b_if_a, mh_gain_a, lam_q1, lam_k1, lam_q2, lam_k2, subln_gain_b, w_in_c, w_out_c, ln_g, ln_b, w_router, b_router, w_gu, b_gu, w_down, b_down):
    bp, sp, _ = x_prompt.shape
    bs, ss, _ = x_sample.shape
    lp = N_META + sp
    past = cache_diff_k.shape[3]
    n_p = bp * lp
    n_s = bs * ss
    tm_p = lp // 3 if (lp % 3 == 0 and (lp // 3) % 16 == 0) else lp
    tq_p = 2 * KEY_TILE

    parts = []
    for bi in range(bp):
        parts += [meta_tokens.astype(x_prompt.dtype), x_prompt[bi]]
    x_all = jnp.concatenate(parts + [x_sample.reshape(n_s, D_MODEL)], axis=0)

    j = 0
    lam_init = 0.8 - 0.6 * math.exp(-0.3 * 0)
    w_ab = w_in_ab[j]
    w_perm = jnp.concatenate([
        w_ab[:, 0:OFF_AIF],
        jnp.pad(w_ab[:, OFF_AIF:OFF_BQ], ((0, 0), (0, LANE - 2 * H_A))),
        w_ab[:, OFF_BQ:OFF_BQ + W_B] * (DH_B ** -0.5 * LOG2_E),
        w_ab[:, OFF_BQ + W_B:]], axis=1).astype(BF16)
    bg = jnp.pad(b_if_a[j], (0, LANE - 2 * H_A)).reshape(1, LANE)
    lam = (jnp.exp(jnp.sum(lam_q1[j] * lam_k1[j])) - jnp.exp(jnp.sum(lam_q2[j] * lam_k2[j]))).astype(F32) + lam_init
    lam_v = jnp.broadcast_to(lam.reshape(1, 1), (1, LANE))
    cw = conv_w_a[j]
    cb = conv_b_a[j].reshape(1, W_A)
    wq = w_aq_a[j].astype(BF16)
    wk = w_ak_a[j].astype(BF16)
    gain_a = mh_gain_a[j].reshape(1, W_A)
    gain_b = subln_gain_b[j].reshape(1, 2 * DH_B)

    def ab_group(row0, bsz, seq, tm, lead, state, past_kv, tq, q_pos0, chunked):
        a, g, q, k_new, v_new = _inproj_ab(x_all, row0, bsz, seq, tm, w_perm, bg)
        c0, n0, m0, buf = state
        mix_a, c1, n1, m1, cs = _mlstm(
            a, g, bsz, seq, lead, cw, cb, wq, wk, gain_a,
            c0, n0.reshape(bsz, H_A, 1, DH_A),
            jnp.broadcast_to(m0[:, :, None, None], (bsz, H_A, 1, LANE)), buf)
        if past_kv is None:
            k_all, v_all = k_new, v_new
        else:
            k_all = jnp.concatenate([past_kv[0], k_new], axis=2)
            v_all = jnp.concatenate([past_kv[1], v_new], axis=2)
        near = _near_bias_tiles(rel_bias, tq, q_pos0)
        mix_b = _diff_attention(q.reshape(bsz, seq, W_B), k_all, v_all, near, lam_v, gain_b,
                                tq=tq, q_pos0=q_pos0, chunked=chunked, lam_init=lam_init)
        outs = (k_new, v_new, c1, n1.reshape(bsz, H_A, DH_A), m1[:, :, 0, 0], cs)
        return mix_a, mix_b.reshape(bsz * seq, W_B), outs

    zero_state = (jnp.zeros((bp, H_A, DH_A, DH_A), F32), jnp.zeros((bp, H_A, DH_A), F32),
                  jnp.zeros((bp, H_A), F32), jnp.zeros((bp, CONV_W - 1, W_A), F32))
    mix_a_p, mix_b_p, ab_p = ab_group(0, bp, lp, tm_p, N_META, zero_state, None, tq_p, 0, True)
    s_state = (state_mlstm_C[j], state_mlstm_n[j], state_mlstm_m[j], state_mlstm_conv[j])
    mix_a_s, mix_b_s, ab_s = ab_group(n_p, bs, ss, ss, 0, s_state, (cache_diff_k[j], cache_diff_v[j]),
                                      ss, past, False)
    mix_a = jnp.concatenate([mix_a_p, mix_a_s], axis=0)
    mix_b = jnp.concatenate([mix_b_p, mix_b_s], axis=0)

    n_le = w_gu.shape[0] * N_EXPERTS
    wgu = _deinterleave_gu(w_gu.reshape(n_le, D_MODEL, 2 * D_EXPERT))
    bgu = jnp.concatenate([b_gu[..., 0::2], b_gu[..., 1::2]], axis=-1).reshape(n_le, 1, 2 * D_EXPERT)
    wdn = w_down.reshape(n_le, D_EXPERT, D_MODEL).astype(BF16)
    bdn = b_down.reshape(n_le, 1, D_MODEL)

    def token_stage(layer, mixes, w_out, x_in):
        wr_f = jnp.transpose(w_router[layer])
        wr_hi = wr_f.astype(BF16)
        wr_t = jnp.concatenate([wr_hi, (wr_f - wr_hi.astype(F32)).astype(BF16)], axis=0)
        br = b_router[layer].reshape(N_EXPERTS, 1)
        w_out = w_out.astype(BF16)
        ntile = x_in.shape[0] // TOKEN_TILE
        bounds = [ntile * s // MOE_SPLIT for s in range(MOE_SPLIT + 1)]
        out = None
        for t0, t1 in zip(bounds[:-1], bounds[1:]):
            x1, x1b, top_i, top_g, rank, sizes = _outproj_ln_router(
                mixes, w_out, x_in, ln_g[layer, 0].reshape(1, D_MODEL), ln_b[layer, 0].reshape(1, D_MODEL),
                wr_t, br, t0, t1 - t0)
            out = _moe_layer(x1, x1b, top_i, top_g, rank, sizes, layer * N_EXPERTS, wgu, bgu, wdn, bdn,
                             ln_g[layer, 1].reshape(1, D_MODEL), ln_b[layer, 1].reshape(1, D_MODEL),
                             out, t0, x_in.shape[0])
        return out

    x_all = token_stage(0, [mix_a, mix_b], w_out_ab[j], x_all)

    q_scale = jnp.where(jnp.arange(3 * D_MODEL) < D_MODEL, DH_C ** -0.5 * LOG2_E, 1.0).astype(F32)
    w_c = (w_in_c[j] * q_scale).astype(BF16)

    def c_group(row0, bsz, seq, tm, past_kv, tq, q_pos0):
        q, k_new, v_new = _inproj_c(x_all, row0, bsz, seq, tm, w_c)
        if past_kv is None:
            k_all, v_all = k_new, v_new
        else:
            k_all = jnp.concatenate([past_kv[0], k_new], axis=2)
            v_all = jnp.concatenate([past_kv[1], v_new], axis=2)
        o = _stick_breaking(q.reshape(bsz, seq, D_MODEL), k_all, v_all, tq=tq, q_pos0=q_pos0)
        return o.reshape(bsz * seq, D_MODEL), (k_new, v_new)

    mix_p, c_p = c_group(0, bp, lp, tm_p, None, 2 * KEY_TILE, 0)
    mix_s, c_s = c_group(n_p, bs, ss, ss, (cache_sb_k[j], cache_sb_v[j]), ss, past)
    x_all = token_stage(1, [jnp.concatenate([mix_p, mix_s], axis=0)], w_out_c[j], x_all)

    y_prompt = x_all[:n_p].reshape(bp, lp, D_MODEL)[:, N_META:]
    y_sample = x_all[n_p:].reshape(bs, ss, D_MODEL)
    stack = lambda t: t[None]
    return (y_prompt, y_sample,
            stack(ab_p[0]), stack(ab_p[1]), stack(ab_p[2]), stack(ab_p[3]), stack(ab_p[4]), stack(ab_p[5]),
            stack(c_p[0]), stack(c_p[1]),
            stack(ab_s[0]), stack(ab_s[1]), stack(ab_s[2]), stack(ab_s[3]), stack(ab_s[4]), stack(ab_s[5]),
            stack(c_s[0]), stack(c_s[1]))
```

```python
import functools
import math

import jax
import jax.numpy as jnp
from jax import lax
from jax.experimental import pallas as pl
from jax.experimental.pallas import tpu as pltpu

F32 = jnp.float32
BF16 = jnp.bfloat16

D_MODEL = 1024
DEPTH = 2
CHUNK = 64
N_META = 16
H_A = 4
DH_A = 128
W_A = H_A * DH_A
CONV_W = 4
H_B = 4
DH_B = 64
W_B = H_B * 2 * DH_B
H_C = 16
DH_C = 64
N_BUCKETS = 32
MAX_DISTANCE = 128
N_EXPERTS = 32
TOP_K = 4
D_EXPERT = D_MODEL // 2
SWIGLU_LIMIT = 7.0
SWIGLU_ALPHA = 1.702
DN_ALPHA = (2 * DEPTH) ** 0.25
LN_EPS = 1e-5
OFF_AIF = 3 * W_A
OFF_BQ = OFF_AIF + 2 * H_A

LANE = 128
KEY_TILE = 128
LOG2_KEY_TILE = 7
LOG2_CHUNK = 6
VMEM_LIMIT = 56 * 1024 * 1024
MOE_BLOCK = 512
TOKEN_TILE = 512
SB_GROUP = 8
MOE_SPLIT = 2
NEG_BIG = -1e30
LOG2_E = 1.4426950408889634

COL_G = 3 * W_A
COL_Q = COL_G + LANE
COL_K = COL_Q + W_B
COL_V = COL_K + W_B
D_IN_AB_PAD = COL_V + W_B


def _dot(a, b):
    return jnp.dot(a, b, preferred_element_type=F32)


def _dot_nt(a, b, precision=None):
    return lax.dot_general(a, b, (((1,), (1,)), ((), ())), preferred_element_type=F32, precision=precision)


def _dot_tn(a, b):
    return lax.dot_general(a, b, (((0,), (0,)), ((), ())), preferred_element_type=F32)


def _log_sigmoid(x):
    return jnp.minimum(x, 0.0) - jnp.log(1.0 + jnp.exp(-jnp.abs(x)))


def _sigmoid(x):
    return 1.0 / (1.0 + jnp.exp(-x))


def _chunk_id(pos):
    return (pos + (CHUNK - N_META)) >> LOG2_CHUNK


def _params(sem):
    return pltpu.CompilerParams(dimension_semantics=sem, vmem_limit_bytes=VMEM_LIMIT)


def _inproj_ab_kernel(x_ref, w_ref, bg_ref, a_ref, g_ref, q_ref, k_ref, v_ref):
    xb = x_ref[...].astype(BF16)
    a_ref[...] = _dot(xb, w_ref[:, 0:COL_G])
    g_ref[...] = _dot(xb, w_ref[:, COL_G:COL_Q]) + bg_ref[...]
    q_ref[...] = _dot(xb, w_ref[:, COL_Q:COL_K]).astype(BF16)
    for h in range(H_B):
        k_ref[0, h] = _dot(xb, w_ref[:, COL_K + 2 * DH_B * h:COL_K + 2 * DH_B * (h + 1)])
        v_ref[0, h] = _dot(xb, w_ref[:, COL_V + 2 * DH_B * h:COL_V + 2 * DH_B * (h + 1)])


def _inproj_ab(x_all, row0, bsz, seq, tm, w, bg):
    nrt = seq // tm
    n = bsz * seq
    if row0 % tm:
        x_all, row0 = x_all[row0:row0 + n], 0
    off = row0 // tm
    return pl.pallas_call(
        _inproj_ab_kernel,
        grid=(bsz, nrt),
        in_specs=[
            pl.BlockSpec((tm, D_MODEL), lambda b, r: (off + b * nrt + r, 0)),
            pl.BlockSpec((D_MODEL, D_IN_AB_PAD), lambda b, r: (0, 0)),
            pl.BlockSpec((1, LANE), lambda b, r: (0, 0)),
        ],
        out_specs=[
            pl.BlockSpec((tm, COL_G), lambda b, r: (b * nrt + r, 0)),
            pl.BlockSpec((tm, LANE), lambda b, r: (b * nrt + r, 0)),
            pl.BlockSpec((tm, W_B), lambda b, r: (b * nrt + r, 0)),
            pl.BlockSpec((1, H_B, tm, 2 * DH_B), lambda b, r: (b, 0, r, 0)),
            pl.BlockSpec((1, H_B, tm, 2 * DH_B), lambda b, r: (b, 0, r, 0)),
        ],
        out_shape=[
            jax.ShapeDtypeStruct((n, COL_G), F32),
            jax.ShapeDtypeStruct((n, LANE), F32),
            jax.ShapeDtypeStruct((n, W_B), BF16),
            jax.ShapeDtypeStruct((bsz, H_B, seq, 2 * DH_B), F32),
            jax.ShapeDtypeStruct((bsz, H_B, seq, 2 * DH_B), F32),
        ],
        compiler_params=_params(("parallel", "parallel")),
        name="inproj_ab",
    )(x_all, w, bg)


def _mlstm_kernel(a_ref, g_ref, cw_ref, cb_ref, wq_ref, wk_ref, gain_ref, c0_ref, n0_ref, m0_ref, buf_ref,
                  out_ref, c1_ref, n1_ref, m1_ref, cs_ref, q_s, num_s, rs_s, u_s, nv_s, *, seq, lead):
    cs_ref[0] = a_ref[seq - (CONV_W - 1):seq, 0:W_A]
    c1_ref[...] = c0_ref[...]
    n1_ref[...] = n0_ref[...]
    m1_ref[...] = m0_ref[...]
    sel = (lax.broadcasted_iota(jnp.int32, (8, LANE), 0) == lax.broadcasted_iota(jnp.int32, (8, LANE), 1)).astype(F32)
    lane = lax.broadcasted_iota(jnp.int32, (1, LANE), 1)

    def local(c, r0, lc, first):
        if first:
            win = jnp.concatenate([jnp.zeros((5, W_A), F32), buf_ref[0], a_ref[0:lc, 0:W_A]], axis=0)
        else:
            win = a_ref[pl.ds(pl.multiple_of(r0 - 8, 8), lc + 8), 0:W_A]
        y = cb_ref[...]
        for j in range(CONV_W):
            y = y + win[5 + j:5 + j + lc, :] * cw_ref[j:j + 1, :]
        ca = y * _sigmoid(y)
        g = g_ref[pl.ds(r0, lc), :]
        g_rows = _dot_nt(sel, g, precision=lax.Precision.HIGHEST)
        ti = lax.broadcasted_iota(jnp.int32, (lc, lc), 0)
        si = lax.broadcasted_iota(jnp.int32, (lc, lc), 1)
        causal = si <= ti
        heads = range(H_A)
        hsl = [slice(DH_A * h, DH_A * (h + 1)) for h in heads]
        cab = [ca[:, hsl[h]].astype(BF16) for h in heads]
        qb = [_dot(cab[h], wq_ref[h]).astype(BF16) for h in heads]
        k = [_dot(cab[h], wk_ref[h]) * (DH_A ** -0.5) for h in heads]
        vb = [a_ref[pl.ds(r0, lc), W_A + DH_A * h:W_A + DH_A * (h + 1)].astype(BF16) for h in heads]
        qk = [_dot_nt(qb[h], k[h].astype(BF16)) for h in heads]
        rs = jnp.zeros((lc, LANE), F32)
        s, kw = [], []
        for h in heads:
            ig_c = g[:, h:h + 1]
            lf_c = _log_sigmoid(g[:, H_A + h:H_A + h + 1])
            ig_r = g_rows[h:h + 1, :]
            lf_r = _log_sigmoid(g_rows[H_A + h:H_A + h + 1, :])
            b_c = jnp.sum(jnp.where(causal, lf_r, 0.0), axis=1, keepdims=True)
            b_r = jnp.sum(jnp.where(ti <= si, lf_c, 0.0), axis=0, keepdims=True)
            dmat = jnp.where(causal, b_c - b_r + ig_r, -jnp.inf)
            m_loc = jnp.max(dmat, axis=1, keepdims=True)
            s.append(qk[h] * jnp.exp(dmat - m_loc))
            den_loc = jnp.sum(s[h], axis=1, keepdims=True)
            rs = jnp.where(lane == h, den_loc, rs)
            rs = jnp.where(lane == H_A + h, m_loc, rs)
            rs = jnp.where(lane == 2 * H_A + h, b_c, rs)
            w_end = jnp.exp(b_c[lc - 1:lc, :] - b_c + ig_c - m_loc[lc - 1:lc, :])
            kw.append(k[h] * w_end)
        num = [_dot(s[h].astype(BF16), vb[h]) for h in heads]
        upd = [_dot_tn(kw[h].astype(BF16), vb[h]) for h in heads]
        for h in heads:
            q_s[pl.ds(r0, lc), hsl[h]] = qb[h]
            num_s[pl.ds(r0, lc), hsl[h]] = num[h]
            u_s[c, h] = upd[h]
            nv_s[c, h] = jnp.sum(kw[h], axis=0, keepdims=True)
        rs_s[pl.ds(r0, lc), :] = rs

    def carry(c, r0, lc):
        rs = rs_s[pl.ds(r0, lc), :]
        for h in range(H_A):
            hs = slice(DH_A * h, DH_A * (h + 1))
            den_loc = jnp.broadcast_to(rs[:, h:h + 1], (lc, DH_A))
            m_loc = jnp.broadcast_to(rs[:, H_A + h:H_A + h + 1], (lc, DH_A))
            b_c = jnp.broadcast_to(rs[:, 2 * H_A + h:2 * H_A + h + 1], (lc, DH_A))
            m_prev = m1_ref[0, h]
            c_prev = c1_ref[0, h]
            n_prev = n1_ref[0, h]
            inter = b_c + m_prev
            m_t = jnp.maximum(inter, m_loc)
            gg = jnp.exp(inter - m_t)
            sc = jnp.exp(m_loc - m_t)
            qb = q_s[pl.ds(r0, lc), hs]
            num = sc * num_s[pl.ds(r0, lc), hs] + gg * _dot(qb, c_prev.astype(BF16))
            qn = jnp.broadcast_to(jnp.sum(qb.astype(F32) * n_prev, axis=1, keepdims=True), (lc, DH_A))
            den = sc * den_loc + gg * qn
            hh = num / jnp.maximum(jnp.abs(den), jnp.exp(-m_t))
            m_new = m_t[lc - 1:lc, :]
            decay = jnp.exp(inter[lc - 1:lc, :] - m_new)
            grow = sc[lc - 1:lc, :]
            c1_ref[0, h] = decay * c_prev + grow * u_s[c, h]
            n1_ref[0, h] = decay * n_prev + grow * nv_s[c, h]
            m1_ref[0, h] = m_new
            hn = hh * lax.rsqrt(jnp.mean(hh * hh, axis=1, keepdims=True) + LN_EPS) * gain_ref[:, hs]
            oa = a_ref[pl.ds(r0, lc), 2 * W_A + DH_A * h:2 * W_A + DH_A * (h + 1)]
            out_ref[pl.ds(r0, lc), hs] = (hn * _sigmoid(oa)).astype(BF16)

    first_len = lead if lead else CHUNK
    nrest = (seq - first_len) // CHUNK
    start = lambda i: pl.multiple_of(first_len + i * CHUNK, 16)
    local(0, 0, first_len, True)

    def local_body(i, c):
        local(2 * i + 1, start(2 * i), CHUNK, False)
        local(2 * i + 2, start(2 * i + 1), CHUNK, False)
        return c

    lax.fori_loop(0, nrest // 2, local_body, 0)
    if nrest % 2:
        local(nrest, start(nrest - 1), CHUNK, False)
    carry(0, 0, first_len)

    def carry_body(i, c):
        carry(i + 1, start(i), CHUNK)
        return c

    lax.fori_loop(0, nrest, carry_body, 0)


def _mlstm(a, g, bsz, seq, lead, cw, cb, wq, wk, gain, c0, n0, m0, buf):
    kern = functools.partial(_mlstm_kernel, seq=seq, lead=lead)
    assert seq >= CONV_W - 1
    nchunks = 1 + (seq - (lead if lead else CHUNK)) // CHUNK
    full = lambda *shape: pl.BlockSpec(shape, lambda b: (0,) * len(shape))
    per_b = lambda *shape: pl.BlockSpec((1,) + shape, lambda b: (b,) + (0,) * len(shape))
    return pl.pallas_call(
        kern,
        grid=(bsz,),
        in_specs=[
            pl.BlockSpec((seq, COL_G), lambda b: (b, 0)),
            pl.BlockSpec((seq, LANE), lambda b: (b, 0)),
            full(CONV_W, W_A), full(1, W_A), full(H_A, DH_A, DH_A), full(H_A, DH_A, DH_A), full(1, W_A),
            per_b(H_A, DH_A, DH_A), per_b(H_A, 1, DH_A), per_b(H_A, 1, LANE), per_b(CONV_W - 1, W_A),
        ],
        out_specs=[
            pl.BlockSpec((seq, W_A), lambda b: (b, 0)),
            per_b(H_A, DH_A, DH_A), per_b(H_A, 1, DH_A), per_b(H_A, 1, LANE), per_b(CONV_W - 1, W_A),
        ],
        out_shape=[
            jax.ShapeDtypeStruct((bsz * seq, W_A), BF16),
            jax.ShapeDtypeStruct((bsz, H_A, DH_A, DH_A), F32),
            jax.ShapeDtypeStruct((bsz, H_A, 1, DH_A), F32),
            jax.ShapeDtypeStruct((bsz, H_A, 1, LANE), F32),
            jax.ShapeDtypeStruct((bsz, CONV_W - 1, W_A), F32),
        ],
        scratch_shapes=[
            pltpu.VMEM((seq, W_A), BF16),
            pltpu.VMEM((seq, W_A), F32),
            pltpu.VMEM((seq, LANE), F32),
            pltpu.VMEM((nchunks, H_A, DH_A, DH_A), F32),
            pltpu.VMEM((nchunks, H_A, 1, DH_A), F32),
        ],
        compiler_params=_params(("parallel",)),
        name="mlstm",
    )(a, g, cw, cb, wq, wk, gain, c0, n0, m0, buf)


def _diff_kernel(lam_ref, q_ref, k_ref, v_ref, near_ref, gain_ref, o_ref, kb_ref, vb_ref, s_ref, mx_ref, l_ref,
                 acc_ref, *, tq, lq, lk, q_pos0, chunked, lam_init):
    qi = pl.program_id(2)
    nq = pl.num_programs(2)
    rows_k = kb_ref.shape[0]

    @pl.when(qi == 0)
    def _():
        kb_ref[0:lk, :] = k_ref[0, 0].astype(BF16)
        vb_ref[0:lk, :] = v_ref[0, 0].astype(BF16)
        kb_ref[lk:, :] = jnp.zeros((rows_k - lk, 2 * DH_B), BF16)
        vb_ref[lk:, :] = jnp.zeros((rows_k - lk, 2 * DH_B), BF16)

    q0 = q_pos0 + qi * tq
    qt = q0 >> LOG2_KEY_TILE
    lane = lax.broadcasted_iota(jnp.int32, (1, 2 * DH_B), 1)
    bias_far = near_ref[0, 0, 0:1, 0:1]
    span = near_ref.shape[1] - 3

    def process(rows):
        r2 = 2 * rows
        qf = q_ref[0, 0:rows, :].astype(F32)
        qs = jnp.concatenate([jnp.where(lane < DH_B, qf, 0.0), jnp.where(lane >= DH_B, qf, 0.0)],
                             axis=0).astype(BF16)
        mx_ref[0:r2, :] = jnp.full((r2, KEY_TILE), NEG_BIG, F32)

        def step(j0, width, mode):
            ks = pl.multiple_of(j0 * KEY_TILE, KEY_TILE)
            kt = kb_ref[pl.ds(ks, width * KEY_TILE), :]
            s = _dot_nt(qs, kt)
            if mode == "far":
                s = s + bias_far
            else:
                first = span + 3 - width
                bias = jnp.concatenate([near_ref[0, first + g, 0:rows, :] for g in range(width)], axis=1)
                kpos = ks + lax.broadcasted_iota(jnp.int32, (rows, width * KEY_TILE), 1)
                if chunked:
                    qpos = q0 + lax.broadcasted_iota(jnp.int32, (rows, width * KEY_TILE), 0)
                    mask = _chunk_id(kpos) <= _chunk_id(qpos)
                else:
                    mask = kpos < lk
                s = jnp.where(jnp.concatenate([mask, mask], axis=0),
                              s + jnp.concatenate([bias, bias], axis=0), NEG_BIG)
            mx = mx_ref[0:r2, :]
            for g in range(width):
                blk = s[:, g * KEY_TILE:(g + 1) * KEY_TILE]
                s_ref[j0 + g, 0:r2, :] = blk
                mx = jnp.maximum(mx, blk)
            mx_ref[0:r2, :] = mx

        @pl.when(qt >= 1)
        def _():
            step(qt - 1, span + 2, "near")

        @pl.when(qt == 0)
        def _():
            step(0, span + 1, "near")

        def for_tiles(count, fn):
            def body(g, carry):
                fn(8 * g, 8)
                return carry

            lax.fori_loop(0, count >> 3, body, 0)
            base = (count >> 3) << 3
            for w in range(1, 8):
                @pl.when((count & 7) == w)
                def _():
                    fn(base, w)

        for_tiles(jnp.maximum(qt - 1, 0), lambda j0, w: step(j0, w, "far"))
        m = jnp.max(mx_ref[0:r2, :], axis=1, keepdims=True)
        l_ref[0:r2, :] = jnp.zeros((r2, KEY_TILE), F32)
        acc_ref[0:r2, :] = jnp.zeros((r2, 2 * DH_B), F32)

        def weights(j0, width):
            vt = vb_ref[pl.ds(pl.multiple_of(j0 * KEY_TILE, KEY_TILE), width * KEY_TILE), :]
            lsum = l_ref[0:r2, :]
            ps = []
            for t in range(width):
                p = jnp.exp2(s_ref[j0 + t, 0:r2, :] - m)
                lsum = lsum + p
                ps.append(p.astype(BF16))
            l_ref[0:r2, :] = lsum
            acc_ref[0:r2, :] += _dot(ps[0] if width == 1 else jnp.concatenate(ps, axis=1), vt)

        for_tiles(qt + span + 1, weights)
        o = acc_ref[0:r2, :] / jnp.sum(l_ref[0:r2, :], axis=1, keepdims=True)
        lam = lam_ref[:, 0:1]
        o = o[0:rows] - lam * o[rows:2 * rows]
        o = o * lax.rsqrt(jnp.mean(o * o, axis=1, keepdims=True) + LN_EPS) * gain_ref[...] * (1.0 - lam_init)
        o_ref[0, 0:rows, :] = o.astype(BF16)

    tail = lq % tq
    if tail == 0:
        process(tq)
    else:
        @pl.when(qi < nq - 1)
        def _():
            process(tq)

        @pl.when(qi == nq - 1)
        def _():
            process(tail)


def _diff_attention(q, k, v, near, lam, gain, *, tq, q_pos0, chunked, lam_init):
    bsz, lq, _ = q.shape
    lk = k.shape[2]
    nq = -(-lq // tq)
    nkt = -(-lk // KEY_TILE)
    span = near.shape[1] - 3
    kern = functools.partial(_diff_kernel, tq=tq, lq=lq, lk=lk, q_pos0=q_pos0, chunked=chunked, lam_init=lam_init)
    return pl.pallas_call(
        kern,
        grid=(bsz, H_B, nq),
        in_specs=[
            pl.BlockSpec((1, LANE), lambda b, h, i: (0, 0)),
            pl.BlockSpec((1, tq, 2 * DH_B), lambda b, h, i: (b, i, h)),
            pl.BlockSpec((1, 1, lk, 2 * DH_B), lambda b, h, i: (b, h, 0, 0)),
            pl.BlockSpec((1, 1, lk, 2 * DH_B), lambda b, h, i: (b, h, 0, 0)),
            pl.BlockSpec((1, span + 3, tq, KEY_TILE), lambda b, h, i: (h, 0, 0, 0)),
            pl.BlockSpec((1, 2 * DH_B), lambda b, h, i: (0, 0)),
        ],
        out_specs=pl.BlockSpec((1, tq, 2 * DH_B), lambda b, h, i: (b, i, h)),
        out_shape=jax.ShapeDtypeStruct((bsz, lq, W_B), BF16),
        scratch_shapes=[
            pltpu.VMEM(((nkt + span) * KEY_TILE, 2 * DH_B), BF16),
            pltpu.VMEM(((nkt + span) * KEY_TILE, 2 * DH_B), BF16),
            pltpu.VMEM((nkt + span, 2 * tq, KEY_TILE), F32),
            pltpu.VMEM((2 * tq, KEY_TILE), F32),
            pltpu.VMEM((2 * tq, KEY_TILE), F32),
            pltpu.VMEM((2 * tq, 2 * DH_B), F32),
        ],
        compiler_params=_params(("parallel", "parallel", "arbitrary")),
        name="diff_attention",
    )(lam, q, k, v, near, gain)


def _inproj_c_kernel(x_ref, w_ref, q_ref, k_ref, v_ref):
    xb = x_ref[...].astype(BF16)
    q_ref[...] = _dot(xb, w_ref[:, 0:D_MODEL]).astype(BF16)
    yk = _dot(xb, w_ref[:, D_MODEL:2 * D_MODEL])
    for h in range(H_C):
        k_ref[0, h] = yk[:, DH_C * h:DH_C * (h + 1)]
    yv = _dot(xb, w_ref[:, 2 * D_MODEL:3 * D_MODEL])
    for h in range(H_C):
        v_ref[0, h] = yv[:, DH_C * h:DH_C * (h + 1)]


def _inproj_c(x_all, row0, bsz, seq, tm, w):
    nrt = seq // tm
    if row0 % tm:
        x_all, row0 = x_all[row0:row0 + bsz * seq], 0
    off = row0 // tm
    return pl.pallas_call(
        _inproj_c_kernel,
        grid=(bsz, nrt),
        in_specs=[
            pl.BlockSpec((tm, D_MODEL), lambda b, r: (off + b * nrt + r, 0)),
            pl.BlockSpec((D_MODEL, 3 * D_MODEL), lambda b, r: (0, 0)),
        ],
        out_specs=[
            pl.BlockSpec((tm, D_MODEL), lambda b, r: (b * nrt + r, 0)),
            pl.BlockSpec((1, H_C, tm, DH_C), lambda b, r: (b, 0, r, 0)),
            pl.BlockSpec((1, H_C, tm, DH_C), lambda b, r: (b, 0, r, 0)),
        ],
        out_shape=[
            jax.ShapeDtypeStruct((bsz * seq, D_MODEL), BF16),
            jax.ShapeDtypeStruct((bsz, H_C, seq, DH_C), F32),
            jax.ShapeDtypeStruct((bsz, H_C, seq, DH_C), F32),
        ],
        compiler_params=_params(("parallel", "parallel")),
        name="inproj_c",
    )(x_all, w)


def _sb_kernel(q_ref, k_ref, v_ref, o_ref, kb_ref, vb_ref, acc_ref, run_ref, u_ref, t_ref, hl_ref,
               *, tq, lq, lk, q_pos0):
    qi = pl.program_id(2)
    nq = pl.num_programs(2)
    rows_k = kb_ref.shape[0]

    @pl.when(qi == 0)
    def _():
        kb_ref[0:lk, :] = jnp.concatenate([k_ref[0, 0], k_ref[0, 1]], axis=1).astype(BF16)
        vb_ref[0:lk, :] = jnp.concatenate([v_ref[0, 0], v_ref[0, 1]], axis=1).astype(BF16)
        if rows_k > lk:
            kb_ref[lk:, :] = jnp.zeros((rows_k - lk, 2 * DH_C), BF16)
            vb_ref[lk:, :] = jnp.zeros((rows_k - lk, 2 * DH_C), BF16)

    q0 = q_pos0 + qi * tq
    jd = q0 >> LOG2_KEY_TILE
    jj = lax.broadcasted_iota(jnp.int32, (2 * KEY_TILE, 2 * KEY_TILE), 0) & (KEY_TILE - 1)
    ss = lax.broadcasted_iota(jnp.int32, (2 * KEY_TILE, 2 * KEY_TILE), 1)
    later = jnp.where((jj > ss) | (ss >= KEY_TILE), 1.0, 0.0).astype(BF16)
    lane = lax.broadcasted_iota(jnp.int32, (1, 2 * DH_C), 1)

    def process(rows):
        qf = q_ref[0, 0:rows, :].astype(F32)
        qh = [jnp.where(lane < DH_C, qf, 0.0).astype(BF16), jnp.where(lane >= DH_C, qf, 0.0).astype(BF16)]
        acc_ref[:, 0:rows, :] = jnp.zeros((2, rows, 2 * DH_C), F32)
        run_ref[:, 0:rows, :] = jnp.zeros((2, rows, KEY_TILE), F32)

        def stage_scores(j0, width, masked):
            ks = pl.multiple_of(j0 * KEY_TILE, KEY_TILE)
            kt = kb_ref[pl.ds(ks, width * KEY_TILE), :]
            for h in range(2):
                z = _dot_nt(qh[h], kt)
                sp = jnp.maximum(z, 0.0) + jnp.log(1.0 + jnp.exp2(-jnp.abs(z))) * LOG2_E
                u = z - sp
                if masked:
                    mask = (ks + lax.broadcasted_iota(jnp.int32, (rows, width * KEY_TILE), 1)) < (
                        q0 + lax.broadcasted_iota(jnp.int32, (rows, width * KEY_TILE), 0))
                    sp = jnp.where(mask, sp, 0.0)
                    u = jnp.where(mask, u, NEG_BIG)
                hi = sp.astype(BF16)
                lo = (sp - hi.astype(F32)).astype(BF16)
                for g in range(width):
                    cols = slice(g * KEY_TILE, (g + 1) * KEY_TILE)
                    u_ref[h, j0 + g, 0:rows, :] = u[:, cols]
                    hl_ref[h, j0 + g, 0:rows, :] = jnp.concatenate([hi[:, cols], lo[:, cols]], axis=1)

        def stage_weights(j0, width):
            ks = pl.multiple_of(j0 * KEY_TILE, KEY_TILE)
            vt = vb_ref[pl.ds(ks, width * KEY_TILE), :]
            for h in range(2):
                off = run_ref[h, 0:rows, :]
                parts = [None] * width
                for g in reversed(range(width)):
                    parts[g] = jnp.exp2(u_ref[h, j0 + g, 0:rows, :] - off).astype(BF16)
                    off = off + t_ref[h, j0 + g, 0:rows, :]
                a_all = parts[0] if width == 1 else jnp.concatenate(parts, axis=1)
                acc_ref[h, 0:rows, :] += _dot(a_all, vt)
                run_ref[h, 0:rows, :] = off

        def stage_sums(j0, width):
            for h in range(2):
                hl = hl_ref[h, pl.ds(j0, width), 0:rows, :].reshape(width * rows, 2 * KEY_TILE)
                cs = _dot(hl, later)
                for g in range(width):
                    blk = cs[g * rows:(g + 1) * rows]
                    u_ref[h, j0 + g, 0:rows, :] = u_ref[h, j0 + g, 0:rows, :] - blk[:, 0:KEY_TILE]
                    t_ref[h, j0 + g, 0:rows, :] = blk[:, KEY_TILE:2 * KEY_TILE]

        nfull = jd // SB_GROUP
        dgrp = nfull * SB_GROUP

        def scores_body(g, carry):
            stage_scores(SB_GROUP * g, SB_GROUP, False)
            return carry

        lax.fori_loop(0, nfull, scores_body, 0)

        def sums_body(g, carry):
            stage_sums(SB_GROUP * g, SB_GROUP)
            return carry

        lax.fori_loop(0, nfull, sums_body, 0)
        jlast = (q0 + rows - 1) >> LOG2_KEY_TILE
        for w in range(1, SB_GROUP + 1):
            @pl.when(jlast - dgrp == w - 1)
            def _():
                stage_scores(dgrp, w, True)
                stage_sums(dgrp, w)
                stage_weights(dgrp, w)

        def weights_body(g, carry):
            stage_weights(SB_GROUP * (nfull - 1 - g), SB_GROUP)
            return carry

        lax.fori_loop(0, nfull, weights_body, 0)

        o_ref[0, 0:rows, :] = jnp.where(lane < DH_C, acc_ref[0, 0:rows, :], acc_ref[1, 0:rows, :]).astype(BF16)

    tail = lq % tq
    if tail == 0:
        process(tq)
    else:
        @pl.when(qi < nq - 1)
        def _():
            process(tq)

        @pl.when(qi == nq - 1)
        def _():
            process(tail)


def _stick_breaking(q, k, v, *, tq, q_pos0):
    bsz, lq, _ = q.shape
    lk = k.shape[2]
    nq = -(-lq // tq)
    nkt = SB_GROUP * (-(-lk // (SB_GROUP * KEY_TILE)))
    kern = functools.partial(_sb_kernel, tq=tq, lq=lq, lk=lk, q_pos0=q_pos0)
    return pl.pallas_call(
        kern,
        grid=(bsz, H_C // 2, nq),
        in_specs=[
            pl.BlockSpec((1, tq, 2 * DH_C), lambda b, h, i: (b, i, h)),
            pl.BlockSpec((1, 2, lk, DH_C), lambda b, h, i: (b, h, 0, 0)),
            pl.BlockSpec((1, 2, lk, DH_C), lambda b, h, i: (b, h, 0, 0)),
        ],
        out_specs=pl.BlockSpec((1, tq, 2 * DH_C), lambda b, h, i: (b, i, h)),
        out_shape=jax.ShapeDtypeStruct((bsz, lq, D_MODEL), BF16),
        scratch_shapes=[
            pltpu.VMEM((nkt * KEY_TILE, 2 * DH_C), BF16),
            pltpu.VMEM((nkt * KEY_TILE, 2 * DH_C), BF16),
            pltpu.VMEM((2, tq, 2 * DH_C), F32),
            pltpu.VMEM((2, tq, KEY_TILE), F32),
            pltpu.VMEM((2, nkt, tq, KEY_TILE), F32),
            pltpu.VMEM((2, nkt, tq, KEY_TILE), F32),
            pltpu.VMEM((2, nkt, tq, 2 * KEY_TILE), BF16),
        ],
        compiler_params=_params(("parallel", "parallel", "arbitrary")),
        name="stick_breaking",
    )(q, k, v)


def _layer_norm(z, g, b):
    mu = jnp.mean(z, axis=1, keepdims=True)
    zc = z - mu
    var = jnp.mean(zc * zc, axis=1, keepdims=True)
    return zc * lax.rsqrt(var + LN_EPS) * g + b


def _outproj_kernel(*refs, n_in):
    mix_refs = refs[:n_in]
    (w_ref, x_ref, g_ref, b_ref, wr_ref, br_ref,
     x1_ref, x1b_ref, ti_ref, tg_ref, rk_ref, cnt_ref, tri_ref) = refs[n_in:]
    i = pl.program_id(0)
    tm = x_ref.shape[0]

    @pl.when(i == 0)
    def _():
        cnt_ref[...] = jnp.zeros_like(cnt_ref)
        tri_ref[...] = (lax.broadcasted_iota(jnp.int32, (tm, tm), 0)
                        < lax.broadcasted_iota(jnp.int32, (tm, tm), 1)).astype(BF16)

    y = None
    c0 = 0
    for r in mix_refs:
        wd = r.shape[1]
        part = _dot(r[...], w_ref[c0:c0 + wd, :])
        y = part if y is None else y + part
        c0 += wd
    x1 = _layer_norm(DN_ALPHA * x_ref[...] + y, g_ref[...], b_ref[...])
    x1_ref[...] = x1
    x1h = x1.astype(BF16)
    x1b_ref[...] = x1h
    x1l = (x1 - x1h.astype(F32)).astype(BF16)
    both = _dot_nt(wr_ref[...], x1h)
    logits = (both[0:N_EXPERTS] + both[N_EXPERTS:2 * N_EXPERTS]
              + _dot_nt(wr_ref[0:N_EXPERTS, :], x1l) + br_ref[...])
    ei = lax.broadcasted_iota(jnp.int32, (N_EXPERTS, tm), 0)
    cur = logits
    vals, idxs = [], []
    for _k in range(TOP_K):
        mx = jnp.max(cur, axis=0, keepdims=True)
        ix = jnp.min(jnp.where(cur == mx, ei, N_EXPERTS), axis=0, keepdims=True)
        vals.append(mx)
        idxs.append(ix)
        cur = jnp.where(ei == ix, -jnp.inf, cur)
    es = [jnp.exp(vv - vals[0]) for vv in vals]
    tot = es[0] + es[1] + es[2] + es[3]
    ti_ref[...] = jnp.concatenate(idxs, axis=0)
    tg_ref[...] = jnp.concatenate([e / tot for e in es], axis=0)
    run = cnt_ref[...]
    ranks = []
    for kk in range(TOP_K):
        oh = (ei == idxs[kk])
        before = _dot(oh.astype(BF16), tri_ref[...])
        ranks.append(jnp.sum(jnp.where(oh, before + run, 0.0), axis=0, keepdims=True))
        run = run + jnp.sum(oh.astype(F32), axis=1, keepdims=True)
    rk_ref[...] = jnp.concatenate(ranks, axis=0).astype(jnp.int32)
    cnt_ref[...] = run


def _outproj_ln_router(mixes, w, x_all, g, b, wr_t, br, tile0, ntiles):
    tm = TOKEN_TILE
    n = ntiles * tm
    kern = functools.partial(_outproj_kernel, n_in=len(mixes))
    row_in = lambda wd: pl.BlockSpec((tm, wd), lambda i: (tile0 + i, 0))
    row = lambda wd: pl.BlockSpec((tm, wd), lambda i: (i, 0))
    full = lambda *shape: pl.BlockSpec(shape, lambda i: (0,) * len(shape))
    lanes = lambda rows: pl.BlockSpec((rows, tm), lambda i: (0, i))
    return pl.pallas_call(
        kern,
        grid=(ntiles,),
        in_specs=[row_in(m.shape[1]) for m in mixes] + [
            full(D_MODEL, D_MODEL), row_in(D_MODEL), full(1, D_MODEL), full(1, D_MODEL),
            full(2 * N_EXPERTS, D_MODEL), full(N_EXPERTS, 1)],
        out_specs=[row(D_MODEL), row(D_MODEL), lanes(TOP_K), lanes(TOP_K), lanes(TOP_K), full(N_EXPERTS, 1)],
        out_shape=[
            jax.ShapeDtypeStruct((n, D_MODEL), F32),
            jax.ShapeDtypeStruct((n, D_MODEL), BF16),
            jax.ShapeDtypeStruct((TOP_K, n), jnp.int32),
            jax.ShapeDtypeStruct((TOP_K, n), F32),
            jax.ShapeDtypeStruct((TOP_K, n), jnp.int32),
            jax.ShapeDtypeStruct((N_EXPERTS, 1), F32),
        ],
        scratch_shapes=[pltpu.VMEM((tm, tm), BF16)],
        compiler_params=_params(("arbitrary",)),
        name="outproj_ln_router",
    )(*mixes, w, x_all, g, b, wr_t, br)


def _moe_kernel(be_ref, nu_ref, x_ref, wgu_ref, bgu_ref, wdn_ref, bdn_ref, o_ref):
    @pl.when(pl.program_id(0) < nu_ref[0])
    def _():
        h = _dot(x_ref[...], wgu_ref[0]) + bgu_ref[0]
        glu = jnp.minimum(h[:, 0:D_EXPERT], SWIGLU_LIMIT)
        lin = jnp.clip(h[:, D_EXPERT:2 * D_EXPERT], -SWIGLU_LIMIT, SWIGLU_LIMIT)
        act = glu * _sigmoid(SWIGLU_ALPHA * glu) * (lin + 1.0)
        o_ref[...] = (_dot(act.astype(BF16), wdn_ref[0]) + bdn_ref[0]).astype(BF16)


def _moe_experts(blk_e, n_used, xg, wgu, bgu, wdn, bdn):
    nb = blk_e.shape[0]
    grid_spec = pltpu.PrefetchScalarGridSpec(
        num_scalar_prefetch=2,
        grid=(nb,),
        in_specs=[
            pl.BlockSpec((MOE_BLOCK, D_MODEL), lambda i, be, nu: (i, 0)),
            pl.BlockSpec((1, D_MODEL, 2 * D_EXPERT), lambda i, be, nu: (be[i], 0, 0)),
            pl.BlockSpec((1, 1, 2 * D_EXPERT), lambda i, be, nu: (be[i], 0, 0)),
            pl.BlockSpec((1, D_EXPERT, D_MODEL), lambda i, be, nu: (be[i], 0, 0)),
            pl.BlockSpec((1, 1, D_MODEL), lambda i, be, nu: (be[i], 0, 0)),
        ],
        out_specs=pl.BlockSpec((MOE_BLOCK, D_MODEL), lambda i, be, nu: (i, 0)),
    )
    return pl.pallas_call(
        _moe_kernel,
        grid_spec=grid_spec,
        out_shape=jax.ShapeDtypeStruct((nb * MOE_BLOCK, D_MODEL), BF16),
        compiler_params=_params(("arbitrary",)),
        name="moe_experts",
    )(blk_e, n_used, xg, wgu, bgu, wdn, bdn)


def _deinterleave_kernel(w_ref, p_ref, o_ref):
    o_ref[0] = _dot(w_ref[0].astype(BF16), p_ref[...]).astype(BF16)


def _deinterleave_gu(w):
    n, d, f2 = w.shape
    col = jnp.arange(f2, dtype=jnp.int32)
    src = jnp.where(col < f2 // 2, 2 * col, 2 * (col - f2 // 2) + 1)
    perm = (col[:, None] == src[None, :]).astype(BF16)
    rows = 512
    return pl.pallas_call(
        _deinterleave_kernel,
        grid=(n, d // rows),
        in_specs=[
            pl.BlockSpec((1, rows, f2), lambda e, r: (e, r, 0)),
            pl.BlockSpec((f2, f2), lambda e, r: (0, 0)),
        ],
        out_specs=pl.BlockSpec((1, rows, f2), lambda e, r: (e, r, 0)),
        out_shape=jax.ShapeDtypeStruct((n, d, f2), BF16),
        compiler_params=_params(("parallel", "parallel")),
        name="deinterleave_gu",
    )(w, perm)


def _combine_ln_kernel(x_ref, r_ref, tg_ref, g_ref, b_ref, o_ref):
    tg = tg_ref[...]
    f = r_ref[0].astype(F32) * tg[:, 0:1]
    for kk in range(1, TOP_K):
        f = f + r_ref[kk].astype(F32) * tg[:, kk:kk + 1]
    o_ref[...] = _layer_norm(DN_ALPHA * x_ref[...] + f, g_ref[...], b_ref[...])


def _combine_ln_into_kernel(x_ref, r_ref, tg_ref, g_ref, b_ref, prev_ref, o_ref):
    del prev_ref
    _combine_ln_kernel(x_ref, r_ref, tg_ref, g_ref, b_ref, o_ref)


def _combine_ln(x1, rows, gates, g, b, into, tile0, n_total):
    n = x1.shape[0]
    tm = TOKEN_TILE
    row = pl.BlockSpec((tm, D_MODEL), lambda i: (i, 0))
    vec = pl.BlockSpec((1, D_MODEL), lambda i: (0, 0))
    in_specs = [row, pl.BlockSpec((TOP_K, tm, D_MODEL), lambda i: (0, i, 0)),
                pl.BlockSpec((tm, TOP_K), lambda i: (i, 0)), vec, vec]
    args = [x1, rows, gates, g, b]
    if into is not None:
        in_specs.append(pl.BlockSpec(memory_space=pl.ANY))
        args.append(into)
    return pl.pallas_call(
        _combine_ln_kernel if into is None else _combine_ln_into_kernel,
        grid=(n // tm,),
        in_specs=in_specs,
        out_specs=pl.BlockSpec((tm, D_MODEL), lambda i: (tile0 + i, 0)),
        out_shape=jax.ShapeDtypeStruct((n_total, D_MODEL), F32),
        input_output_aliases={} if into is None else {len(args) - 1: 0},
        compiler_params=_params(("parallel",)),
        name="combine_ln",
    )(*args)


def _moe_layer(x1, x1b, top_i, top_g, rank, sizes, e0, wgu, bgu, wdn, bdn, g, b, into, tile0, n_total):
    n = x1.shape[0]
    nb = -(-(n * TOP_K) // MOE_BLOCK) + N_EXPERTS
    sizes = sizes[:, 0].astype(jnp.int32)
    nblk = (sizes + MOE_BLOCK - 1) // MOE_BLOCK
    blk_end = jnp.cumsum(nblk)
    pad_starts = (blk_end - nblk) * MOE_BLOCK
    n_used = blk_end[-1]
    blk_ids = jnp.minimum(jnp.arange(nb, dtype=jnp.int32), n_used - 1)
    blk_e = jnp.sum((blk_ids[:, None] >= blk_end[None, :]).astype(jnp.int32), axis=1)
    blk_e = jnp.clip(blk_e, 0, N_EXPERTS - 1)
    experts = jnp.arange(N_EXPERTS, dtype=jnp.int32)
    dest = jnp.sum(jnp.where(top_i[:, :, None] == experts, pad_starts, 0), axis=2) + rank
    tok = jnp.broadcast_to(jnp.arange(n, dtype=jnp.int32)[None], (TOP_K, n))
    tok_of_row = jnp.zeros((nb * MOE_BLOCK,), jnp.int32).at[dest.reshape(-1)].set(
        tok.reshape(-1), unique_indices=True, indices_are_sorted=False)
    xg = x1b.at[tok_of_row].get(mode="promise_in_bounds")
    out = _moe_experts(blk_e + e0, n_used.reshape(1), xg, wgu, bgu, wdn, bdn)
    rows = out.at[dest.reshape(-1)].get(mode="promise_in_bounds").reshape(TOP_K, n, D_MODEL)
    return _combine_ln(x1, rows, jnp.transpose(top_g), g, b, into, tile0, n_total)


def _rel_bucket(rel):
    half = N_BUCKETS // 2
    exact = half // 2
    ret = jnp.where(rel > 0, half, 0)
    n = jnp.abs(rel)
    large = exact + (jnp.log(jnp.maximum(n, 1).astype(F32) / exact)
                     / math.log(MAX_DISTANCE / exact) * (half - exact)).astype(jnp.int32)
    large = jnp.minimum(large, half - 1)
    return ret + jnp.where(n < exact, n, large)


def _near_bias_tiles(rel_table, tq, q_pos0):
    base = q_pos0 % KEY_TILE
    span = max(1, tq // KEY_TILE)
    i = jnp.arange(tq, dtype=jnp.int32)[:, None] + base
    j = jnp.arange(KEY_TILE, dtype=jnp.int32)[None, :]
    tiles = [jnp.full((tq, KEY_TILE), -2 * MAX_DISTANCE, jnp.int32)]
    for d in range(-1, span + 1):
        tiles.append(d * KEY_TILE + j - i)
    rel = jnp.stack(tiles)
    bucket = _rel_bucket(rel)
    out = jnp.zeros((H_B,) + rel.shape, F32)
    for bk in range(N_BUCKETS):
        out = jnp.where(bucket[None] == bk, rel_table[bk].astype(F32)[:, None, None, None], out)
    return out * LOG2_E


def kernel(x_prompt, x_sample, cache_diff_k, cache_diff_v, state_mlstm_C, state_mlstm_n, state_mlstm_m, state_mlstm_conv, cache_sb_k, cache_sb_v, meta_tokens, rel_bias, w_in_ab, w_out_ab, conv_w_a, conv_b_a, w_aq_a, w_ak_a, b_if_a, mh_gain_a, lam_q1, lam_k1, lam_q2, lam_k2, subln_gain_b, w_in_c, w_out_c, ln_g, ln_b, w_router, b_router, w_gu, b_gu, w_down, b_down):
    bp, sp, _ = x_prompt.shape
    bs, ss, _ = x_sample.shape
    lp = N_META + sp
    past = cache_diff_k.shape[3]
    n_p = bp * lp
    n_s = bs * ss
    tm_p = lp // 3 if (lp % 3 == 0 and (lp // 3) % 16 == 0) else lp
    tq_p = 2 * KEY_TILE

    parts = []
    for bi in range(bp):
        parts += [meta_tokens.astype(x_prompt.dtype), x_prompt[bi]]
    x_all = jnp.concatenate(parts + [x_sample.reshape(n_s, D_MODEL)], axis=0)

    j = 0
    lam_init = 0.8 - 0.6 * math.exp(-0.3 * 0)
    w_ab = w_in_ab[j]
    w_perm = jnp.concatenate([
        w_ab[:, 0:OFF_AIF],
        jnp.pad(w_ab[:, OFF_AIF:OFF_BQ], ((0, 0), (0, LANE - 2 * H_A))),
        w_ab[:, OFF_BQ:OFF_BQ + W_B] * (DH_B ** -0.5 * LOG2_E),
        w_ab[:, OFF_BQ + W_B:]], axis=1).astype(BF16)
    bg = jnp.pad(b_if_a[j], (0, LANE - 2 * H_A)).reshape(1, LANE)
    lam = (jnp.exp(jnp.sum(lam_q1[j] * lam_k1[j])) - jnp.exp(jnp.sum(lam_q2[j] * lam_k2[j]))).astype(F32) + lam_init
    lam_v = jnp.broadcast_to(lam.reshape(1, 1), (1, LANE))
    cw = conv_w_a[j]
    cb = conv_b_a[j].reshape(1, W_A)
    wq = w_aq_a[j].astype(BF16)
    wk = w_ak_a[j].astype(BF16)
    gain_a = mh_gain_a[j].reshape(1, W_A)
    gain_b = subln_gain_b[j].reshape(1, 2 * DH_B)

    def ab_group(row0, bsz, seq, tm, lead, state, past_kv, tq, q_pos0, chunked):
        a, g, q, k_new, v_new = _inproj_ab(x_all, row0, bsz, seq, tm, w_perm, bg)
        c0, n0, m0, buf = state
        mix_a, c1, n1, m1, cs = _mlstm(
            a, g, bsz, seq, lead, cw, cb, wq, wk, gain_a,
            c0, n0.reshape(bsz, H_A, 1, DH_A),
            jnp.broadcast_to(m0[:, :, None, None], (bsz, H_A, 1, LANE)), buf)
        if past_kv is None:
            k_all, v_all = k_new, v_new
        else:
            k_all = jnp.concatenate([past_kv[0], k_new], axis=2)
            v_all = jnp.concatenate([past_kv[1], v_new], axis=2)
        near = _near_bias_tiles(rel_bias, tq, q_pos0)
        mix_b = _diff_attention(q.reshape(bsz, seq, W_B), k_all, v_all, near, lam_v, gain_b,
                                tq=tq, q_pos0=q_pos0, chunked=chunked, lam_init=lam_init)
        outs = (k_new, v_new, c1, n1.reshape(bsz, H_A, DH_A), m1[:, :, 0, 0], cs)
        return mix_a, mix_b.reshape(bsz * seq, W_B), outs

    zero_state = (jnp.zeros((bp, H_A, DH_A, DH_A), F32), jnp.zeros((bp, H_A, DH_A), F32),
                  jnp.zeros((bp, H_A), F32), jnp.zeros((bp, CONV_W - 1, W_A), F32))
    mix_a_p, mix_b_p, ab_p = ab_group(0, bp, lp, tm_p, N_META, zero_state, None, tq_p, 0, True)
    s_state = (state_mlstm_C[j], state_mlstm_n[j], state_mlstm_m[j], state_mlstm_conv[j])
    mix_a_s, mix_b_s, ab_s = ab_group(n_p, bs, ss, ss, 0, s_state, (cache_diff_k[j], cache_diff_v[j]),
                                      ss, past, False)
    mix_a = jnp.concatenate([mix_a_p, mix_a_s], axis=0)
    mix_b = jnp.concatenate([mix_b_p, mix_b_s], axis=0)

    n_le = w_gu.shape[0] * N_EXPERTS
    wgu = _deinterleave_gu(w_gu.reshape(n_le, D_MODEL, 2 * D_EXPERT))
    bgu = jnp.concatenate([b_gu[..., 0::2], b_gu[..., 1::2]], axis=-1).reshape(n_le, 1, 2 * D_EXPERT)
    wdn = w_down.reshape(n_le, D_EXPERT, D_MODEL).astype(BF16)
    bdn = b_down.reshape(n_le, 1, D_MODEL)

    def token_stage(layer, mixes, w_out, x_in):
        wr_f = jnp.transpose(w_router[layer])
        wr_hi = wr_f.astype(BF16)
        wr_t = jnp.concatenate([wr_hi, (wr_f - wr_hi.astype(F32)).astype(BF16)], axis=0)
        br = b_router[layer].reshape(N_EXPERTS, 1)
        w_out = w_out.astype(BF16)
        ntile = x_in.shape[0] // TOKEN_TILE
        bounds = [ntile * s // MOE_SPLIT for s in range(MOE_SPLIT + 1)]
        out = None
        for t0, t1 in zip(bounds[:-1], bounds[1:]):
            x1, x1b, top_i, top_g, rank, sizes = _outproj_ln_router(
                mixes, w_out, x_in, ln_g[layer, 0].reshape(1, D_MODEL), ln_b[layer, 0].reshape(1, D_MODEL),
                wr_t, br, t0, t1 - t0)
            out = _moe_layer(x1, x1b, top_i, top_g, rank, sizes, layer * N_EXPERTS, wgu, bgu, wdn, bdn,
                             ln_g[layer, 1].reshape(1, D_MODEL), ln_b[layer, 1].reshape(1, D_MODEL),
                             out, t0, x_in.shape[0])
        return out

    x_all = token_stage(0, [mix_a, mix_b], w_out_ab[j], x_all)

    q_scale = jnp.where(jnp.arange(3 * D_MODEL) < D_MODEL, DH_C ** -0.5 * LOG2_E, 1.0).astype(F32)
    w_c = (w_in_c[j] * q_scale).astype(BF16)

    def c_group(row0, bsz, seq, tm, past_kv, tq, q_pos0):
        q, k_new, v_new = _inproj_c(x_all, row0, bsz, seq, tm, w_c)
        if past_kv is None:
            k_all, v_all = k_new, v_new
        else:
            k_all = jnp.concatenate([past_kv[0], k_new], axis=2)
            v_all = jnp.concatenate([past_kv[1], v_new], axis=2)
        o = _stick_breaking(q.reshape(bsz, seq, D_MODEL), k_all, v_all, tq=tq, q_pos0=q_pos0)
        return o.reshape(bsz * seq, D_MODEL), (k_new, v_new)

    mix_p, c_p = c_group(0, bp, lp, tm_p, None, 2 * KEY_TILE, 0)
    mix_s, c_s = c_group(n_p, bs, ss, ss, (cache_sb_k[j], cache_sb_v[j]), ss, past)
    x_all = token_stage(1, [jnp.concatenate([mix_p, mix_s], axis=0)], w_out_c[j], x_all)

    y_prompt = x_all[:n_p].reshape(bp, lp, D_MODEL)[:, N_META:]
    y_sample = x_all[n_p:].reshape(bs, ss, D_MODEL)
    stack = lambda t: t[None]
    return (y_prompt, y_sample,
            stack(ab_p[0]), stack(ab_p[1]), stack(ab_p[2]), stack(ab_p[3]), stack(ab_p[4]), stack(ab_p[5]),
            stack(c_p[0]), stack(c_p[1]),
            stack(ab_s[0]), stack(ab_s[1]), stack(ab_s[2]), stack(ab_s[3]), stack(ab_s[4]), stack(ab_s[5]),
            stack(c_s[0]), stack(c_s[1]))
```

```python
import functools
import math

import jax
import jax.numpy as jnp
from jax import lax
from jax.experimental import pallas as pl
from jax.experimental.pallas import tpu as pltpu

F32 = jnp.float32
BF16 = jnp.bfloat16

D_MODEL = 1024
DEPTH = 2
CHUNK = 64
N_META = 16
H_A = 4
DH_A = 128
W_A = H_A * DH_A
CONV_W = 4
H_B = 4
DH_B = 64
W_B = H_B * 2 * DH_B
H_C = 16
DH_C = 64
N_BUCKETS = 32
MAX_DISTANCE = 128
N_EXPERTS = 32
TOP_K = 4
D_EXPERT = D_MODEL // 2
SWIGLU_LIMIT = 7.0
SWIGLU_ALPHA = 1.702
DN_ALPHA = (2 * DEPTH) ** 0.25
LN_EPS = 1e-5
OFF_AIF = 3 * W_A
OFF_BQ = OFF_AIF + 2 * H_A

LANE = 128
KEY_TILE = 128
LOG2_KEY_TILE = 7
LOG2_CHUNK = 6
VMEM_LIMIT = 56 * 1024 * 1024
MOE_BLOCK = 512
TOKEN_TILE = 512
SB_GROUP = 8
MOE_SPLIT = 3
NEG_BIG = -1e30
LOG2_E = 1.4426950408889634

COL_G = 3 * W_A
COL_Q = COL_G + LANE
COL_K = COL_Q + W_B
COL_V = COL_K + W_B
D_IN_AB_PAD = COL_V + W_B


def _dot(a, b):
    return jnp.dot(a, b, preferred_element_type=F32)


def _dot_nt(a, b, precision=None):
    return lax.dot_general(a, b, (((1,), (1,)), ((), ())), preferred_element_type=F32, precision=precision)


def _dot_tn(a, b):
    return lax.dot_general(a, b, (((0,), (0,)), ((), ())), preferred_element_type=F32)


def _log_sigmoid(x):
    return jnp.minimum(x, 0.0) - jnp.log(1.0 + jnp.exp(-jnp.abs(x)))


def _sigmoid(x):
    return 1.0 / (1.0 + jnp.exp(-x))


def _chunk_id(pos):
    return (pos + (CHUNK - N_META)) >> LOG2_CHUNK


def _params(sem):
    return pltpu.CompilerParams(dimension_semantics=sem, vmem_limit_bytes=VMEM_LIMIT)


def _inproj_ab_kernel(x_ref, w_ref, bg_ref, a_ref, g_ref, q_ref, k_ref, v_ref):
    xb = x_ref[...].astype(BF16)
    a_ref[...] = _dot(xb, w_ref[:, 0:COL_G])
    g_ref[...] = _dot(xb, w_ref[:, COL_G:COL_Q]) + bg_ref[...]
    q_ref[...] = _dot(xb, w_ref[:, COL_Q:COL_K]).astype(BF16)
    for h in range(H_B):
        k_ref[0, h] = _dot(xb, w_ref[:, COL_K + 2 * DH_B * h:COL_K + 2 * DH_B * (h + 1)])
        v_ref[0, h] = _dot(xb, w_ref[:, COL_V + 2 * DH_B * h:COL_V + 2 * DH_B * (h + 1)])


def _inproj_ab(x_all, row0, bsz, seq, tm, w, bg):
    nrt = seq // tm
    n = bsz * seq
    if row0 % tm:
        x_all, row0 = x_all[row0:row0 + n], 0
    off = row0 // tm
    return pl.pallas_call(
        _inproj_ab_kernel,
        grid=(bsz, nrt),
        in_specs=[
            pl.BlockSpec((tm, D_MODEL), lambda b, r: (off + b * nrt + r, 0)),
            pl.BlockSpec((D_MODEL, D_IN_AB_PAD), lambda b, r: (0, 0)),
            pl.BlockSpec((1, LANE), lambda b, r: (0, 0)),
        ],
        out_specs=[
            pl.BlockSpec((tm, COL_G), lambda b, r: (b * nrt + r, 0)),
            pl.BlockSpec((tm, LANE), lambda b, r: (b * nrt + r, 0)),
            pl.BlockSpec((tm, W_B), lambda b, r: (b * nrt + r, 0)),
            pl.BlockSpec((1, H_B, tm, 2 * DH_B), lambda b, r: (b, 0, r, 0)),
            pl.BlockSpec((1, H_B, tm, 2 * DH_B), lambda b, r: (b, 0, r, 0)),
        ],
        out_shape=[
            jax.ShapeDtypeStruct((n, COL_G), F32),
            jax.ShapeDtypeStruct((n, LANE), F32),
            jax.ShapeDtypeStruct((n, W_B), BF16),
            jax.ShapeDtypeStruct((bsz, H_B, seq, 2 * DH_B), F32),
            jax.ShapeDtypeStruct((bsz, H_B, seq, 2 * DH_B), F32),
        ],
        compiler_params=_params(("parallel", "parallel")),
        name="inproj_ab",
    )(x_all, w, bg)


def _mlstm_kernel(a_ref, g_ref, cw_ref, cb_ref, wq_ref, wk_ref, gain_ref, c0_ref, n0_ref, m0_ref, buf_ref,
                  out_ref, c1_ref, n1_ref, m1_ref, cs_ref, q_s, num_s, rs_s, u_s, nv_s, *, seq, lead):
    cs_ref[0] = a_ref[seq - (CONV_W - 1):seq, 0:W_A]
    c1_ref[...] = c0_ref[...]
    n1_ref[...] = n0_ref[...]
    m1_ref[...] = m0_ref[...]
    sel = (lax.broadcasted_iota(jnp.int32, (8, LANE), 0) == lax.broadcasted_iota(jnp.int32, (8, LANE), 1)).astype(F32)
    lane = lax.broadcasted_iota(jnp.int32, (1, LANE), 1)

    def local(c, r0, lc, first):
        if first:
            win = jnp.concatenate([jnp.zeros((5, W_A), F32), buf_ref[0], a_ref[0:lc, 0:W_A]], axis=0)
        else:
            win = a_ref[pl.ds(pl.multiple_of(r0 - 8, 8), lc + 8), 0:W_A]
        y = cb_ref[...]
        for j in range(CONV_W):
            y = y + win[5 + j:5 + j + lc, :] * cw_ref[j:j + 1, :]
        ca = y * _sigmoid(y)
        g = g_ref[pl.ds(r0, lc), :]
        g_rows = _dot_nt(sel, g, precision=lax.Precision.HIGHEST)
        ti = lax.broadcasted_iota(jnp.int32, (lc, lc), 0)
        si = lax.broadcasted_iota(jnp.int32, (lc, lc), 1)
        causal = si <= ti
        heads = range(H_A)
        hsl = [slice(DH_A * h, DH_A * (h + 1)) for h in heads]
        cab = [ca[:, hsl[h]].astype(BF16) for h in heads]
        qb = [_dot(cab[h], wq_ref[h]).astype(BF16) for h in heads]
        k = [_dot(cab[h], wk_ref[h]) * (DH_A ** -0.5) for h in heads]
        vb = [a_ref[pl.ds(r0, lc), W_A + DH_A * h:W_A + DH_A * (h + 1)].astype(BF16) for h in heads]
        qk = [_dot_nt(qb[h], k[h].astype(BF16)) for h in heads]
        rs = jnp.zeros((lc, LANE), F32)
        s, kw = [], []
        for h in heads:
            ig_c = g[:, h:h + 1]
            lf_c = _log_sigmoid(g[:, H_A + h:H_A + h + 1])
            ig_r = g_rows[h:h + 1, :]
            lf_r = _log_sigmoid(g_rows[H_A + h:H_A + h + 1, :])
            b_c = jnp.sum(jnp.where(causal, lf_r, 0.0), axis=1, keepdims=True)
            b_r = jnp.sum(jnp.where(ti <= si, lf_c, 0.0), axis=0, keepdims=True)
            dmat = jnp.where(causal, b_c - b_r + ig_r, -jnp.inf)
            m_loc = jnp.max(dmat, axis=1, keepdims=True)
            s.append(qk[h] * jnp.exp(dmat - m_loc))
            den_loc = jnp.sum(s[h], axis=1, keepdims=True)
            rs = jnp.where(lane == h, den_loc, rs)
            rs = jnp.where(lane == H_A + h, m_loc, rs)
            rs = jnp.where(lane == 2 * H_A + h, b_c, rs)
            w_end = jnp.exp(b_c[lc - 1:lc, :] - b_c + ig_c - m_loc[lc - 1:lc, :])
            kw.append(k[h] * w_end)
        num = [_dot(s[h].astype(BF16), vb[h]) for h in heads]
        upd = [_dot_tn(kw[h].astype(BF16), vb[h]) for h in heads]
        for h in heads:
            q_s[pl.ds(r0, lc), hsl[h]] = qb[h]
            num_s[pl.ds(r0, lc), hsl[h]] = num[h]
            u_s[c, h] = upd[h]
            nv_s[c, h] = jnp.sum(kw[h], axis=0, keepdims=True)
        rs_s[pl.ds(r0, lc), :] = rs

    def carry(c, r0, lc):
        rs = rs_s[pl.ds(r0, lc), :]
        for h in range(H_A):
            hs = slice(DH_A * h, DH_A * (h + 1))
            den_loc = jnp.broadcast_to(rs[:, h:h + 1], (lc, DH_A))
            m_loc = jnp.broadcast_to(rs[:, H_A + h:H_A + h + 1], (lc, DH_A))
            b_c = jnp.broadcast_to(rs[:, 2 * H_A + h:2 * H_A + h + 1], (lc, DH_A))
            m_prev = m1_ref[0, h]
            c_prev = c1_ref[0, h]
            n_prev = n1_ref[0, h]
            inter = b_c + m_prev
            m_t = jnp.maximum(inter, m_loc)
            gg = jnp.exp(inter - m_t)
            sc = jnp.exp(m_loc - m_t)
            qb = q_s[pl.ds(r0, lc), hs]
            num = sc * num_s[pl.ds(r0, lc), hs] + gg * _dot(qb, c_prev.astype(BF16))
            qn = jnp.broadcast_to(jnp.sum(qb.astype(F32) * n_prev, axis=1, keepdims=True), (lc, DH_A))
            den = sc * den_loc + gg * qn
            hh = num / jnp.maximum(jnp.abs(den), jnp.exp(-m_t))
            m_new = m_t[lc - 1:lc, :]
            decay = jnp.exp(inter[lc - 1:lc, :] - m_new)
            grow = sc[lc - 1:lc, :]
            c1_ref[0, h] = decay * c_prev + grow * u_s[c, h]
            n1_ref[0, h] = decay * n_prev + grow * nv_s[c, h]
            m1_ref[0, h] = m_new
            hn = hh * lax.rsqrt(jnp.mean(hh * hh, axis=1, keepdims=True) + LN_EPS) * gain_ref[:, hs]
            oa = a_ref[pl.ds(r0, lc), 2 * W_A + DH_A * h:2 * W_A + DH_A * (h + 1)]
            out_ref[pl.ds(r0, lc), hs] = (hn * _sigmoid(oa)).astype(BF16)

    first_len = lead if lead else CHUNK
    nrest = (seq - first_len) // CHUNK
    start = lambda i: pl.multiple_of(first_len + i * CHUNK, 16)
    local(0, 0, first_len, True)

    def local_body(i, c):
        local(2 * i + 1, start(2 * i), CHUNK, False)
        local(2 * i + 2, start(2 * i + 1), CHUNK, False)
        return c

    lax.fori_loop(0, nrest // 2, local_body, 0)
    if nrest % 2:
        local(nrest, start(nrest - 1), CHUNK, False)
    carry(0, 0, first_len)

    def carry_body(i, c):
        carry(i + 1, start(i), CHUNK)
        return c

    lax.fori_loop(0, nrest, carry_body, 0)


def _mlstm(a, g, bsz, seq, lead, cw, cb, wq, wk, gain, c0, n0, m0, buf):
    kern = functools.partial(_mlstm_kernel, seq=seq, lead=lead)
    assert seq >= CONV_W - 1
    nchunks = 1 + (seq - (lead if lead else CHUNK)) // CHUNK
    full = lambda *shape: pl.BlockSpec(shape, lambda b: (0,) * len(shape))
    per_b = lambda *shape: pl.BlockSpec((1,) + shape, lambda b: (b,) + (0,) * len(shape))
    return pl.pallas_call(
        kern,
        grid=(bsz,),
        in_specs=[
            pl.BlockSpec((seq, COL_G), lambda b: (b, 0)),
            pl.BlockSpec((seq, LANE), lambda b: (b, 0)),
            full(CONV_W, W_A), full(1, W_A), full(H_A, DH_A, DH_A), full(H_A, DH_A, DH_A), full(1, W_A),
            per_b(H_A, DH_A, DH_A), per_b(H_A, 1, DH_A), per_b(H_A, 1, LANE), per_b(CONV_W - 1, W_A),
        ],
        out_specs=[
            pl.BlockSpec((seq, W_A), lambda b: (b, 0)),
            per_b(H_A, DH_A, DH_A), per_b(H_A, 1, DH_A), per_b(H_A, 1, LANE), per_b(CONV_W - 1, W_A),
        ],
        out_shape=[
            jax.ShapeDtypeStruct((bsz * seq, W_A), BF16),
            jax.ShapeDtypeStruct((bsz, H_A, DH_A, DH_A), F32),
            jax.ShapeDtypeStruct((bsz, H_A, 1, DH_A), F32),
            jax.ShapeDtypeStruct((bsz, H_A, 1, LANE), F32),
            jax.ShapeDtypeStruct((bsz, CONV_W - 1, W_A), F32),
        ],
        scratch_shapes=[
            pltpu.VMEM((seq, W_A), BF16),
            pltpu.VMEM((seq, W_A), F32),
            pltpu.VMEM((seq, LANE), F32),
            pltpu.VMEM((nchunks, H_A, DH_A, DH_A), F32),
            pltpu.VMEM((nchunks, H_A, 1, DH_A), F32),
        ],
        compiler_params=_params(("parallel",)),
        name="mlstm",
    )(a, g, cw, cb, wq, wk, gain, c0, n0, m0, buf)


def _diff_kernel(lam_ref, q_ref, k_ref, v_ref, near_ref, gain_ref, o_ref, kb_ref, vb_ref, s_ref, mx_ref, l_ref,
                 acc_ref, *, tq, lq, lk, q_pos0, chunked, lam_init):
    qi = pl.program_id(2)
    nq = pl.num_programs(2)
    rows_k = kb_ref.shape[0]

    @pl.when(qi == 0)
    def _():
        kb_ref[0:lk, :] = k_ref[0, 0].astype(BF16)
        vb_ref[0:lk, :] = v_ref[0, 0].astype(BF16)
        kb_ref[lk:, :] = jnp.zeros((rows_k - lk, 2 * DH_B), BF16)
        vb_ref[lk:, :] = jnp.zeros((rows_k - lk, 2 * DH_B), BF16)

    q0 = q_pos0 + qi * tq
    qt = q0 >> LOG2_KEY_TILE
    lane = lax.broadcasted_iota(jnp.int32, (1, 2 * DH_B), 1)
    bias_far = near_ref[0, 0, 0:1, 0:1]
    span = near_ref.shape[1] - 3

    def process(rows):
        r2 = 2 * rows
        qf = q_ref[0, 0:rows, :].astype(F32)
        qs = jnp.concatenate([jnp.where(lane < DH_B, qf, 0.0), jnp.where(lane >= DH_B, qf, 0.0)],
                             axis=0).astype(BF16)
        mx_ref[0:r2, :] = jnp.full((r2, KEY_TILE), NEG_BIG, F32)

        def step(j0, width, mode):
            ks = pl.multiple_of(j0 * KEY_TILE, KEY_TILE)
            kt = kb_ref[pl.ds(ks, width * KEY_TILE), :]
            s = _dot_nt(qs, kt)
            if mode == "far":
                s = s + bias_far
            else:
                first = span + 3 - width
                bias = jnp.concatenate([near_ref[0, first + g, 0:rows, :] for g in range(width)], axis=1)
                kpos = ks + lax.broadcasted_iota(jnp.int32, (rows, width * KEY_TILE), 1)
                if chunked:
                    qpos = q0 + lax.broadcasted_iota(jnp.int32, (rows, width * KEY_TILE), 0)
                    mask = _chunk_id(kpos) <= _chunk_id(qpos)
                else:
                    mask = kpos < lk
                s = jnp.where(jnp.concatenate([mask, mask], axis=0),
                              s + jnp.concatenate([bias, bias], axis=0), NEG_BIG)
            mx = mx_ref[0:r2, :]
            for g in range(width):
                blk = s[:, g * KEY_TILE:(g + 1) * KEY_TILE]
                s_ref[j0 + g, 0:r2, :] = blk
                mx = jnp.maximum(mx, blk)
            mx_ref[0:r2, :] = mx

        @pl.when(qt >= 1)
        def _():
            step(qt - 1, span + 2, "near")

        @pl.when(qt == 0)
        def _():
            step(0, span + 1, "near")

        def for_tiles(count, fn):
            def body(g, carry):
                fn(8 * g, 8)
                return carry

            lax.fori_loop(0, count >> 3, body, 0)
            base = (count >> 3) << 3
            for w in range(1, 8):
                @pl.when((count & 7) == w)
                def _():
                    fn(base, w)

        for_tiles(jnp.maximum(qt - 1, 0), lambda j0, w: step(j0, w, "far"))
        m = jnp.max(mx_ref[0:r2, :], axis=1, keepdims=True)
        l_ref[0:r2, :] = jnp.zeros((r2, KEY_TILE), F32)
        acc_ref[0:r2, :] = jnp.zeros((r2, 2 * DH_B), F32)

        def weights(j0, width):
            vt = vb_ref[pl.ds(pl.multiple_of(j0 * KEY_TILE, KEY_TILE), width * KEY_TILE), :]
            lsum = l_ref[0:r2, :]
            ps = []
            for t in range(width):
                p = jnp.exp2(s_ref[j0 + t, 0:r2, :] - m)
                lsum = lsum + p
                ps.append(p.astype(BF16))
            l_ref[0:r2, :] = lsum
            acc_ref[0:r2, :] += _dot(ps[0] if width == 1 else jnp.concatenate(ps, axis=1), vt)

        for_tiles(qt + span + 1, weights)
        o = acc_ref[0:r2, :] / jnp.sum(l_ref[0:r2, :], axis=1, keepdims=True)
        lam = lam_ref[:, 0:1]
        o = o[0:rows] - lam * o[rows:2 * rows]
        o = o * lax.rsqrt(jnp.mean(o * o, axis=1, keepdims=True) + LN_EPS) * gain_ref[...] * (1.0 - lam_init)
        o_ref[0, 0:rows, :] = o.astype(BF16)

    tail = lq % tq
    if tail == 0:
        process(tq)
    else:
        @pl.when(qi < nq - 1)
        def _():
            process(tq)

        @pl.when(qi == nq - 1)
        def _():
            process(tail)


def _diff_attention(q, k, v, near, lam, gain, *, tq, q_pos0, chunked, lam_init):
    bsz, lq, _ = q.shape
    lk = k.shape[2]
    nq = -(-lq // tq)
    nkt = -(-lk // KEY_TILE)
    span = near.shape[1] - 3
    kern = functools.partial(_diff_kernel, tq=tq, lq=lq, lk=lk, q_pos0=q_pos0, chunked=chunked, lam_init=lam_init)
    return pl.pallas_call(
        kern,
        grid=(bsz, H_B, nq),
        in_specs=[
            pl.BlockSpec((1, LANE), lambda b, h, i: (0, 0)),
            pl.BlockSpec((1, tq, 2 * DH_B), lambda b, h, i: (b, i, h)),
            pl.BlockSpec((1, 1, lk, 2 * DH_B), lambda b, h, i: (b, h, 0, 0)),
            pl.BlockSpec((1, 1, lk, 2 * DH_B), lambda b, h, i: (b, h, 0, 0)),
            pl.BlockSpec((1, span + 3, tq, KEY_TILE), lambda b, h, i: (h, 0, 0, 0)),
            pl.BlockSpec((1, 2 * DH_B), lambda b, h, i: (0, 0)),
        ],
        out_specs=pl.BlockSpec((1, tq, 2 * DH_B), lambda b, h, i: (b, i, h)),
        out_shape=jax.ShapeDtypeStruct((bsz, lq, W_B), BF16),
        scratch_shapes=[
            pltpu.VMEM(((nkt + span) * KEY_TILE, 2 * DH_B), BF16),
            pltpu.VMEM(((nkt + span) * KEY_TILE, 2 * DH_B), BF16),
            pltpu.VMEM((nkt + span, 2 * tq, KEY_TILE), F32),
            pltpu.VMEM((2 * tq, KEY_TILE), F32),
            pltpu.VMEM((2 * tq, KEY_TILE), F32),
            pltpu.VMEM((2 * tq, 2 * DH_B), F32),
        ],
        compiler_params=_params(("parallel", "parallel", "arbitrary")),
        name="diff_attention",
    )(lam, q, k, v, near, gain)


def _inproj_c_kernel(x_ref, w_ref, q_ref, k_ref, v_ref):
    xb = x_ref[...].astype(BF16)
    q_ref[...] = _dot(xb, w_ref[:, 0:D_MODEL]).astype(BF16)
    yk = _dot(xb, w_ref[:, D_MODEL:2 * D_MODEL])
    for h in range(H_C):
        k_ref[0, h] = yk[:, DH_C * h:DH_C * (h + 1)]
    yv = _dot(xb, w_ref[:, 2 * D_MODEL:3 * D_MODEL])
    for h in range(H_C):
        v_ref[0, h] = yv[:, DH_C * h:DH_C * (h + 1)]


def _inproj_c(x_all, row0, bsz, seq, tm, w):
    nrt = seq // tm
    if row0 % tm:
        x_all, row0 = x_all[row0:row0 + bsz * seq], 0
    off = row0 // tm
    return pl.pallas_call(
        _inproj_c_kernel,
        grid=(bsz, nrt),
        in_specs=[
            pl.BlockSpec((tm, D_MODEL), lambda b, r: (off + b * nrt + r, 0)),
            pl.BlockSpec((D_MODEL, 3 * D_MODEL), lambda b, r: (0, 0)),
        ],
        out_specs=[
            pl.BlockSpec((tm, D_MODEL), lambda b, r: (b * nrt + r, 0)),
            pl.BlockSpec((1, H_C, tm, DH_C), lambda b, r: (b, 0, r, 0)),
            pl.BlockSpec((1, H_C, tm, DH_C), lambda b, r: (b, 0, r, 0)),
        ],
        out_shape=[
            jax.ShapeDtypeStruct((bsz * seq, D_MODEL), BF16),
            jax.ShapeDtypeStruct((bsz, H_C, seq, DH_C), F32),
            jax.ShapeDtypeStruct((bsz, H_C, seq, DH_C), F32),
        ],
        compiler_params=_params(("parallel", "parallel")),
        name="inproj_c",
    )(x_all, w)


def _sb_kernel(q_ref, k_ref, v_ref, o_ref, kb_ref, vb_ref, acc_ref, run_ref, u_ref, t_ref, hl_ref,
               *, tq, lq, lk, q_pos0):
    qi = pl.program_id(2)
    nq = pl.num_programs(2)
    rows_k = kb_ref.shape[0]

    @pl.when(qi == 0)
    def _():
        kb_ref[0:lk, :] = jnp.concatenate([k_ref[0, 0], k_ref[0, 1]], axis=1).astype(BF16)
        vb_ref[0:lk, :] = jnp.concatenate([v_ref[0, 0], v_ref[0, 1]], axis=1).astype(BF16)
        if rows_k > lk:
            kb_ref[lk:, :] = jnp.zeros((rows_k - lk, 2 * DH_C), BF16)
            vb_ref[lk:, :] = jnp.zeros((rows_k - lk, 2 * DH_C), BF16)

    q0 = q_pos0 + qi * tq
    jd = q0 >> LOG2_KEY_TILE
    jj = lax.broadcasted_iota(jnp.int32, (2 * KEY_TILE, 2 * KEY_TILE), 0) & (KEY_TILE - 1)
    ss = lax.broadcasted_iota(jnp.int32, (2 * KEY_TILE, 2 * KEY_TILE), 1)
    later = jnp.where((jj > ss) | (ss >= KEY_TILE), 1.0, 0.0).astype(BF16)
    lane = lax.broadcasted_iota(jnp.int32, (1, 2 * DH_C), 1)

    def process(rows):
        qf = q_ref[0, 0:rows, :].astype(F32)
        qh = [jnp.where(lane < DH_C, qf, 0.0).astype(BF16), jnp.where(lane >= DH_C, qf, 0.0).astype(BF16)]
        acc_ref[:, 0:rows, :] = jnp.zeros((2, rows, 2 * DH_C), F32)
        run_ref[:, 0:rows, :] = jnp.zeros((2, rows, KEY_TILE), F32)

        def stage_scores(j0, width, masked):
            ks = pl.multiple_of(j0 * KEY_TILE, KEY_TILE)
            kt = kb_ref[pl.ds(ks, width * KEY_TILE), :]
            for h in range(2):
                z = _dot_nt(qh[h], kt)
                sp = jnp.maximum(z, 0.0) + jnp.log(1.0 + jnp.exp2(-jnp.abs(z))) * LOG2_E
                u = z - sp
                if masked:
                    mask = (ks + lax.broadcasted_iota(jnp.int32, (rows, width * KEY_TILE), 1)) < (
                        q0 + lax.broadcasted_iota(jnp.int32, (rows, width * KEY_TILE), 0))
                    sp = jnp.where(mask, sp, 0.0)
                    u = jnp.where(mask, u, NEG_BIG)
                hi = sp.astype(BF16)
                lo = (sp - hi.astype(F32)).astype(BF16)
                for g in range(width):
                    cols = slice(g * KEY_TILE, (g + 1) * KEY_TILE)
                    u_ref[h, j0 + g, 0:rows, :] = u[:, cols]
                    hl_ref[h, j0 + g, 0:rows, :] = jnp.concatenate([hi[:, cols], lo[:, cols]], axis=1)

        def stage_weights(j0, width):
            ks = pl.multiple_of(j0 * KEY_TILE, KEY_TILE)
            vt = vb_ref[pl.ds(ks, width * KEY_TILE), :]
            for h in range(2):
                off = run_ref[h, 0:rows, :]
                parts = [None] * width
                for g in reversed(range(width)):
                    parts[g] = jnp.exp2(u_ref[h, j0 + g, 0:rows, :] - off).astype(BF16)
                    off = off + t_ref[h, j0 + g, 0:rows, :]
                a_all = parts[0] if width == 1 else jnp.concatenate(parts, axis=1)
                acc_ref[h, 0:rows, :] += _dot(a_all, vt)
                run_ref[h, 0:rows, :] = off

        def stage_sums(j0, width):
            for h in range(2):
                hl = hl_ref[h, pl.ds(j0, width), 0:rows, :].reshape(width * rows, 2 * KEY_TILE)
                cs = _dot(hl, later)
                for g in range(width):
                    blk = cs[g * rows:(g + 1) * rows]
                    u_ref[h, j0 + g, 0:rows, :] = u_ref[h, j0 + g, 0:rows, :] - blk[:, 0:KEY_TILE]
                    t_ref[h, j0 + g, 0:rows, :] = blk[:, KEY_TILE:2 * KEY_TILE]

        nfull = jd // SB_GROUP
        dgrp = nfull * SB_GROUP

        def scores_body(g, carry):
            stage_scores(SB_GROUP * g, SB_GROUP, False)
            return carry

        lax.fori_loop(0, nfull, scores_body, 0)

        def sums_body(g, carry):
            stage_sums(SB_GROUP * g, SB_GROUP)
            return carry

        lax.fori_loop(0, nfull, sums_body, 0)
        jlast = (q0 + rows - 1) >> LOG2_KEY_TILE
        for w in range(1, SB_GROUP + 1):
            @pl.when(jlast - dgrp == w - 1)
            def _():
                stage_scores(dgrp, w, True)
                stage_sums(dgrp, w)
                stage_weights(dgrp, w)

        def weights_body(g, carry):
            stage_weights(SB_GROUP * (nfull - 1 - g), SB_GROUP)
            return carry

        lax.fori_loop(0, nfull, weights_body, 0)

        o_ref[0, 0:rows, :] = jnp.where(lane < DH_C, acc_ref[0, 0:rows, :], acc_ref[1, 0:rows, :]).astype(BF16)

    tail = lq % tq
    if tail == 0:
        process(tq)
    else:
        @pl.when(qi < nq - 1)
        def _():
            process(tq)

        @pl.when(qi == nq - 1)
        def _():
            process(tail)


def _stick_breaking(q, k, v, *, tq, q_pos0):
    bsz, lq, _ = q.shape
    lk = k.shape[2]
    nq = -(-lq // tq)
    nkt = SB_GROUP * (-(-lk // (SB_GROUP * KEY_TILE)))
    kern = functools.partial(_sb_kernel, tq=tq, lq=lq, lk=lk, q_pos0=q_pos0)
    return pl.pallas_call(
        kern,
        grid=(bsz, H_C // 2, nq),
        in_specs=[
            pl.BlockSpec((1, tq, 2 * DH_C), lambda b, h, i: (b, i, h)),
            pl.BlockSpec((1, 2, lk, DH_C), lambda b, h, i: (b, h, 0, 0)),
            pl.BlockSpec((1, 2, lk, DH_C), lambda b, h, i: (b, h, 0, 0)),
        ],
        out_specs=pl.BlockSpec((1, tq, 2 * DH_C), lambda b, h, i: (b, i, h)),
        out_shape=jax.ShapeDtypeStruct((bsz, lq, D_MODEL), BF16),
        scratch_shapes=[
            pltpu.VMEM((nkt * KEY_TILE, 2 * DH_C), BF16),
            pltpu.VMEM((nkt * KEY_TILE, 2 * DH_C), BF16),
            pltpu.VMEM((2, tq, 2 * DH_C), F32),
            pltpu.VMEM((2, tq, KEY_TILE), F32),
            pltpu.VMEM((2, nkt, tq, KEY_TILE), F32),
            pltpu.VMEM((2, nkt, tq, KEY_TILE), F32),
            pltpu.VMEM((2, nkt, tq, 2 * KEY_TILE), BF16),
        ],
        compiler_params=_params(("parallel", "parallel", "arbitrary")),
        name="stick_breaking",
    )(q, k, v)


def _layer_norm(z, g, b):
    mu = jnp.mean(z, axis=1, keepdims=True)
    zc = z - mu
    var = jnp.mean(zc * zc, axis=1, keepdims=True)
    return zc * lax.rsqrt(var + LN_EPS) * g + b


def _outproj_kernel(*refs, n_in):
    mix_refs = refs[:n_in]
    (w_ref, x_ref, g_ref, b_ref, wr_ref, br_ref,
     x1_ref, x1b_ref, ti_ref, tg_ref, rk_ref, cnt_ref, tri_ref) = refs[n_in:]
    i = pl.program_id(0)
    tm = x_ref.shape[0]

    @pl.when(i == 0)
    def _():
        cnt_ref[...] = jnp.zeros_like(cnt_ref)
        tri_ref[...] = (lax.broadcasted_iota(jnp.int32, (tm, tm), 0)
                        < lax.broadcasted_iota(jnp.int32, (tm, tm), 1)).astype(BF16)

    y = None
    c0 = 0
    for r in mix_refs:
        wd = r.shape[1]
        part = _dot(r[...], w_ref[c0:c0 + wd, :])
        y = part if y is None else y + part
        c0 += wd
    x1 = _layer_norm(DN_ALPHA * x_ref[...] + y, g_ref[...], b_ref[...])
    x1_ref[...] = x1
    x1h = x1.astype(BF16)
    x1b_ref[...] = x1h
    x1l = (x1 - x1h.astype(F32)).astype(BF16)
    both = _dot_nt(wr_ref[...], x1h)
    logits = (both[0:N_EXPERTS] + both[N_EXPERTS:2 * N_EXPERTS]
              + _dot_nt(wr_ref[0:N_EXPERTS, :], x1l) + br_ref[...])
    ei = lax.broadcasted_iota(jnp.int32, (N_EXPERTS, tm), 0)
    cur = logits
    vals, idxs = [], []
    for _k in range(TOP_K):
        mx = jnp.max(cur, axis=0, keepdims=True)
        ix = jnp.min(jnp.where(cur == mx, ei, N_EXPERTS), axis=0, keepdims=True)
        vals.append(mx)
        idxs.append(ix)
        cur = jnp.where(ei == ix, -jnp.inf, cur)
    es = [jnp.exp(vv - vals[0]) for vv in vals]
    tot = es[0] + es[1] + es[2] + es[3]
    ti_ref[...] = jnp.concatenate(idxs, axis=0)
    tg_ref[...] = jnp.concatenate([e / tot for e in es], axis=0)
    run = cnt_ref[...]
    ranks = []
    for kk in range(TOP_K):
        oh = (ei == idxs[kk])
        before = _dot(oh.astype(BF16), tri_ref[...])
        ranks.append(jnp.sum(jnp.where(oh, before + run, 0.0), axis=0, keepdims=True))
        run = run + jnp.sum(oh.astype(F32), axis=1, keepdims=True)
    rk_ref[...] = jnp.concatenate(ranks, axis=0).astype(jnp.int32)
    cnt_ref[...] = run


def _outproj_ln_router(mixes, w, x_all, g, b, wr_t, br, tile0, ntiles):
    tm = TOKEN_TILE
    n = ntiles * tm
    kern = functools.partial(_outproj_kernel, n_in=len(mixes))
    row_in = lambda wd: pl.BlockSpec((tm, wd), lambda i: (tile0 + i, 0))
    row = lambda wd: pl.BlockSpec((tm, wd), lambda i: (i, 0))
    full = lambda *shape: pl.BlockSpec(shape, lambda i: (0,) * len(shape))
    lanes = lambda rows: pl.BlockSpec((rows, tm), lambda i: (0, i))
    return pl.pallas_call(
        kern,
        grid=(ntiles,),
        in_specs=[row_in(m.shape[1]) for m in mixes] + [
            full(D_MODEL, D_MODEL), row_in(D_MODEL), full(1, D_MODEL), full(1, D_MODEL),
            full(2 * N_EXPERTS, D_MODEL), full(N_EXPERTS, 1)],
        out_specs=[row(D_MODEL), row(D_MODEL), lanes(TOP_K), lanes(TOP_K), lanes(TOP_K), full(N_EXPERTS, 1)],
        out_shape=[
            jax.ShapeDtypeStruct((n, D_MODEL), F32),
            jax.ShapeDtypeStruct((n, D_MODEL), BF16),
            jax.ShapeDtypeStruct((TOP_K, n), jnp.int32),
            jax.ShapeDtypeStruct((TOP_K, n), F32),
            jax.ShapeDtypeStruct((TOP_K, n), jnp.int32),
            jax.ShapeDtypeStruct((N_EXPERTS, 1), F32),
        ],
        scratch_shapes=[pltpu.VMEM((tm, tm), BF16)],
        compiler_params=_params(("arbitrary",)),
        name="outproj_ln_router",
    )(*mixes, w, x_all, g, b, wr_t, br)


def _moe_kernel(be_ref, nu_ref, x_ref, wgu_ref, bgu_ref, wdn_ref, bdn_ref, o_ref):
    @pl.when(pl.program_id(0) < nu_ref[0])
    def _():
        h = _dot(x_ref[...], wgu_ref[0]) + bgu_ref[0]
        glu = jnp.minimum(h[:, 0:D_EXPERT], SWIGLU_LIMIT)
        lin = jnp.clip(h[:, D_EXPERT:2 * D_EXPERT], -SWIGLU_LIMIT, SWIGLU_LIMIT)
        act = glu * _sigmoid(SWIGLU_ALPHA * glu) * (lin + 1.0)
        o_ref[...] = (_dot(act.astype(BF16), wdn_ref[0]) + bdn_ref[0]).astype(BF16)


def _moe_experts(blk_e, n_used, xg, wgu, bgu, wdn, bdn):
    nb = blk_e.shape[0]
    grid_spec = pltpu.PrefetchScalarGridSpec(
        num_scalar_prefetch=2,
        grid=(nb,),
        in_specs=[
            pl.BlockSpec((MOE_BLOCK, D_MODEL), lambda i, be, nu: (i, 0)),
            pl.BlockSpec((1, D_MODEL, 2 * D_EXPERT), lambda i, be, nu: (be[i], 0, 0)),
            pl.BlockSpec((1, 1, 2 * D_EXPERT), lambda i, be, nu: (be[i], 0, 0)),
            pl.BlockSpec((1, D_EXPERT, D_MODEL), lambda i, be, nu: (be[i], 0, 0)),
            pl.BlockSpec((1, 1, D_MODEL), lambda i, be, nu: (be[i], 0, 0)),
        ],
        out_specs=pl.BlockSpec((MOE_BLOCK, D_MODEL), lambda i, be, nu: (i, 0)),
    )
    return pl.pallas_call(
        _moe_kernel,
        grid_spec=grid_spec,
        out_shape=jax.ShapeDtypeStruct((nb * MOE_BLOCK, D_MODEL), BF16),
        compiler_params=_params(("arbitrary",)),
        name="moe_experts",
    )(blk_e, n_used, xg, wgu, bgu, wdn, bdn)


def _deinterleave_kernel(w_ref, p_ref, o_ref):
    o_ref[0] = _dot(w_ref[0].astype(BF16), p_ref[...]).astype(BF16)


def _deinterleave_gu(w):
    n, d, f2 = w.shape
    col = jnp.arange(f2, dtype=jnp.int32)
    src = jnp.where(col < f2 // 2, 2 * col, 2 * (col - f2 // 2) + 1)
    perm = (col[:, None] == src[None, :]).astype(BF16)
    rows = 512
    return pl.pallas_call(
        _deinterleave_kernel,
        grid=(n, d // rows),
        in_specs=[
            pl.BlockSpec((1, rows, f2), lambda e, r: (e, r, 0)),
            pl.BlockSpec((f2, f2), lambda e, r: (0, 0)),
        ],
        out_specs=pl.BlockSpec((1, rows, f2), lambda e, r: (e, r, 0)),
        out_shape=jax.ShapeDtypeStruct((n, d, f2), BF16),
        compiler_params=_params(("parallel", "parallel")),
        name="deinterleave_gu",
    )(w, perm)


def _combine_ln_kernel(x_ref, r_ref, tg_ref, g_ref, b_ref, o_ref):
    tg = tg_ref[...]
    f = r_ref[0].astype(F32) * tg[:, 0:1]
    for kk in range(1, TOP_K):
        f = f + r_ref[kk].astype(F32) * tg[:, kk:kk + 1]
    o_ref[...] = _layer_norm(DN_ALPHA * x_ref[...] + f, g_ref[...], b_ref[...])


def _combine_ln_into_kernel(x_ref, r_ref, tg_ref, g_ref, b_ref, prev_ref, o_ref):
    del prev_ref
    _combine_ln_kernel(x_ref, r_ref, tg_ref, g_ref, b_ref, o_ref)


def _combine_ln(x1, rows, gates, g, b, into, tile0, n_total):
    n = x1.shape[0]
    tm = TOKEN_TILE
    row = pl.BlockSpec((tm, D_MODEL), lambda i: (i, 0))
    vec = pl.BlockSpec((1, D_MODEL), lambda i: (0, 0))
    in_specs = [row, pl.BlockSpec((TOP_K, tm, D_MODEL), lambda i: (0, i, 0)),
                pl.BlockSpec((tm, TOP_K), lambda i: (i, 0)), vec, vec]
    args = [x1, rows, gates, g, b]
    if into is not None:
        in_specs.append(pl.BlockSpec(memory_space=pl.ANY))
        args.append(into)
    return pl.pallas_call(
        _combine_ln_kernel if into is None else _combine_ln_into_kernel,
        grid=(n // tm,),
        in_specs=in_specs,
        out_specs=pl.BlockSpec((tm, D_MODEL), lambda i: (tile0 + i, 0)),
        out_shape=jax.ShapeDtypeStruct((n_total, D_MODEL), F32),
        input_output_aliases={} if into is None else {len(args) - 1: 0},
        compiler_params=_params(("parallel",)),
        name="combine_ln",
    )(*args)


def _moe_layer(x1, x1b, top_i, top_g, rank, sizes, e0, wgu, bgu, wdn, bdn, g, b, into, tile0, n_total):
    n = x1.shape[0]
    nb = -(-(n * TOP_K) // MOE_BLOCK) + N_EXPERTS
    sizes = sizes[:, 0].astype(jnp.int32)
    nblk = (sizes + MOE_BLOCK - 1) // MOE_BLOCK
    blk_end = jnp.cumsum(nblk)
    pad_starts = (blk_end - nblk) * MOE_BLOCK
    n_used = blk_end[-1]
    blk_ids = jnp.minimum(jnp.arange(nb, dtype=jnp.int32), n_used - 1)
    blk_e = jnp.sum((blk_ids[:, None] >= blk_end[None, :]).astype(jnp.int32), axis=1)
    blk_e = jnp.clip(blk_e, 0, N_EXPERTS - 1)
    experts = jnp.arange(N_EXPERTS, dtype=jnp.int32)
    dest = jnp.sum(jnp.where(top_i[:, :, None] == experts, pad_starts, 0), axis=2) + rank
    tok = jnp.broadcast_to(jnp.arange(n, dtype=jnp.int32)[None], (TOP_K, n))
    tok_of_row = jnp.zeros((nb * MOE_BLOCK,), jnp.int32).at[dest.reshape(-1)].set(
        tok.reshape(-1), unique_indices=True, indices_are_sorted=False)
    xg = x1b.at[tok_of_row].get(mode="promise_in_bounds")
    out = _moe_experts(blk_e + e0, n_used.reshape(1), xg, wgu, bgu, wdn, bdn)
    rows = out.at[dest.reshape(-1)].get(mode="promise_in_bounds").reshape(TOP_K, n, D_MODEL)
    return _combine_ln(x1, rows, jnp.transpose(top_g), g, b, into, tile0, n_total)


def _rel_bucket(rel):
    half = N_BUCKETS // 2
    exact = half // 2
    ret = jnp.where(rel > 0, half, 0)
    n = jnp.abs(rel)
    large = exact + (jnp.log(jnp.maximum(n, 1).astype(F32) / exact)
                     / math.log(MAX_DISTANCE / exact) * (half - exact)).astype(jnp.int32)
    large = jnp.minimum(large, half - 1)
    return ret + jnp.where(n < exact, n, large)


def _near_bias_tiles(rel_table, tq, q_pos0):
    base = q_pos0 % KEY_TILE
    span = max(1, tq // KEY_TILE)
    i = jnp.arange(tq, dtype=jnp.int32)[:, None] + base
    j = jnp.arange(KEY_TILE, dtype=jnp.int32)[None, :]
    tiles = [jnp.full((tq, KEY_TILE), -2 * MAX_DISTANCE, jnp.int32)]
    for d in range(-1, span + 1):
        tiles.append(d * KEY_TILE + j - i)
    rel = jnp.stack(tiles)
    bucket = _rel_bucket(rel)
    out = jnp.zeros((H_B,) + rel.shape, F32)
    for bk in range(N_BUCKETS):
        out = jnp.where(bucket[None] == bk, rel_table[bk].astype(F32)[:, None, None, None], out)
    return out * LOG2_E


def kernel(x_prompt, x_sample, cache_diff_k, cache_diff_v, state_mlstm_C, state_mlstm_n, state_mlstm_m, state_mlstm_conv, cache_sb_k, cache_sb_v, meta_tokens, rel_bias, w_in_ab, w_out_ab, conv_w_a, conv_b_a, w_aq_a, w_ak_a, b_if_a, mh_gain_a, lam_q1, lam_k1, lam_q2, lam_k2, subln_gain_b, w_in_c, w_out_c, ln_g, ln_b, w_router, b_router, w_gu, b_gu, w_down, b_down):
    bp, sp, _ = x_prompt.shape
    bs, ss, _ = x_sample.shape
    lp = N_META + sp
    past = cache_diff_k.shape[3]
    n_p = bp * lp
    n_s = bs * ss
    tm_p = lp // 3 if (lp % 3 == 0 and (lp // 3) % 16 == 0) else lp
    tq_p = 2 * KEY_TILE

    parts = []
    for bi in range(bp):
        parts += [meta_tokens.astype(x_prompt.dtype), x_prompt[bi]]
    x_all = jnp.concatenate(parts + [x_sample.reshape(n_s, D_MODEL)], axis=0)

    j = 0
    lam_init = 0.8 - 0.6 * math.exp(-0.3 * 0)
    w_ab = w_in_ab[j]
    w_perm = jnp.concatenate([
        w_ab[:, 0:OFF_AIF],
        jnp.pad(w_ab[:, OFF_AIF:OFF_BQ], ((0, 0), (0, LANE - 2 * H_A))),
        w_ab[:, OFF_BQ:OFF_BQ + W_B] * (DH_B ** -0.5 * LOG2_E),
        w_ab[:, OFF_BQ + W_B:]], axis=1).astype(BF16)
    bg = jnp.pad(b_if_a[j], (0, LANE - 2 * H_A)).reshape(1, LANE)
    lam = (jnp.exp(jnp.sum(lam_q1[j] * lam_k1[j])) - jnp.exp(jnp.sum(lam_q2[j] * lam_k2[j]))).astype(F32) + lam_init
    lam_v = jnp.broadcast_to(lam.reshape(1, 1), (1, LANE))
    cw = conv_w_a[j]
    cb = conv_b_a[j].reshape(1, W_A)
    wq = w_aq_a[j].astype(BF16)
    wk = w_ak_a[j].astype(BF16)
    gain_a = mh_gain_a[j].reshape(1, W_A)
    gain_b = subln_gain_b[j].reshape(1, 2 * DH_B)

    def ab_group(row0, bsz, seq, tm, lead, state, past_kv, tq, q_pos0, chunked):
        a, g, q, k_new, v_new = _inproj_ab(x_all, row0, bsz, seq, tm, w_perm, bg)
        c0, n0, m0, buf = state
        mix_a, c1, n1, m1, cs = _mlstm(
            a, g, bsz, seq, lead, cw, cb, wq, wk, gain_a,
            c0, n0.reshape(bsz, H_A, 1, DH_A),
            jnp.broadcast_to(m0[:, :, None, None], (bsz, H_A, 1, LANE)), buf)
        if past_kv is None:
            k_all, v_all = k_new, v_new
        else:
            k_all = jnp.concatenate([past_kv[0], k_new], axis=2)
            v_all = jnp.concatenate([past_kv[1], v_new], axis=2)
        near = _near_bias_tiles(rel_bias, tq, q_pos0)
        mix_b = _diff_attention(q.reshape(bsz, seq, W_B), k_all, v_all, near, lam_v, gain_b,
                                tq=tq, q_pos0=q_pos0, chunked=chunked, lam_init=lam_init)
        outs = (k_new, v_new, c1, n1.reshape(bsz, H_A, DH_A), m1[:, :, 0, 0], cs)
        return mix_a, mix_b.reshape(bsz * seq, W_B), outs

    zero_state = (jnp.zeros((bp, H_A, DH_A, DH_A), F32), jnp.zeros((bp, H_A, DH_A), F32),
                  jnp.zeros((bp, H_A), F32), jnp.zeros((bp, CONV_W - 1, W_A), F32))
    mix_a_p, mix_b_p, ab_p = ab_group(0, bp, lp, tm_p, N_META, zero_state, None, tq_p, 0, True)
    s_state = (state_mlstm_C[j], state_mlstm_n[j], state_mlstm_m[j], state_mlstm_conv[j])
    mix_a_s, mix_b_s, ab_s = ab_group(n_p, bs, ss, ss, 0, s_state, (cache_diff_k[j], cache_diff_v[j]),
                                      ss, past, False)
    mix_a = jnp.concatenate([mix_a_p, mix_a_s], axis=0)
    mix_b = jnp.concatenate([mix_b_p, mix_b_s], axis=0)

    n_le = w_gu.shape[0] * N_EXPERTS
    wgu = _deinterleave_gu(w_gu.reshape(n_le, D_MODEL, 2 * D_EXPERT))
    bgu = jnp.concatenate([b_gu[..., 0::2], b_gu[..., 1::2]], axis=-1).reshape(n_le, 1, 2 * D_EXPERT)
    wdn = w_down.reshape(n_le, D_EXPERT, D_MODEL).astype(BF16)
    bdn = b_down.reshape(n_le, 1, D_MODEL)

    def token_stage(layer, mixes, w_out, x_in):
        wr_f = jnp.transpose(w_router[layer])
        wr_hi = wr_f.astype(BF16)
        wr_t = jnp.concatenate([wr_hi, (wr_f - wr_hi.astype(F32)).astype(BF16)], axis=0)
        br = b_router[layer].reshape(N_EXPERTS, 1)
        w_out = w_out.astype(BF16)
        ntile = x_in.shape[0] // TOKEN_TILE
        bounds = [ntile * s // MOE_SPLIT for s in range(MOE_SPLIT + 1)]
        out = None
        for t0, t1 in zip(bounds[:-1], bounds[1:]):
            x1, x1b, top_i, top_g, rank, sizes = _outproj_ln_router(
                mixes, w_out, x_in, ln_g[layer, 0].reshape(1, D_MODEL), ln_b[layer, 0].reshape(1, D_MODEL),
                wr_t, br, t0, t1 - t0)
            out = _moe_layer(x1, x1b, top_i, top_g, rank, sizes, layer * N_EXPERTS, wgu, bgu, wdn, bdn,
                             ln_g[layer, 1].reshape(1, D_MODEL), ln_b[layer, 1].reshape(1, D_MODEL),
                             out, t0, x_in.shape[0])
        return out

    x_all = token_stage(0, [mix_a, mix_b], w_out_ab[j], x_all)

    q_scale = jnp.where(jnp.arange(3 * D_MODEL) < D_MODEL, DH_C ** -0.5 * LOG2_E, 1.0).astype(F32)
    w_c = (w_in_c[j] * q_scale).astype(BF16)

    def c_group(row0, bsz, seq, tm, past_kv, tq, q_pos0):
        q, k_new, v_new = _inproj_c(x_all, row0, bsz, seq, tm, w_c)
        if past_kv is None:
            k_all, v_all = k_new, v_new
        else:
            k_all = jnp.concatenate([past_kv[0], k_new], axis=2)
            v_all = jnp.concatenate([past_kv[1], v_new], axis=2)
        o = _stick_breaking(q.reshape(bsz, seq, D_MODEL), k_all, v_all, tq=tq, q_pos0=q_pos0)
        return o.reshape(bsz * seq, D_MODEL), (k_new, v_new)

    mix_p, c_p = c_group(0, bp, lp, tm_p, None, 2 * KEY_TILE, 0)
    mix_s, c_s = c_group(n_p, bs, ss, ss, (cache_sb_k[j], cache_sb_v[j]), ss, past)
    x_all = token_stage(1, [jnp.concatenate([mix_p, mix_s], axis=0)], w_out_c[j], x_all)

    y_prompt = x_all[:n_p].reshape(bp, lp, D_MODEL)[:, N_META:]
    y_sample = x_all[n_p:].reshape(bs, ss, D_MODEL)
    stack = lambda t: t[None]
    return (y_prompt, y_sample,
            stack(ab_p[0]), stack(ab_p[1]), stack(ab_p[2]), stack(ab_p[3]), stack(ab_p[4]), stack(ab_p[5]),
            stack(c_p[0]), stack(c_p[1]),
            stack(ab_s[0]), stack(ab_s[1]), stack(ab_s[2]), stack(ab_s[3]), stack(ab_s[4]), stack(ab_s[5]),
            stack(c_s[0]), stack(c_s[1]))
```

```python
import functools
import math

import jax
import jax.numpy as jnp
from jax import lax
from jax.experimental import pallas as pl
from jax.experimental.pallas import tpu as pltpu

F32 = jnp.float32
BF16 = jnp.bfloat16

D_MODEL = 1024
DEPTH = 2
CHUNK = 64
N_META = 16
H_A = 4
DH_A = 128
W_A = H_A * DH_A
CONV_W = 4
H_B = 4
DH_B = 64
W_B = H_B * 2 * DH_B
H_C = 16
DH_C = 64
N_BUCKETS = 32
MAX_DISTANCE = 128
N_EXPERTS = 32
TOP_K = 4
D_EXPERT = D_MODEL // 2
SWIGLU_LIMIT = 7.0
SWIGLU_ALPHA = 1.702
DN_ALPHA = (2 * DEPTH) ** 0.25
LN_EPS = 1e-5
OFF_AIF = 3 * W_A
OFF_BQ = OFF_AIF + 2 * H_A

LANE = 128
KEY_TILE = 128
LOG2_KEY_TILE = 7
LOG2_CHUNK = 6
VMEM_LIMIT = 56 * 1024 * 1024
MOE_BLOCK = 512
TOKEN_TILE = 1024
SB_GROUP = 8
MOE_SPLIT = 2
NEG_BIG = -1e30
LOG2_E = 1.4426950408889634

COL_G = 3 * W_A
COL_Q = COL_G + LANE
COL_K = COL_Q + W_B
COL_V = COL_K + W_B
D_IN_AB_PAD = COL_V + W_B


def _dot(a, b):
    return jnp.dot(a, b, preferred_element_type=F32)


def _dot_nt(a, b, precision=None):
    return lax.dot_general(a, b, (((1,), (1,)), ((), ())), preferred_element_type=F32, precision=precision)


def _dot_tn(a, b):
    return lax.dot_general(a, b, (((0,), (0,)), ((), ())), preferred_element_type=F32)


def _log_sigmoid(x):
    return jnp.minimum(x, 0.0) - jnp.log(1.0 + jnp.exp(-jnp.abs(x)))


def _sigmoid(x):
    return 1.0 / (1.0 + jnp.exp(-x))


def _chunk_id(pos):
    return (pos + (CHUNK - N_META)) >> LOG2_CHUNK


def _params(sem):
    return pltpu.CompilerParams(dimension_semantics=sem, vmem_limit_bytes=VMEM_LIMIT)


def _inproj_ab_kernel(x_ref, w_ref, bg_ref, a_ref, g_ref, q_ref, k_ref, v_ref):
    xb = x_ref[...].astype(BF16)
    a_ref[...] = _dot(xb, w_ref[:, 0:COL_G])
    g_ref[...] = _dot(xb, w_ref[:, COL_G:COL_Q]) + bg_ref[...]
    q_ref[...] = _dot(xb, w_ref[:, COL_Q:COL_K]).astype(BF16)
    for h in range(H_B):
        k_ref[0, h] = _dot(xb, w_ref[:, COL_K + 2 * DH_B * h:COL_K + 2 * DH_B * (h + 1)])
        v_ref[0, h] = _dot(xb, w_ref[:, COL_V + 2 * DH_B * h:COL_V + 2 * DH_B * (h + 1)])


def _inproj_ab(x_all, row0, bsz, seq, tm, w, bg):
    nrt = seq // tm
    n = bsz * seq
    if row0 % tm:
        x_all, row0 = x_all[row0:row0 + n], 0
    off = row0 // tm
    return pl.pallas_call(
        _inproj_ab_kernel,
        grid=(bsz, nrt),
        in_specs=[
            pl.BlockSpec((tm, D_MODEL), lambda b, r: (off + b * nrt + r, 0)),
            pl.BlockSpec((D_MODEL, D_IN_AB_PAD), lambda b, r: (0, 0)),
            pl.BlockSpec((1, LANE), lambda b, r: (0, 0)),
        ],
        out_specs=[
            pl.BlockSpec((tm, COL_G), lambda b, r: (b * nrt + r, 0)),
            pl.BlockSpec((tm, LANE), lambda b, r: (b * nrt + r, 0)),
            pl.BlockSpec((tm, W_B), lambda b, r: (b * nrt + r, 0)),
            pl.BlockSpec((1, H_B, tm, 2 * DH_B), lambda b, r: (b, 0, r, 0)),
            pl.BlockSpec((1, H_B, tm, 2 * DH_B), lambda b, r: (b, 0, r, 0)),
        ],
        out_shape=[
            jax.ShapeDtypeStruct((n, COL_G), F32),
            jax.ShapeDtypeStruct((n, LANE), F32),
            jax.ShapeDtypeStruct((n, W_B), BF16),
            jax.ShapeDtypeStruct((bsz, H_B, seq, 2 * DH_B), F32),
            jax.ShapeDtypeStruct((bsz, H_B, seq, 2 * DH_B), F32),
        ],
        compiler_params=_params(("parallel", "parallel")),
        name="inproj_ab",
    )(x_all, w, bg)


def _mlstm_kernel(a_ref, g_ref, cw_ref, cb_ref, wq_ref, wk_ref, gain_ref, c0_ref, n0_ref, m0_ref, buf_ref,
                  out_ref, c1_ref, n1_ref, m1_ref, cs_ref, q_s, num_s, rs_s, u_s, nv_s, *, seq, lead):
    cs_ref[0] = a_ref[seq - (CONV_W - 1):seq, 0:W_A]
    c1_ref[...] = c0_ref[...]
    n1_ref[...] = n0_ref[...]
    m1_ref[...] = m0_ref[...]
    sel = (lax.broadcasted_iota(jnp.int32, (8, LANE), 0) == lax.broadcasted_iota(jnp.int32, (8, LANE), 1)).astype(F32)
    lane = lax.broadcasted_iota(jnp.int32, (1, LANE), 1)

    def local(c, r0, lc, first):
        if first:
            win = jnp.concatenate([jnp.zeros((5, W_A), F32), buf_ref[0], a_ref[0:lc, 0:W_A]], axis=0)
        else:
            win = a_ref[pl.ds(pl.multiple_of(r0 - 8, 8), lc + 8), 0:W_A]
        y = cb_ref[...]
        for j in range(CONV_W):
            y = y + win[5 + j:5 + j + lc, :] * cw_ref[j:j + 1, :]
        ca = y * _sigmoid(y)
        g = g_ref[pl.ds(r0, lc), :]
        g_rows = _dot_nt(sel, g, precision=lax.Precision.HIGHEST)
        ti = lax.broadcasted_iota(jnp.int32, (lc, lc), 0)
        si = lax.broadcasted_iota(jnp.int32, (lc, lc), 1)
        causal = si <= ti
        heads = range(H_A)
        hsl = [slice(DH_A * h, DH_A * (h + 1)) for h in heads]
        cab = [ca[:, hsl[h]].astype(BF16) for h in heads]
        qb = [_dot(cab[h], wq_ref[h]).astype(BF16) for h in heads]
        k = [_dot(cab[h], wk_ref[h]) * (DH_A ** -0.5) for h in heads]
        vb = [a_ref[pl.ds(r0, lc), W_A + DH_A * h:W_A + DH_A * (h + 1)].astype(BF16) for h in heads]
        qk = [_dot_nt(qb[h], k[h].astype(BF16)) for h in heads]
        rs = jnp.zeros((lc, LANE), F32)
        s, kw = [], []
        for h in heads:
            ig_c = g[:, h:h + 1]
            lf_c = _log_sigmoid(g[:, H_A + h:H_A + h + 1])
            ig_r = g_rows[h:h + 1, :]
            lf_r = _log_sigmoid(g_rows[H_A + h:H_A + h + 1, :])
            b_c = jnp.sum(jnp.where(causal, lf_r, 0.0), axis=1, keepdims=True)
            b_r = jnp.sum(jnp.where(ti <= si, lf_c, 0.0), axis=0, keepdims=True)
            dmat = jnp.where(causal, b_c - b_r + ig_r, -jnp.inf)
            m_loc = jnp.max(dmat, axis=1, keepdims=True)
            s.append(qk[h] * jnp.exp(dmat - m_loc))
            den_loc = jnp.sum(s[h], axis=1, keepdims=True)
            rs = jnp.where(lane == h, den_loc, rs)
            rs = jnp.where(lane == H_A + h, m_loc, rs)
            rs = jnp.where(lane == 2 * H_A + h, b_c, rs)
            w_end = jnp.exp(b_c[lc - 1:lc, :] - b_c + ig_c - m_loc[lc - 1:lc, :])
            kw.append(k[h] * w_end)
        num = [_dot(s[h].astype(BF16), vb[h]) for h in heads]
        upd = [_dot_tn(kw[h].astype(BF16), vb[h]) for h in heads]
        for h in heads:
            q_s[pl.ds(r0, lc), hsl[h]] = qb[h]
            num_s[pl.ds(r0, lc), hsl[h]] = num[h]
            u_s[c, h] = upd[h]
            nv_s[c, h] = jnp.sum(kw[h], axis=0, keepdims=True)
        rs_s[pl.ds(r0, lc), :] = rs

    def carry(c, r0, lc):
        rs = rs_s[pl.ds(r0, lc), :]
        for h in range(H_A):
            hs = slice(DH_A * h, DH_A * (h + 1))
            den_loc = jnp.broadcast_to(rs[:, h:h + 1], (lc, DH_A))
            m_loc = jnp.broadcast_to(rs[:, H_A + h:H_A + h + 1], (lc, DH_A))
            b_c = jnp.broadcast_to(rs[:, 2 * H_A + h:2 * H_A + h + 1], (lc, DH_A))
            m_prev = m1_ref[0, h]
            c_prev = c1_ref[0, h]
            n_prev = n1_ref[0, h]
            inter = b_c + m_prev
            m_t = jnp.maximum(inter, m_loc)
            gg = jnp.exp(inter - m_t)
            sc = jnp.exp(m_loc - m_t)
            qb = q_s[pl.ds(r0, lc), hs]
            num = sc * num_s[pl.ds(r0, lc), hs] + gg * _dot(qb, c_prev.astype(BF16))
            qn = jnp.broadcast_to(jnp.sum(qb.astype(F32) * n_prev, axis=1, keepdims=True), (lc, DH_A))
            den = sc * den_loc + gg * qn
            hh = num / jnp.maximum(jnp.abs(den), jnp.exp(-m_t))
            m_new = m_t[lc - 1:lc, :]
            decay = jnp.exp(inter[lc - 1:lc, :] - m_new)
            grow = sc[lc - 1:lc, :]
            c1_ref[0, h] = decay * c_prev + grow * u_s[c, h]
            n1_ref[0, h] = decay * n_prev + grow * nv_s[c, h]
            m1_ref[0, h] = m_new
            hn = hh * lax.rsqrt(jnp.mean(hh * hh, axis=1, keepdims=True) + LN_EPS) * gain_ref[:, hs]
            oa = a_ref[pl.ds(r0, lc), 2 * W_A + DH_A * h:2 * W_A + DH_A * (h + 1)]
            out_ref[pl.ds(r0, lc), hs] = (hn * _sigmoid(oa)).astype(BF16)

    first_len = lead if lead else CHUNK
    nrest = (seq - first_len) // CHUNK
    start = lambda i: pl.multiple_of(first_len + i * CHUNK, 16)
    local(0, 0, first_len, True)

    def local_body(i, c):
        local(2 * i + 1, start(2 * i), CHUNK, False)
        local(2 * i + 2, start(2 * i + 1), CHUNK, False)
        return c

    lax.fori_loop(0, nrest // 2, local_body, 0)
    if nrest % 2:
        local(nrest, start(nrest - 1), CHUNK, False)
    carry(0, 0, first_len)

    def carry_body(i, c):
        carry(i + 1, start(i), CHUNK)
        return c

    lax.fori_loop(0, nrest, carry_body, 0)


def _mlstm(a, g, bsz, seq, lead, cw, cb, wq, wk, gain, c0, n0, m0, buf):
    kern = functools.partial(_mlstm_kernel, seq=seq, lead=lead)
    assert seq >= CONV_W - 1
    nchunks = 1 + (seq - (lead if lead else CHUNK)) // CHUNK
    full = lambda *shape: pl.BlockSpec(shape, lambda b: (0,) * len(shape))
    per_b = lambda *shape: pl.BlockSpec((1,) + shape, lambda b: (b,) + (0,) * len(shape))
    return pl.pallas_call(
        kern,
        grid=(bsz,),
        in_specs=[
            pl.BlockSpec((seq, COL_G), lambda b: (b, 0)),
            pl.BlockSpec((seq, LANE), lambda b: (b, 0)),
            full(CONV_W, W_A), full(1, W_A), full(H_A, DH_A, DH_A), full(H_A, DH_A, DH_A), full(1, W_A),
            per_b(H_A, DH_A, DH_A), per_b(H_A, 1, DH_A), per_b(H_A, 1, LANE), per_b(CONV_W - 1, W_A),
        ],
        out_specs=[
            pl.BlockSpec((seq, W_A), lambda b: (b, 0)),
            per_b(H_A, DH_A, DH_A), per_b(H_A, 1, DH_A), per_b(H_A, 1, LANE), per_b(CONV_W - 1, W_A),
        ],
        out_shape=[
            jax.ShapeDtypeStruct((bsz * seq, W_A), BF16),
            jax.ShapeDtypeStruct((bsz, H_A, DH_A, DH_A), F32),
            jax.ShapeDtypeStruct((bsz, H_A, 1, DH_A), F32),
            jax.ShapeDtypeStruct((bsz, H_A, 1, LANE), F32),
            jax.ShapeDtypeStruct((bsz, CONV_W - 1, W_A), F32),
        ],
        scratch_shapes=[
            pltpu.VMEM((seq, W_A), BF16),
            pltpu.VMEM((seq, W_A), F32),
            pltpu.VMEM((seq, LANE), F32),
            pltpu.VMEM((nchunks, H_A, DH_A, DH_A), F32),
            pltpu.VMEM((nchunks, H_A, 1, DH_A), F32),
        ],
        compiler_params=_params(("parallel",)),
        name="mlstm",
    )(a, g, cw, cb, wq, wk, gain, c0, n0, m0, buf)


def _diff_kernel(lam_ref, q_ref, k_ref, v_ref, near_ref, gain_ref, o_ref, kb_ref, vb_ref, s_ref, mx_ref, l_ref,
                 acc_ref, *, tq, lq, lk, q_pos0, chunked, lam_init):
    qi = pl.program_id(2)
    nq = pl.num_programs(2)
    rows_k = kb_ref.shape[0]

    @pl.when(qi == 0)
    def _():
        kb_ref[0:lk, :] = k_ref[0, 0].astype(BF16)
        vb_ref[0:lk, :] = v_ref[0, 0].astype(BF16)
        kb_ref[lk:, :] = jnp.zeros((rows_k - lk, 2 * DH_B), BF16)
        vb_ref[lk:, :] = jnp.zeros((rows_k - lk, 2 * DH_B), BF16)

    q0 = q_pos0 + qi * tq
    qt = q0 >> LOG2_KEY_TILE
    lane = lax.broadcasted_iota(jnp.int32, (1, 2 * DH_B), 1)
    bias_far = near_ref[0, 0, 0:1, 0:1]
    span = near_ref.shape[1] - 3

    def process(rows):
        r2 = 2 * rows
        qf = q_ref[0, 0:rows, :].astype(F32)
        qs = jnp.concatenate([jnp.where(lane < DH_B, qf, 0.0), jnp.where(lane >= DH_B, qf, 0.0)],
                             axis=0).astype(BF16)
        mx_ref[0:r2, :] = jnp.full((r2, KEY_TILE), NEG_BIG, F32)

        def step(j0, width, mode):
            ks = pl.multiple_of(j0 * KEY_TILE, KEY_TILE)
            kt = kb_ref[pl.ds(ks, width * KEY_TILE), :]
            s = _dot_nt(qs, kt)
            if mode == "far":
                s = s + bias_far
            else:
                first = span + 3 - width
                bias = jnp.concatenate([near_ref[0, first + g, 0:rows, :] for g in range(width)], axis=1)
                kpos = ks + lax.broadcasted_iota(jnp.int32, (rows, width * KEY_TILE), 1)
                if chunked:
                    qpos = q0 + lax.broadcasted_iota(jnp.int32, (rows, width * KEY_TILE), 0)
                    mask = _chunk_id(kpos) <= _chunk_id(qpos)
                else:
                    mask = kpos < lk
                s = jnp.where(jnp.concatenate([mask, mask], axis=0),
                              s + jnp.concatenate([bias, bias], axis=0), NEG_BIG)
            mx = mx_ref[0:r2, :]
            for g in range(width):
                blk = s[:, g * KEY_TILE:(g + 1) * KEY_TILE]
                s_ref[j0 + g, 0:r2, :] = blk
                mx = jnp.maximum(mx, blk)
            mx_ref[0:r2, :] = mx

        @pl.when(qt >= 1)
        def _():
            step(qt - 1, span + 2, "near")

        @pl.when(qt == 0)
        def _():
            step(0, span + 1, "near")

        def for_tiles(count, fn):
            def body(g, carry):
                fn(8 * g, 8)
                return carry

            lax.fori_loop(0, count >> 3, body, 0)
            base = (count >> 3) << 3
            for w in range(1, 8):
                @pl.when((count & 7) == w)
                def _():
                    fn(base, w)

        for_tiles(jnp.maximum(qt - 1, 0), lambda j0, w: step(j0, w, "far"))
        m = jnp.max(mx_ref[0:r2, :], axis=1, keepdims=True)
        l_ref[0:r2, :] = jnp.zeros((r2, KEY_TILE), F32)
        acc_ref[0:r2, :] = jnp.zeros((r2, 2 * DH_B), F32)

        def weights(j0, width):
            vt = vb_ref[pl.ds(pl.multiple_of(j0 * KEY_TILE, KEY_TILE), width * KEY_TILE), :]
            lsum = l_ref[0:r2, :]
            ps = []
            for t in range(width):
                p = jnp.exp2(s_ref[j0 + t, 0:r2, :] - m)
                lsum = lsum + p
                ps.append(p.astype(BF16))
            l_ref[0:r2, :] = lsum
            acc_ref[0:r2, :] += _dot(ps[0] if width == 1 else jnp.concatenate(ps, axis=1), vt)

        for_tiles(qt + span + 1, weights)
        o = acc_ref[0:r2, :] / jnp.sum(l_ref[0:r2, :], axis=1, keepdims=True)
        lam = lam_ref[:, 0:1]
        o = o[0:rows] - lam * o[rows:2 * rows]
        o = o * lax.rsqrt(jnp.mean(o * o, axis=1, keepdims=True) + LN_EPS) * gain_ref[...] * (1.0 - lam_init)
        o_ref[0, 0:rows, :] = o.astype(BF16)

    tail = lq % tq
    if tail == 0:
        process(tq)
    else:
        @pl.when(qi < nq - 1)
        def _():
            process(tq)

        @pl.when(qi == nq - 1)
        def _():
            process(tail)


def _diff_attention(q, k, v, near, lam, gain, *, tq, q_pos0, chunked, lam_init):
    bsz, lq, _ = q.shape
    lk = k.shape[2]
    nq = -(-lq // tq)
    nkt = -(-lk // KEY_TILE)
    span = near.shape[1] - 3
    kern = functools.partial(_diff_kernel, tq=tq, lq=lq, lk=lk, q_pos0=q_pos0, chunked=chunked, lam_init=lam_init)
    return pl.pallas_call(
        kern,
        grid=(bsz, H_B, nq),
        in_specs=[
            pl.BlockSpec((1, LANE), lambda b, h, i: (0, 0)),
            pl.BlockSpec((1, tq, 2 * DH_B), lambda b, h, i: (b, i, h)),
            pl.BlockSpec((1, 1, lk, 2 * DH_B), lambda b, h, i: (b, h, 0, 0)),
            pl.BlockSpec((1, 1, lk, 2 * DH_B), lambda b, h, i: (b, h, 0, 0)),
            pl.BlockSpec((1, span + 3, tq, KEY_TILE), lambda b, h, i: (h, 0, 0, 0)),
            pl.BlockSpec((1, 2 * DH_B), lambda b, h, i: (0, 0)),
        ],
        out_specs=pl.BlockSpec((1, tq, 2 * DH_B), lambda b, h, i: (b, i, h)),
        out_shape=jax.ShapeDtypeStruct((bsz, lq, W_B), BF16),
        scratch_shapes=[
            pltpu.VMEM(((nkt + span) * KEY_TILE, 2 * DH_B), BF16),
            pltpu.VMEM(((nkt + span) * KEY_TILE, 2 * DH_B), BF16),
            pltpu.VMEM((nkt + span, 2 * tq, KEY_TILE), F32),
            pltpu.VMEM((2 * tq, KEY_TILE), F32),
            pltpu.VMEM((2 * tq, KEY_TILE), F32),
            pltpu.VMEM((2 * tq, 2 * DH_B), F32),
        ],
        compiler_params=_params(("parallel", "parallel", "arbitrary")),
        name="diff_attention",
    )(lam, q, k, v, near, gain)


def _inproj_c_kernel(x_ref, w_ref, q_ref, k_ref, v_ref):
    xb = x_ref[...].astype(BF16)
    q_ref[...] = _dot(xb, w_ref[:, 0:D_MODEL]).astype(BF16)
    yk = _dot(xb, w_ref[:, D_MODEL:2 * D_MODEL])
    for h in range(H_C):
        k_ref[0, h] = yk[:, DH_C * h:DH_C * (h + 1)]
    yv = _dot(xb, w_ref[:, 2 * D_MODEL:3 * D_MODEL])
    for h in range(H_C):
        v_ref[0, h] = yv[:, DH_C * h:DH_C * (h + 1)]


def _inproj_c(x_all, row0, bsz, seq, tm, w):
    nrt = seq // tm
    if row0 % tm:
        x_all, row0 = x_all[row0:row0 + bsz * seq], 0
    off = row0 // tm
    return pl.pallas_call(
        _inproj_c_kernel,
        grid=(bsz, nrt),
        in_specs=[
            pl.BlockSpec((tm, D_MODEL), lambda b, r: (off + b * nrt + r, 0)),
            pl.BlockSpec((D_MODEL, 3 * D_MODEL), lambda b, r: (0, 0)),
        ],
        out_specs=[
            pl.BlockSpec((tm, D_MODEL), lambda b, r: (b * nrt + r, 0)),
            pl.BlockSpec((1, H_C, tm, DH_C), lambda b, r: (b, 0, r, 0)),
            pl.BlockSpec((1, H_C, tm, DH_C), lambda b, r: (b, 0, r, 0)),
        ],
        out_shape=[
            jax.ShapeDtypeStruct((bsz * seq, D_MODEL), BF16),
            jax.ShapeDtypeStruct((bsz, H_C, seq, DH_C), F32),
            jax.ShapeDtypeStruct((bsz, H_C, seq, DH_C), F32),
        ],
        compiler_params=_params(("parallel", "parallel")),
        name="inproj_c",
    )(x_all, w)


def _sb_kernel(q_ref, k_ref, v_ref, o_ref, kb_ref, vb_ref, acc_ref, run_ref, u_ref, t_ref, hl_ref,
               *, tq, lq, lk, q_pos0):
    qi = pl.program_id(2)
    nq = pl.num_programs(2)
    rows_k = kb_ref.shape[0]

    @pl.when(qi == 0)
    def _():
        kb_ref[0:lk, :] = jnp.concatenate([k_ref[0, 0], k_ref[0, 1]], axis=1).astype(BF16)
        vb_ref[0:lk, :] = jnp.concatenate([v_ref[0, 0], v_ref[0, 1]], axis=1).astype(BF16)
        if rows_k > lk:
            kb_ref[lk:, :] = jnp.zeros((rows_k - lk, 2 * DH_C), BF16)
            vb_ref[lk:, :] = jnp.zeros((rows_k - lk, 2 * DH_C), BF16)

    q0 = q_pos0 + qi * tq
    jd = q0 >> LOG2_KEY_TILE
    jj = lax.broadcasted_iota(jnp.int32, (2 * KEY_TILE, 2 * KEY_TILE), 0) & (KEY_TILE - 1)
    ss = lax.broadcasted_iota(jnp.int32, (2 * KEY_TILE, 2 * KEY_TILE), 1)
    later = jnp.where((jj > ss) | (ss >= KEY_TILE), 1.0, 0.0).astype(BF16)
    lane = lax.broadcasted_iota(jnp.int32, (1, 2 * DH_C), 1)

    def process(rows):
        qf = q_ref[0, 0:rows, :].astype(F32)
        qh = [jnp.where(lane < DH_C, qf, 0.0).astype(BF16), jnp.where(lane >= DH_C, qf, 0.0).astype(BF16)]
        acc_ref[:, 0:rows, :] = jnp.zeros((2, rows, 2 * DH_C), F32)
        run_ref[:, 0:rows, :] = jnp.zeros((2, rows, KEY_TILE), F32)

        def stage_scores(j0, width, masked):
            ks = pl.multiple_of(j0 * KEY_TILE, KEY_TILE)
            kt = kb_ref[pl.ds(ks, width * KEY_TILE), :]
            for h in range(2):
                z = _dot_nt(qh[h], kt)
                sp = jnp.maximum(z, 0.0) + jnp.log(1.0 + jnp.exp2(-jnp.abs(z))) * LOG2_E
                u = z - sp
                if masked:
                    mask = (ks + lax.broadcasted_iota(jnp.int32, (rows, width * KEY_TILE), 1)) < (
                        q0 + lax.broadcasted_iota(jnp.int32, (rows, width * KEY_TILE), 0))
                    sp = jnp.where(mask, sp, 0.0)
                    u = jnp.where(mask, u, NEG_BIG)
                hi = sp.astype(BF16)
                lo = (sp - hi.astype(F32)).astype(BF16)
                for g in range(width):
                    cols = slice(g * KEY_TILE, (g + 1) * KEY_TILE)
                    u_ref[h, j0 + g, 0:rows, :] = u[:, cols]
                    hl_ref[h, j0 + g, 0:rows, :] = jnp.concatenate([hi[:, cols], lo[:, cols]], axis=1)

        def stage_weights(j0, width):
            ks = pl.multiple_of(j0 * KEY_TILE, KEY_TILE)
            vt = vb_ref[pl.ds(ks, width * KEY_TILE), :]
            for h in range(2):
                off = run_ref[h, 0:rows, :]
                parts = [None] * width
                for g in reversed(range(width)):
                    parts[g] = jnp.exp2(u_ref[h, j0 + g, 0:rows, :] - off).astype(BF16)
                    off = off + t_ref[h, j0 + g, 0:rows, :]
                a_all = parts[0] if width == 1 else jnp.concatenate(parts, axis=1)
                acc_ref[h, 0:rows, :] += _dot(a_all, vt)
                run_ref[h, 0:rows, :] = off

        def stage_sums(j0, width):
            for h in range(2):
                hl = hl_ref[h, pl.ds(j0, width), 0:rows, :].reshape(width * rows, 2 * KEY_TILE)
                cs = _dot(hl, later)
                for g in range(width):
                    blk = cs[g * rows:(g + 1) * rows]
                    u_ref[h, j0 + g, 0:rows, :] = u_ref[h, j0 + g, 0:rows, :] - blk[:, 0:KEY_TILE]
                    t_ref[h, j0 + g, 0:rows, :] = blk[:, KEY_TILE:2 * KEY_TILE]

        nfull = jd // SB_GROUP
        dgrp = nfull * SB_GROUP

        def scores_body(g, carry):
            stage_scores(SB_GROUP * g, SB_GROUP, False)
            return carry

        lax.fori_loop(0, nfull, scores_body, 0)

        def sums_body(g, carry):
            stage_sums(SB_GROUP * g, SB_GROUP)
            return carry

        lax.fori_loop(0, nfull, sums_body, 0)
        jlast = (q0 + rows - 1) >> LOG2_KEY_TILE
        for w in range(1, SB_GROUP + 1):
            @pl.when(jlast - dgrp == w - 1)
            def _():
                stage_scores(dgrp, w, True)
                stage_sums(dgrp, w)
                stage_weights(dgrp, w)

        def weights_body(g, carry):
            stage_weights(SB_GROUP * (nfull - 1 - g), SB_GROUP)
            return carry

        lax.fori_loop(0, nfull, weights_body, 0)

        o_ref[0, 0:rows, :] = jnp.where(lane < DH_C, acc_ref[0, 0:rows, :], acc_ref[1, 0:rows, :]).astype(BF16)

    tail = lq % tq
    if tail == 0:
        process(tq)
    else:
        @pl.when(qi < nq - 1)
        def _():
            process(tq)

        @pl.when(qi == nq - 1)
        def _():
            process(tail)


def _stick_breaking(q, k, v, *, tq, q_pos0):
    bsz, lq, _ = q.shape
    lk = k.shape[2]
    nq = -(-lq // tq)
    nkt = SB_GROUP * (-(-lk // (SB_GROUP * KEY_TILE)))
    kern = functools.partial(_sb_kernel, tq=tq, lq=lq, lk=lk, q_pos0=q_pos0)
    return pl.pallas_call(
        kern,
        grid=(bsz, H_C // 2, nq),
        in_specs=[
            pl.BlockSpec((1, tq, 2 * DH_C), lambda b, h, i: (b, i, h)),
            pl.BlockSpec((1, 2, lk, DH_C), lambda b, h, i: (b, h, 0, 0)),
            pl.BlockSpec((1, 2, lk, DH_C), lambda b, h, i: (b, h, 0, 0)),
        ],
        out_specs=pl.BlockSpec((1, tq, 2 * DH_C), lambda b, h, i: (b, i, h)),
        out_shape=jax.ShapeDtypeStruct((bsz, lq, D_MODEL), BF16),
        scratch_shapes=[
            pltpu.VMEM((nkt * KEY_TILE, 2 * DH_C), BF16),
            pltpu.VMEM((nkt * KEY_TILE, 2 * DH_C), BF16),
            pltpu.VMEM((2, tq, 2 * DH_C), F32),
            pltpu.VMEM((2, tq, KEY_TILE), F32),
            pltpu.VMEM((2, nkt, tq, KEY_TILE), F32),
            pltpu.VMEM((2, nkt, tq, KEY_TILE), F32),
            pltpu.VMEM((2, nkt, tq, 2 * KEY_TILE), BF16),
        ],
        compiler_params=_params(("parallel", "parallel", "arbitrary")),
        name="stick_breaking",
    )(q, k, v)


def _layer_norm(z, g, b):
    mu = jnp.mean(z, axis=1, keepdims=True)
    zc = z - mu
    var = jnp.mean(zc * zc, axis=1, keepdims=True)
    return zc * lax.rsqrt(var + LN_EPS) * g + b


def _outproj_kernel(*refs, n_in):
    mix_refs = refs[:n_in]
    (w_ref, x_ref, g_ref, b_ref, wr_ref, br_ref,
     x1_ref, x1b_ref, ti_ref, tg_ref, rk_ref, cnt_ref, tri_ref) = refs[n_in:]
    i = pl.program_id(0)
    tm = x_ref.shape[0]

    @pl.when(i == 0)
    def _():
        cnt_ref[...] = jnp.zeros_like(cnt_ref)
        tri_ref[...] = (lax.broadcasted_iota(jnp.int32, (tm, tm), 0)
                        < lax.broadcasted_iota(jnp.int32, (tm, tm), 1)).astype(BF16)

    y = None
    c0 = 0
    for r in mix_refs:
        wd = r.shape[1]
        part = _dot(r[...], w_ref[c0:c0 + wd, :])
        y = part if y is None else y + part
        c0 += wd
    x1 = _layer_norm(DN_ALPHA * x_ref[...] + y, g_ref[...], b_ref[...])
    x1_ref[...] = x1
    x1h = x1.astype(BF16)
    x1b_ref[...] = x1h
    x1l = (x1 - x1h.astype(F32)).astype(BF16)
    both = _dot_nt(wr_ref[...], x1h)
    logits = (both[0:N_EXPERTS] + both[N_EXPERTS:2 * N_EXPERTS]
              + _dot_nt(wr_ref[0:N_EXPERTS, :], x1l) + br_ref[...])
    ei = lax.broadcasted_iota(jnp.int32, (N_EXPERTS, tm), 0)
    cur = logits
    vals, idxs = [], []
    for _k in range(TOP_K):
        mx = jnp.max(cur, axis=0, keepdims=True)
        ix = jnp.min(jnp.where(cur == mx, ei, N_EXPERTS), axis=0, keepdims=True)
        vals.append(mx)
        idxs.append(ix)
        cur = jnp.where(ei == ix, -jnp.inf, cur)
    es = [jnp.exp(vv - vals[0]) for vv in vals]
    tot = es[0] + es[1] + es[2] + es[3]
    ti_ref[...] = jnp.concatenate(idxs, axis=0)
    tg_ref[...] = jnp.concatenate([e / tot for e in es], axis=0)
    run = cnt_ref[...]
    ranks = []
    for kk in range(TOP_K):
        oh = (ei == idxs[kk])
        before = _dot(oh.astype(BF16), tri_ref[...])
        ranks.append(jnp.sum(jnp.where(oh, before + run, 0.0), axis=0, keepdims=True))
        run = run + jnp.sum(oh.astype(F32), axis=1, keepdims=True)
    rk_ref[...] = jnp.concatenate(ranks, axis=0).astype(jnp.int32)
    cnt_ref[...] = run


def _outproj_ln_router(mixes, w, x_all, g, b, wr_t, br, tile0, ntiles):
    tm = TOKEN_TILE
    n = ntiles * tm
    kern = functools.partial(_outproj_kernel, n_in=len(mixes))
    row_in = lambda wd: pl.BlockSpec((tm, wd), lambda i: (tile0 + i, 0))
    row = lambda wd: pl.BlockSpec((tm, wd), lambda i: (i, 0))
    full = lambda *shape: pl.BlockSpec(shape, lambda i: (0,) * len(shape))
    lanes = lambda rows: pl.BlockSpec((rows, tm), lambda i: (0, i))
    return pl.pallas_call(
        kern,
        grid=(ntiles,),
        in_specs=[row_in(m.shape[1]) for m in mixes] + [
            full(D_MODEL, D_MODEL), row_in(D_MODEL), full(1, D_MODEL), full(1, D_MODEL),
            full(2 * N_EXPERTS, D_MODEL), full(N_EXPERTS, 1)],
        out_specs=[row(D_MODEL), row(D_MODEL), lanes(TOP_K), lanes(TOP_K), lanes(TOP_K), full(N_EXPERTS, 1)],
        out_shape=[
            jax.ShapeDtypeStruct((n, D_MODEL), F32),
            jax.ShapeDtypeStruct((n, D_MODEL), BF16),
            jax.ShapeDtypeStruct((TOP_K, n), jnp.int32),
            jax.ShapeDtypeStruct((TOP_K, n), F32),
            jax.ShapeDtypeStruct((TOP_K, n), jnp.int32),
            jax.ShapeDtypeStruct((N_EXPERTS, 1), F32),
        ],
        scratch_shapes=[pltpu.VMEM((tm, tm), BF16)],
        compiler_params=_params(("arbitrary",)),
        name="outproj_ln_router",
    )(*mixes, w, x_all, g, b, wr_t, br)


def _moe_kernel(be_ref, nu_ref, x_ref, wgu_ref, bgu_ref, wdn_ref, bdn_ref, o_ref):
    @pl.when(pl.program_id(0) < nu_ref[0])
    def _():
        h = _dot(x_ref[...], wgu_ref[0]) + bgu_ref[0]
        glu = jnp.minimum(h[:, 0:D_EXPERT], SWIGLU_LIMIT)
        lin = jnp.clip(h[:, D_EXPERT:2 * D_EXPERT], -SWIGLU_LIMIT, SWIGLU_LIMIT)
        act = glu * _sigmoid(SWIGLU_ALPHA * glu) * (lin + 1.0)
        o_ref[...] = (_dot(act.astype(BF16), wdn_ref[0]) + bdn_ref[0]).astype(BF16)


def _moe_experts(blk_e, n_used, xg, wgu, bgu, wdn, bdn):
    nb = blk_e.shape[0]
    grid_spec = pltpu.PrefetchScalarGridSpec(
        num_scalar_prefetch=2,
        grid=(nb,),
        in_specs=[
            pl.BlockSpec((MOE_BLOCK, D_MODEL), lambda i, be, nu: (i, 0)),
            pl.BlockSpec((1, D_MODEL, 2 * D_EXPERT), lambda i, be, nu: (be[i], 0, 0)),
            pl.BlockSpec((1, 1, 2 * D_EXPERT), lambda i, be, nu: (be[i], 0, 0)),
            pl.BlockSpec((1, D_EXPERT, D_MODEL), lambda i, be, nu: (be[i], 0, 0)),
            pl.BlockSpec((1, 1, D_MODEL), lambda i, be, nu: (be[i], 0, 0)),
        ],
        out_specs=pl.BlockSpec((MOE_BLOCK, D_MODEL), lambda i, be, nu: (i, 0)),
    )
    return pl.pallas_call(
        _moe_kernel,
        grid_spec=grid_spec,
        out_shape=jax.ShapeDtypeStruct((nb * MOE_BLOCK, D_MODEL), BF16),
        compiler_params=_params(("arbitrary",)),
        name="moe_experts",
    )(blk_e, n_used, xg, wgu, bgu, wdn, bdn)


def _deinterleave_kernel(w_ref, p_ref, o_ref):
    o_ref[0] = _dot(w_ref[0].astype(BF16), p_ref[...]).astype(BF16)


def _deinterleave_gu(w):
    n, d, f2 = w.shape
    col = jnp.arange(f2, dtype=jnp.int32)
    src = jnp.where(col < f2 // 2, 2 * col, 2 * (col - f2 // 2) + 1)
    perm = (col[:, None] == src[None, :]).astype(BF16)
    rows = 512
    return pl.pallas_call(
        _deinterleave_kernel,
        grid=(n, d // rows),
        in_specs=[
            pl.BlockSpec((1, rows, f2), lambda e, r: (e, r, 0)),
            pl.BlockSpec((f2, f2), lambda e, r: (0, 0)),
        ],
        out_specs=pl.BlockSpec((1, rows, f2), lambda e, r: (e, r, 0)),
        out_shape=jax.ShapeDtypeStruct((n, d, f2), BF16),
        compiler_params=_params(("parallel", "parallel")),
        name="deinterleave_gu",
    )(w, perm)


def _combine_ln_kernel(x_ref, r_ref, tg_ref, g_ref, b_ref, o_ref):
    tg = tg_ref[...]
    f = r_ref[0].astype(F32) * tg[:, 0:1]
    for kk in range(1, TOP_K):
        f = f + r_ref[kk].astype(F32) * tg[:, kk:kk + 1]
    o_ref[...] = _layer_norm(DN_ALPHA * x_ref[...] + f, g_ref[...], b_ref[...])


def _combine_ln_into_kernel(x_ref, r_ref, tg_ref, g_ref, b_ref, prev_ref, o_ref):
    del prev_ref
    _combine_ln_kernel(x_ref, r_ref, tg_ref, g_ref, b_ref, o_ref)


def _combine_ln(x1, rows, gates, g, b, into, tile0, n_total):
    n = x1.shape[0]
    tm = TOKEN_TILE
    row = pl.BlockSpec((tm, D_MODEL), lambda i: (i, 0))
    vec = pl.BlockSpec((1, D_MODEL), lambda i: (0, 0))
    in_specs = [row, pl.BlockSpec((TOP_K, tm, D_MODEL), lambda i: (0, i, 0)),
                pl.BlockSpec((tm, TOP_K), lambda i: (i, 0)), vec, vec]
    args = [x1, rows, gates, g, b]
    if into is not None:
        in_specs.append(pl.BlockSpec(memory_space=pl.ANY))
        args.append(into)
    return pl.pallas_call(
        _combine_ln_kernel if into is None else _combine_ln_into_kernel,
        grid=(n // tm,),
        in_specs=in_specs,
        out_specs=pl.BlockSpec((tm, D_MODEL), lambda i: (tile0 + i, 0)),
        out_shape=jax.ShapeDtypeStruct((n_total, D_MODEL), F32),
        input_output_aliases={} if into is None else {len(args) - 1: 0},
        compiler_params=_params(("parallel",)),
        name="combine_ln",
    )(*args)


def _moe_layer(x1, x1b, top_i, top_g, rank, sizes, e0, wgu, bgu, wdn, bdn, g, b, into, tile0, n_total):
    n = x1.shape[0]
    nb = -(-(n * TOP_K) // MOE_BLOCK) + N_EXPERTS
    sizes = sizes[:, 0].astype(jnp.int32)
    nblk = (sizes + MOE_BLOCK - 1) // MOE_BLOCK
    blk_end = jnp.cumsum(nblk)
    pad_starts = (blk_end - nblk) * MOE_BLOCK
    n_used = blk_end[-1]
    blk_ids = jnp.minimum(jnp.arange(nb, dtype=jnp.int32), n_used - 1)
    blk_e = jnp.sum((blk_ids[:, None] >= blk_end[None, :]).astype(jnp.int32), axis=1)
    blk_e = jnp.clip(blk_e, 0, N_EXPERTS - 1)
    experts = jnp.arange(N_EXPERTS, dtype=jnp.int32)
    dest = jnp.sum(jnp.where(top_i[:, :, None] == experts, pad_starts, 0), axis=2) + rank
    tok = jnp.broadcast_to(jnp.arange(n, dtype=jnp.int32)[None], (TOP_K, n))
    tok_of_row = jnp.zeros((nb * MOE_BLOCK,), jnp.int32).at[dest.reshape(-1)].set(
        tok.reshape(-1), unique_indices=True, indices_are_sorted=False)
    xg = x1b.at[tok_of_row].get(mode="promise_in_bounds")
    out = _moe_experts(blk_e + e0, n_used.reshape(1), xg, wgu, bgu, wdn, bdn)
    rows = out.at[dest.reshape(-1)].get(mode="promise_in_bounds").reshape(TOP_K, n, D_MODEL)
    return _combine_ln(x1, rows, jnp.transpose(top_g), g, b, into, tile0, n_total)


def _rel_bucket(rel):
    half = N_BUCKETS // 2
    exact = half // 2
    ret = jnp.where(rel > 0, half, 0)
    n = jnp.abs(rel)
    large = exact + (jnp.log(jnp.maximum(n, 1).astype(F32) / exact)
                     / math.log(MAX_DISTANCE / exact) * (half - exact)).astype(jnp.int32)
    large = jnp.minimum(large, half - 1)
    return ret + jnp.where(n < exact, n, large)


def _near_bias_tiles(rel_table, tq, q_pos0):
    base = q_pos0 % KEY_TILE
    span = max(1, tq // KEY_TILE)
    i = jnp.arange(tq, dtype=jnp.int32)[:, None] + base
    j = jnp.arange(KEY_TILE, dtype=jnp.int32)[None, :]
    tiles = [jnp.full((tq, KEY_TILE), -2 * MAX_DISTANCE, jnp.int32)]
    for d in range(-1, span + 1):
        tiles.append(d * KEY_TILE + j - i)
    rel = jnp.stack(tiles)
    bucket = _rel_bucket(rel)
    out = jnp.zeros((H_B,) + rel.shape, F32)
    for bk in range(N_BUCKETS):
        out = jnp.where(bucket[None] == bk, rel_table[bk].astype(F32)[:, None, None, None], out)
    return out * LOG2_E


def kernel(x_prompt, x_sample, cache_diff_k, cache_diff_v, state_mlstm_C, state_mlstm_n, state_mlstm_m, state_mlstm_conv, cache_sb_k, cache_sb_v, meta_tokens, rel_bias, w_in_ab, w_out_ab, conv_w_a, conv_b_a, w_aq_a, w_ak_a, b_if_a, mh_gain_a, lam_q1, lam_k1, lam_q2, lam_k2, subln_gain_b, w_in_c, w_out_c, ln_g, ln_b, w_router, b_router, w_gu, b_gu, w_down, b_down):
    bp, sp, _ = x_prompt.shape
    bs, ss, _ = x_sample.shape
    lp = N_META + sp
    past = cache_diff_k.shape[3]
    n_p = bp * lp
    n_s = bs * ss
    tm_p = lp // 3 if (lp % 3 == 0 and (lp // 3) % 16 == 0) else lp
    tq_p = 2 * KEY_TILE

    parts = []
    for bi in range(bp):
        parts += [meta_tokens.astype(x_prompt.dtype), x_prompt[bi]]
    x_all = jnp.concatenate(parts + [x_sample.reshape(n_s, D_MODEL)], axis=0)

    j = 0
    lam_init = 0.8 - 0.6 * math.exp(-0.3 * 0)
    w_ab = w_in_ab[j]
    w_perm = jnp.concatenate([
        w_ab[:, 0:OFF_AIF],
        jnp.pad(w_ab[:, OFF_AIF:OFF_BQ], ((0, 0), (0, LANE - 2 * H_A))),
        w_ab[:, OFF_BQ:OFF_BQ + W_B] * (DH_B ** -0.5 * LOG2_E),
        w_ab[:, OFF_BQ + W_B:]], axis=1).astype(BF16)
    bg = jnp.pad(b_if_a[j], (0, LANE - 2 * H_A)).reshape(1, LANE)
    lam = (jnp.exp(jnp.sum(lam_q1[j] * lam_k1[j])) - jnp.exp(jnp.sum(lam_q2[j] * lam_k2[j]))).astype(F32) + lam_init
    lam_v = jnp.broadcast_to(lam.reshape(1, 1), (1, LANE))
    cw = conv_w_a[j]
    cb = conv_b_a[j].reshape(1, W_A)
    wq = w_aq_a[j].astype(BF16)
    wk = w_ak_a[j].astype(BF16)
    gain_a = mh_gain_a[j].reshape(1, W_A)
    gain_b = subln_gain_b[j].reshape(1, 2 * DH_B)

    def ab_group(row0, bsz, seq, tm, lead, state, past_kv, tq, q_pos0, chunked):
        a, g, q, k_new, v_new = _inproj_ab(x_all, row0, bsz, seq, tm, w_perm, bg)
        c0, n0, m0, buf = state
        mix_a, c1, n1, m1, cs = _mlstm(
            a, g, bsz, seq, lead, cw, cb, wq, wk, gain_a,
            c0, n0.reshape(bsz, H_A, 1, DH_A),
            jnp.broadcast_to(m0[:, :, None, None], (bsz, H_A, 1, LANE)), buf)
        if past_kv is None:
            k_all, v_all = k_new, v_new
        else:
            k_all = jnp.concatenate([past_kv[0], k_new], axis=2)
            v_all = jnp.concatenate([past_kv[1], v_new], axis=2)
        near = _near_bias_tiles(rel_bias, tq, q_pos0)
        mix_b = _diff_attention(q.reshape(bsz, seq, W_B), k_all, v_all, near, lam_v, gain_b,
                                tq=tq, q_pos0=q_pos0, chunked=chunked, lam_init=lam_init)
        outs = (k_new, v_new, c1, n1.reshape(bsz, H_A, DH_A), m1[:, :, 0, 0], cs)
        return mix_a, mix_b.reshape(bsz * seq, W_B), outs

    zero_state = (jnp.zeros((bp, H_A, DH_A, DH_A), F32), jnp.zeros((bp, H_A, DH_A), F32),
                  jnp.zeros((bp, H_A), F32), jnp.zeros((bp, CONV_W - 1, W_A), F32))
    mix_a_p, mix_b_p, ab_p = ab_group(0, bp, lp, tm_p, N_META, zero_state, None, tq_p, 0, True)
    s_state = (state_mlstm_C[j], state_mlstm_n[j], state_mlstm_m[j], state_mlstm_conv[j])
    mix_a_s, mix_b_s, ab_s = ab_group(n_p, bs, ss, ss, 0, s_state, (cache_diff_k[j], cache_diff_v[j]),
                                      ss, past, False)
    mix_a = jnp.concatenate([mix_a_p, mix_a_s], axis=0)
    mix_b = jnp.concatenate([mix_b_p, mix_b_s], axis=0)

    n_le = w_gu.shape[0] * N_EXPERTS
    wgu = _deinterleave_gu(w_gu.reshape(n_le, D_MODEL, 2 * D_EXPERT))
    bgu = jnp.concatenate([b_gu[..., 0::2], b_gu[..., 1::2]], axis=-1).reshape(n_le, 1, 2 * D_EXPERT)
    wdn = w_down.reshape(n_le, D_EXPERT, D_MODEL).astype(BF16)
    bdn = b_down.reshape(n_le, 1, D_MODEL)

    def token_stage(layer, mixes, w_out, x_in):
        wr_f = jnp.transpose(w_router[layer])
        wr_hi = wr_f.astype(BF16)
        wr_t = jnp.concatenate([wr_hi, (wr_f - wr_hi.astype(F32)).astype(BF16)], axis=0)
        br = b_router[layer].reshape(N_EXPERTS, 1)
        w_out = w_out.astype(BF16)
        ntile = x_in.shape[0] // TOKEN_TILE
        bounds = [ntile * s // MOE_SPLIT for s in range(MOE_SPLIT + 1)]
        out = None
        for t0, t1 in zip(bounds[:-1], bounds[1:]):
            x1, x1b, top_i, top_g, rank, sizes = _outproj_ln_router(
                mixes, w_out, x_in, ln_g[layer, 0].reshape(1, D_MODEL), ln_b[layer, 0].reshape(1, D_MODEL),
                wr_t, br, t0, t1 - t0)
            out = _moe_layer(x1, x1b, top_i, top_g, rank, sizes, layer * N_EXPERTS, wgu, bgu, wdn, bdn,
                             ln_g[layer, 1].reshape(1, D_MODEL), ln_b[layer, 1].reshape(1, D_MODEL),
                             out, t0, x_in.shape[0])
        return out

    x_all = token_stage(0, [mix_a, mix_b], w_out_ab[j], x_all)

    q_scale = jnp.where(jnp.arange(3 * D_MODEL) < D_MODEL, DH_C ** -0.5 * LOG2_E, 1.0).astype(F32)
    w_c = (w_in_c[j] * q_scale).astype(BF16)

    def c_group(row0, bsz, seq, tm, past_kv, tq, q_pos0):
        q, k_new, v_new = _inproj_c(x_all, row0, bsz, seq, tm, w_c)
        if past_kv is None:
            k_all, v_all = k_new, v_new
        else:
            k_all = jnp.concatenate([past_kv[0], k_new], axis=2)
            v_all = jnp.concatenate([past_kv[1], v_new], axis=2)
        o = _stick_breaking(q.reshape(bsz, seq, D_MODEL), k_all, v_all, tq=tq, q_pos0=q_pos0)
        return o.reshape(bsz * seq, D_MODEL), (k_new, v_new)

    mix_p, c_p = c_group(0, bp, lp, tm_p, None, 2 * KEY_TILE, 0)
    mix_s, c_s = c_group(n_p, bs, ss, ss, (cache_sb_k[j], cache_sb_v[j]), ss, past)
    x_all = token_stage(1, [jnp.concatenate([mix_p, mix_s], axis=0)], w_out_c[j], x_all)

    y_prompt = x_all[:n_p].reshape(bp, lp, D_MODEL)[:, N_META:]
    y_sample = x_all[n_p:].reshape(bs, ss, D_MODEL)
    stack = lambda t: t[None]
    return (y_prompt, y_sample,
            stack(ab_p[0]), stack(ab_p[1]), stack(ab_p[2]), stack(ab_p[3]), stack(ab_p[4]), stack(ab_p[5]),
            stack(c_p[0]), stack(c_p[1]),
            stack(ab_s[0]), stack(ab_s[1]), stack(ab_s[2]), stack(ab_s[3]), stack(ab_s[4]), stack(ab_s[5]),
            stack(c_s[0]), stack(c_s[1]))
```
